```python
import jax, jax.numpy as jnp
from jax import lax
import numpy as np

D_MODEL = 1024
BATCH = 8
SEQ = 2048
DEPTH = 4
DEC_BATCH = 128
DEC_SEQ = 4
PAST_LEN = 8192
PAGE_SIZE = 128

N_MIXERS = 4
N_SB = len(range(0, DEPTH, N_MIXERS))
N_MLA = len(range(1, DEPTH, N_MIXERS))
N_DIFF = len(range(2, DEPTH, N_MIXERS))
N_NSA = len(range(3, DEPTH, N_MIXERS))
ALPHA = (2 * DEPTH) ** 0.25
OUT_INIT_SCALE = (8 * DEPTH) ** -0.25
Q_BLOCK = 128
NORM_EPS = 1e-5
NEG_INF = -1e30
FORCE_SCORE = 1e9
ROPE_THETA = 500000.0

SB_HEADS = 16
SB_KV_HEADS = 4
SB_HEAD_DIM = 64

MLA_HEADS = 16
MLA_Q_RANK = 256
MLA_KV_RANK = 128
MLA_NOPE_DIM = 64
MLA_ROPE_DIM = 32
MLA_V_DIM = 64
MLA_ROPE_THETA = 10000.0

DIFF_HEADS = 8
DIFF_KV_HEADS = 2
DIFF_HEAD_DIM = 64
DIFF_V_DIM = 2 * DIFF_HEAD_DIM

NSA_HEADS = 16
NSA_KV_GROUPS = 2
NSA_HEAD_DIM = 64
CMP_BLOCK = 32
CMP_STRIDE = 16
CMP_HIDDEN = 128
SEL_BLOCK = 64
N_SELECT = 16
WINDOW = 512

SB_IN = 2 * SB_HEADS * SB_HEAD_DIM + 2 * SB_KV_HEADS * SB_HEAD_DIM
MLA_IN = MLA_Q_RANK + MLA_KV_RANK + MLA_ROPE_DIM + MLA_HEADS * MLA_V_DIM
DIFF_IN = DIFF_HEADS * 2 * DIFF_HEAD_DIM + DIFF_KV_HEADS * 2 * DIFF_HEAD_DIM + DIFF_KV_HEADS * DIFF_V_DIM + DIFF_HEADS * DIFF_V_DIM
NSA_IN = 2 * NSA_HEADS * NSA_HEAD_DIM + 6 * NSA_KV_GROUPS * NSA_HEAD_DIM + 3 * NSA_HEADS

kernel_name = 'hybrid_sb_mla_diff_nsa_decoder_step'

F32 = jnp.float32


def layer_norm(x, g, b):
    xf = x.astype(F32)
    mu = jnp.mean(xf, axis=-1, keepdims=True)
    var = jnp.mean(jnp.square(xf - mu), axis=-1, keepdims=True)
    return ((xf - mu) * lax.rsqrt(var + NORM_EPS) * g + b).astype(x.dtype)


def rms_norm(x, g):
    xf = x.astype(F32)
    return (xf * lax.rsqrt(jnp.mean(xf * xf, axis=-1, keepdims=True) + NORM_EPS) * g).astype(x.dtype)


def rope(x, pos, rot_dim, theta):
    half = rot_dim // 2
    inv = theta ** (-jnp.arange(half, dtype=F32) / half)
    ang = pos.astype(F32)[:, None] * inv
    ang = ang.reshape(ang.shape[0], *([1] * (x.ndim - 3)), half)
    cos, sin = jnp.cos(ang).astype(x.dtype), jnp.sin(ang).astype(x.dtype)
    x1, x2, rest = x[..., :half], x[..., half:rot_dim], x[..., rot_dim:]
    return jnp.concatenate([x1 * cos - x2 * sin, x2 * cos + x1 * sin, rest], axis=-1)


def masked_softmax(s, mask):
    return jax.nn.softmax(jnp.where(mask, s, NEG_INF), axis=-1)


def map_query_blocks(fn, *xs):
    nb = xs[0].shape[1] // Q_BLOCK
    split = lambda x: jnp.moveaxis(x.reshape(x.shape[0], nb, Q_BLOCK, *x.shape[2:]), 1, 0)
    out = lax.map(lambda a: fn(*a), tuple(split(x) for x in xs))
    merge = lambda y: jnp.moveaxis(y, 0, 1).reshape(y.shape[1], nb * Q_BLOCK, *y.shape[3:])
    return jax.tree_util.tree_map(merge, out)


def attend_queries(fn, qs, q_pos):
    n_q = q_pos.shape[0]
    if n_q > Q_BLOCK and n_q % Q_BLOCK == 0:
        return map_query_blocks(lambda *a: fn(*a[:-1], a[-1][0]), *qs, q_pos[None, :])
    return fn(*qs, q_pos)


def gather_pages(pool, layer, page_table):
    g = pool[layer, page_table]
    return g.reshape(g.shape[0], g.shape[1] * g.shape[2], *g.shape[3:])


def adaln(c, w, b):
    mod = (jax.nn.silu(c) @ w + b)[:, None, :]
    return jnp.split(mod, 3, axis=-1)


def sb_mixer(h, pos, past_kv, w_in, w_out):
    B, T, _ = h.shape
    H, G, D = SB_HEADS, SB_KV_HEADS, SB_HEAD_DIM
    q, kv, z = jnp.split(h @ w_in, [H * D, H * D + 2 * G * D], axis=-1)
    q = q.reshape(B, T, H, D)
    new_rows = kv.reshape(B, T, 2, G, D)
    rows = new_rows if past_kv is None else jnp.concatenate([past_kv, new_rows], axis=1)
    k, v = rows[:, :, 0].astype(F32), rows[:, :, 1].astype(F32)
    k_pos = jnp.arange(rows.shape[1])

    def attend(qb, qp):
        qg = qb.reshape(B, qb.shape[1], G, H // G, D).astype(F32)
        zs = jnp.einsum('bqgnd,btgd->bgnqt', qg, k) * D ** -0.5
        mask = k_pos[None, :] < qp[:, None]
        u = jnp.where(mask, jax.nn.softplus(zs), 0.0)
        log_remaining = lax.cumsum(u, axis=4, reverse=True) - u
        a = jnp.where(mask, jnp.exp(jax.nn.log_sigmoid(zs) - log_remaining), 0.0)
        o = jnp.einsum('bgnqt,btgd->bqgnd', a, v)
        return o.reshape(B, qb.shape[1], H * D).astype(h.dtype)

    o = attend_queries(attend, (q,), pos)
    return (o * jax.nn.silu(z)) @ w_out, new_rows


def mla_mixer(h, pos, past_lat, w_in, q_norm, w_q_up, kv_norm, w_uk, w_uv, w_out):
    B, T, _ = h.shape
    H, R, DN, DR, DV = MLA_HEADS, MLA_KV_RANK, MLA_NOPE_DIM, MLA_ROPE_DIM, MLA_V_DIM
    cq, ckv, kpe, z = jnp.split(h @ w_in, [MLA_Q_RANK, MLA_Q_RANK + R, MLA_Q_RANK + R + DR], axis=-1)
    q = (rms_norm(cq, q_norm) @ w_q_up).reshape(B, T, H, DN + DR)
    q_nope, q_pe = q[..., :DN], rope(q[..., DN:], pos, DR, MLA_ROPE_THETA)
    new_rows = jnp.concatenate([rms_norm(ckv, kv_norm), rope(kpe, pos, DR, MLA_ROPE_THETA)], axis=-1)
    rows = new_rows if past_lat is None else jnp.concatenate([past_lat, new_rows], axis=1)
    lat, kpe_all = rows[..., :R].astype(F32), rows[..., R:].astype(F32)
    k_pos = jnp.arange(rows.shape[1])
    q_lat = jnp.einsum('bqhn,rhn->bqhr', q_nope, w_uk)
    scale = (DN + DR) ** -0.5

    def attend(ql, qpe, qp):
        s = (jnp.einsum('bqhr,btr->bhqt', ql.astype(F32), lat)
             + jnp.einsum('bqhe,bte->bhqt', qpe.astype(F32), kpe_all)) * scale
        p = masked_softmax(s, k_pos[None, :] <= qp[:, None])
        return jnp.einsum('bhqt,btr->bqhr', p, lat).astype(h.dtype)

    o_lat = attend_queries(attend, (q_lat, q_pe), pos)
    o = jnp.einsum('bqhr,rhv->bqhv', o_lat, w_uv).reshape(B, T, H * DV)
    return (o * jax.nn.silu(z)) @ w_out, new_rows


def diff_mixer(h, pos, past_kv, layer_idx, w_in, lam, subln, w_out):
    B, T, _ = h.shape
    H, G, D, DV = DIFF_HEADS, DIFF_KV_HEADS, DIFF_HEAD_DIM, DIFF_V_DIM
    n1 = H * 2 * D
    n2 = n1 + G * 2 * D
    n3 = n2 + G * DV
    q, k, v, z = jnp.split(h @ w_in, [n1, n2, n3], axis=-1)
    q = rope(q.reshape(B, T, H, 2, D), pos, D // 4, ROPE_THETA)
    k = rope(k.reshape(B, T, G, 2, D), pos, D // 4, ROPE_THETA)
    new_rows = jnp.stack([k.reshape(B, T, G, 2 * D), v.reshape(B, T, G, DV)], axis=2)
    rows = new_rows if past_kv is None else jnp.concatenate([past_kv, new_rows], axis=1)
    k_all = rows[:, :, 0].reshape(B, rows.shape[1], G, 2, D).astype(F32)
    v_all = rows[:, :, 1].astype(F32)
    k_pos = jnp.arange(rows.shape[1])
    lam_init = 0.8 - 0.6 * float(np.exp(-0.3 * layer_idx))
    lam_f = lam.astype(F32)
    lam_full = jnp.exp(jnp.sum(lam_f[0] * lam_f[1])) - jnp.exp(jnp.sum(lam_f[2] * lam_f[3])) + lam_init

    def attend(qb, qp):
        qg = qb.reshape(B, qb.shape[1], G, H // G, 2, D).astype(F32)
        s = jnp.einsum('bqgncd,btgcd->bgncqt', qg, k_all) * D ** -0.5
        p = masked_softmax(s, k_pos[None, :] <= qp[:, None])
        a = p[:, :, :, 0] - lam_full * p[:, :, :, 1]
        o = jnp.einsum('bgnqt,btgv->bqgnv', a, v_all)
        return o.reshape(B, qb.shape[1], H, DV).astype(h.dtype)

    o = attend_queries(attend, (q,), pos)
    o = (rms_norm(o, subln) * (1.0 - lam_init)).reshape(B, T, H * DV)
    return (o * jax.nn.silu(z)) @ w_out, new_rows


def compress_rows(x, pe, w1, b1, w2, b2):
    B, Tk, G, D = x.shape
    x = jnp.pad(x, ((0, 0), (0, (-Tk) % CMP_STRIDE), (0, 0), (0, 0)))
    ch = x.reshape(B, -1, CMP_STRIDE, G, D)
    w1a, w1b = w1[:CMP_STRIDE], w1[CMP_STRIDE:]
    first = jnp.einsum('bcpgd,pdh->bcgh', ch, w1a) + jnp.einsum('pd,pdh->h', pe[:CMP_STRIDE], w1a)
    second = jnp.einsum('bcpgd,pdh->bcgh', ch, w1b) + jnp.einsum('pd,pdh->h', pe[CMP_STRIDE:], w1b)
    hid = jax.nn.silu(first[:, :-1] + second[:, 1:] + b1)
    return hid @ w2 + b2


def nsa_mixer(h, pos, past_kv, win_buf, w_in, cmp_pe, cmp_w1, cmp_b1, cmp_w2, cmp_b2, w_out):
    B, T, _ = h.shape
    H, G, D = NSA_HEADS, NSA_KV_GROUPS, NSA_HEAD_DIM
    N = H // G
    rot = D // 4
    scale = D ** -0.5
    q, kv, wkv, gl, z = jnp.split(h @ w_in, np.cumsum([H * D, 4 * G * D, 2 * G * D, 3 * H]).tolist(), axis=-1)
    q = rope(q.reshape(B, T, H, D), pos, rot, ROPE_THETA)
    kv = kv.reshape(B, T, 4, G, D)
    new_rows = jnp.stack([kv[:, :, 0], kv[:, :, 1], rope(kv[:, :, 2], pos, rot, ROPE_THETA), kv[:, :, 3]], axis=2)
    wkv = wkv.reshape(B, T, 2, G, D)
    new_win = jnp.stack([rope(wkv[:, :, 0], pos, rot, ROPE_THETA), wkv[:, :, 1]], axis=2)
    rows = new_rows if past_kv is None else jnp.concatenate([past_kv, new_rows], axis=1)
    Tk = rows.shape[1]

    k_cmp = compress_rows(rows[:, :, 0], cmp_pe[0], cmp_w1[0], cmp_b1[0], cmp_w2[0], cmp_b2[0]).astype(F32)
    v_cmp = compress_rows(rows[:, :, 1], cmp_pe[1], cmp_w1[1], cmp_b1[1], cmp_w2[1], cmp_b2[1]).astype(F32)
    n_cmp = k_cmp.shape[1]
    cmp_end = jnp.arange(n_cmp) * CMP_STRIDE + CMP_BLOCK - 1

    sel_pad = (-Tk) % SEL_BLOCK
    to_blocks = lambda x: jnp.pad(x, ((0, 0), (0, sel_pad), (0, 0), (0, 0))).reshape(B, -1, SEL_BLOCK, G, D).transpose(0, 3, 1, 2, 4)
    k_sel, v_sel = to_blocks(rows[:, :, 2]), to_blocks(rows[:, :, 3])
    n_sel = k_sel.shape[2]
    ci = jnp.arange(n_cmp)[:, None] * CMP_STRIDE
    sj = jnp.arange(n_sel)[None, :] * SEL_BLOCK
    overlap = ((ci < sj + SEL_BLOCK) & (ci + CMP_BLOCK > sj)).astype(F32)
    k_eff = min(N_SELECT, n_sel)
    blk = jnp.arange(n_sel)
    b_i = jnp.arange(B)[:, None, None, None]
    g_i = jnp.arange(G)[None, :, None, None]

    def cmp_sel(qb, qp):
        Qb = qb.shape[1]
        qg = qb.reshape(B, Qb, G, N, D).astype(F32)
        s = jnp.einsum('bqgnd,bcgd->bgnqc', qg, k_cmp) * scale
        m_c = cmp_end[None, :] <= qp[:, None]
        p_c = masked_softmax(s, m_c) * jnp.any(m_c, axis=-1)[:, None].astype(F32)
        o_c = jnp.einsum('bgnqc,bcgd->bqgnd', p_c, v_cmp)
        imp = jnp.einsum('bgnqc,cs->bgqs', p_c, overlap)
        cur = qp // SEL_BLOCK
        forced = (blk[None, :] == 0) | (blk[None, :] == cur[:, None]) | (blk[None, :] == cur[:, None] - 1)
        score = jnp.where(blk[None, :] > cur[:, None], NEG_INF, jnp.where(forced, FORCE_SCORE, imp))
        vals, idx = lax.top_k(score, k_eff)
        valid = vals > NEG_INF / 2
        ks = k_sel[b_i, g_i, idx].astype(F32)
        vs = v_sel[b_i, g_i, idx].astype(F32)
        kpos = idx[..., None] * SEL_BLOCK + jnp.arange(SEL_BLOCK)
        m_s = valid[..., None] & (kpos <= qp[None, None, :, None, None])
        s2 = jnp.einsum('bqgnd,bgqkpd->bgnqkp', qg, ks) * scale
        p2 = masked_softmax(s2.reshape(B, G, N, Qb, -1), m_s.reshape(B, G, 1, Qb, -1)).reshape(s2.shape)
        o_s = jnp.einsum('bgnqkp,bgqkpd->bqgnd', p2, vs)
        return o_c.reshape(B, Qb, H, D), o_s.reshape(B, Qb, H, D)

    def window_attend(qb, keys, qp, kpos):
        Qb = qb.shape[1]
        qg = qb.reshape(B, Qb, G, N, D).astype(F32)
        s = jnp.einsum('bqgnd,btgd->bgnqt', qg, keys[:, :, 0].astype(F32)) * scale
        diff = qp[:, None] - kpos[None, :]
        mask = (diff >= 0) & (diff < WINDOW) & (kpos[None, :] >= 0)
        p = masked_softmax(s, mask)
        return jnp.einsum('bgnqt,btgd->bqgnd', p, keys[:, :, 1].astype(F32)).reshape(B, Qb, H, D)

    if win_buf is None:
        wpad = jnp.pad(new_win, ((0, 0), (WINDOW, 0), (0, 0), (0, 0), (0, 0)))

        def win_fn(qb, qp):
            n_k = qb.shape[1] + WINDOW
            keys = lax.dynamic_slice_in_dim(wpad, qp[0], n_k, axis=1)
            return window_attend(qb, keys, qp, qp[0] - WINDOW + jnp.arange(n_k))

        new_buf = new_win[:, -min(WINDOW, T):]
    else:
        wrows = jnp.concatenate([win_buf, new_win], axis=1)
        kpos_w = pos[0] - win_buf.shape[1] + jnp.arange(wrows.shape[1])
        win_fn = lambda qb, qp: window_attend(qb, wrows, qp, kpos_w)
        new_buf = wrows[:, -min(WINDOW, wrows.shape[1]):]

    o_c, o_s = attend_queries(cmp_sel, (q,), pos)
    o_w = attend_queries(win_fn, (q,), pos)
    g = jax.nn.sigmoid(gl.reshape(B, T, H, 3).astype(F32))
    o = (g[..., 0:1] * o_c + g[..., 1:2] * o_s + g[..., 2:3] * o_w).astype(h.dtype).reshape(B, T, H * D)
    return (o * jax.nn.silu(z)) @ w_out, new_rows, new_buf


def setup_inputs(seed: int = 0) -> dict:
    key = jax.random.key(seed)
    ks = iter(jax.random.split(key, 48))

    def nrm(shape, scale=1.0):
        return scale * jax.random.normal(next(ks), shape, F32)

    D = D_MODEL
    n_pages = PAST_LEN // PAGE_SIZE
    used = DEC_BATCH * n_pages
    n_pool = used + max(1, used // 4)
    win_buf = min(WINDOW, PAST_LEN)
    page_table = jax.random.permutation(next(ks), n_pool)[:used].reshape(DEC_BATCH, n_pages).astype(jnp.int32)
    return {
        'x_prompt': nrm((BATCH, SEQ, D)),
        'x_sample': nrm((DEC_BATCH, DEC_SEQ, D)),
        'cache_sb_kv': nrm((N_SB, n_pool, PAGE_SIZE, 2, SB_KV_HEADS, SB_HEAD_DIM)),
        'cache_mla_latent': nrm((N_MLA, n_pool, PAGE_SIZE, MLA_KV_RANK + MLA_ROPE_DIM)),
        'cache_diff_kv': nrm((N_DIFF, n_pool, PAGE_SIZE, 2, DIFF_KV_HEADS, DIFF_V_DIM)),
        'cache_nsa_kv': nrm((N_NSA, n_pool, PAGE_SIZE, 4, NSA_KV_GROUPS, NSA_HEAD_DIM)),
        'state_nsa_window': nrm((N_NSA, DEC_BATCH, win_buf, 2, NSA_KV_GROUPS, NSA_HEAD_DIM)),
        'page_table': page_table,
        'c_prompt': nrm((BATCH, D)),
        'c_sample': nrm((DEC_BATCH, D)),
        'ada_w': nrm((DEPTH, D, 3 * D), 0.2 * D ** -0.5),
        'ada_b': nrm((DEPTH, 3 * D), 0.01),
        'ln_g': 1.0 + nrm((DEPTH, D), 0.01),
        'ln_b': nrm((DEPTH, D), 0.01),
        'sb_w_in': nrm((N_SB, D, SB_IN), D ** -0.5),
        'sb_w_out': nrm((N_SB, SB_HEADS * SB_HEAD_DIM, D), OUT_INIT_SCALE * (SB_HEADS * SB_HEAD_DIM) ** -0.5),
        'mla_w_in': nrm((N_MLA, D, MLA_IN), D ** -0.5),
        'mla_q_norm': 1.0 + nrm((N_MLA, MLA_Q_RANK), 0.01),
        'mla_w_q_up': nrm((N_MLA, MLA_Q_RANK, MLA_HEADS * (MLA_NOPE_DIM + MLA_ROPE_DIM)), MLA_Q_RANK ** -0.5),
        'mla_kv_norm': 1.0 + nrm((N_MLA, MLA_KV_RANK), 0.01),
        'mla_w_uk': nrm((N_MLA, MLA_KV_RANK, MLA_HEADS, MLA_NOPE_DIM), MLA_KV_RANK ** -0.5),
        'mla_w_uv': nrm((N_MLA, MLA_KV_RANK, MLA_HEADS, MLA_V_DIM), MLA_KV_RANK ** -0.5),
        'mla_w_out': nrm((N_MLA, MLA_HEADS * MLA_V_DIM, D), OUT_INIT_SCALE * (MLA_HEADS * MLA_V_DIM) ** -0.5),
        'diff_w_in': nrm((N_DIFF, D, DIFF_IN), D ** -0.5),
        'diff_lam': nrm((N_DIFF, 4, DIFF_HEAD_DIM), 0.1),
        'diff_subln': 1.0 + nrm((N_DIFF, DIFF_V_DIM), 0.01),
        'diff_w_out': nrm((N_DIFF, DIFF_HEADS * DIFF_V_DIM, D), OUT_INIT_SCALE * (DIFF_HEADS * DIFF_V_DIM) ** -0.5),
        'nsa_w_in': nrm((N_NSA, D, NSA_IN), D ** -0.5),
        'nsa_cmp_pe': nrm((N_NSA, 2, CMP_BLOCK, NSA_HEAD_DIM), 0.02),
        'nsa_cmp_w1': nrm((N_NSA, 2, CMP_BLOCK, NSA_HEAD_DIM, CMP_HIDDEN), (CMP_BLOCK * NSA_HEAD_DIM) ** -0.5),
        'nsa_cmp_b1': nrm((N_NSA, 2, CMP_HIDDEN), 0.01),
        'nsa_cmp_w2': nrm((N_NSA, 2, CMP_HIDDEN, NSA_HEAD_DIM), CMP_HIDDEN ** -0.5),
        'nsa_cmp_b2': nrm((N_NSA, 2, NSA_HEAD_DIM), 0.01),
        'nsa_w_out': nrm((N_NSA, NSA_HEADS * NSA_HEAD_DIM, D), OUT_INIT_SCALE * (NSA_HEADS * NSA_HEAD_DIM) ** -0.5),
    }


def reference(x_prompt, x_sample, cache_sb_kv, cache_mla_latent, cache_diff_kv, cache_nsa_kv, state_nsa_window,
              page_table, c_prompt, c_sample, ada_w, ada_b, ln_g, ln_b, sb_w_in, sb_w_out,
              mla_w_in, mla_q_norm, mla_w_q_up, mla_kv_norm, mla_w_uk, mla_w_uv, mla_w_out,
              diff_w_in, diff_lam, diff_subln, diff_w_out,
              nsa_w_in, nsa_cmp_pe, nsa_cmp_w1, nsa_cmp_b1, nsa_cmp_w2, nsa_cmp_b2, nsa_w_out):
    past_len = page_table.shape[1] * cache_sb_kv.shape[2]
    pos_p = jnp.arange(x_prompt.shape[1], dtype=jnp.int32)
    pos_s = past_len + jnp.arange(x_sample.shape[1], dtype=jnp.int32)
    xp, xs = x_prompt, x_sample
    sb_p, sb_s, mla_p, mla_s, diff_p, diff_s, nsa_p, nsa_s, win_p, win_s = ([] for _ in range(10))
    for i in range(DEPTH):
        m, j = i % N_MIXERS, i // N_MIXERS
        sh_p, sc_p, gt_p = adaln(c_prompt, ada_w[i], ada_b[i])
        sh_s, sc_s, gt_s = adaln(c_sample, ada_w[i], ada_b[i])
        hp = xp * (1.0 + sc_p) + sh_p
        hs = xs * (1.0 + sc_s) + sh_s
        if m == 0:
            op, rp = sb_mixer(hp, pos_p, None, sb_w_in[j], sb_w_out[j])
            os_, rs = sb_mixer(hs, pos_s, gather_pages(cache_sb_kv, j, page_table), sb_w_in[j], sb_w_out[j])
            sb_p.append(rp)
            sb_s.append(rs)
        elif m == 1:
            w = (mla_w_in[j], mla_q_norm[j], mla_w_q_up[j], mla_kv_norm[j], mla_w_uk[j], mla_w_uv[j], mla_w_out[j])
            op, rp = mla_mixer(hp, pos_p, None, *w)
            os_, rs = mla_mixer(hs, pos_s, gather_pages(cache_mla_latent, j, page_table), *w)
            mla_p.append(rp)
            mla_s.append(rs)
        elif m == 2:
            w = (diff_w_in[j], diff_lam[j], diff_subln[j], diff_w_out[j])
            op, rp = diff_mixer(hp, pos_p, None, i, *w)
            os_, rs = diff_mixer(hs, pos_s, gather_pages(cache_diff_kv, j, page_table), i, *w)
            diff_p.append(rp)
            diff_s.append(rs)
        else:
            w = (nsa_w_in[j], nsa_cmp_pe[j], nsa_cmp_w1[j], nsa_cmp_b1[j], nsa_cmp_w2[j], nsa_cmp_b2[j], nsa_w_out[j])
            op, rp, bp = nsa_mixer(hp, pos_p, None, None, *w)
            os_, rs, bs = nsa_mixer(hs, pos_s, gather_pages(cache_nsa_kv, j, page_table), state_nsa_window[j], *w)
            nsa_p.append(rp)
            nsa_s.append(rs)
            win_p.append(bp)
            win_s.append(bs)
        xp = layer_norm(ALPHA * xp + (1.0 + gt_p) * op, ln_g[i], ln_b[i])
        xs = layer_norm(ALPHA * xs + (1.0 + gt_s) * os_, ln_g[i], ln_b[i])
    return (xp, xs, jnp.stack(sb_p), jnp.stack(sb_s), jnp.stack(mla_p), jnp.stack(mla_s),
            jnp.stack(diff_p), jnp.stack(diff_s), jnp.stack(nsa_p), jnp.stack(nsa_s),
            jnp.stack(win_p), jnp.stack(win_s))
```

```python
import functools
import math

import numpy as np
import jax
import jax.numpy as jnp
from jax import lax
from jax.experimental import pallas as pl
from jax.experimental.pallas import tpu as pltpu

F32 = jnp.float32
BF16 = jnp.bfloat16

DEPTH = 4
ALPHA = (2 * DEPTH) ** 0.25
NORM_EPS = 1e-5
NEG_INF = -1e30
FORCE_SCORE = 1e9
ROPE_THETA = 500000.0

SB_HEADS, SB_KV_HEADS, SB_HEAD_DIM = 16, 4, 64
MLA_HEADS, MLA_Q_RANK, MLA_KV_RANK = 16, 256, 128
MLA_NOPE_DIM, MLA_ROPE_DIM, MLA_V_DIM = 64, 32, 64
MLA_ROPE_THETA = 10000.0
DIFF_HEADS, DIFF_KV_HEADS, DIFF_HEAD_DIM = 8, 2, 64
NSA_HEADS, NSA_KV_GROUPS, NSA_HEAD_DIM = 16, 2, 64
CMP_BLOCK, CMP_STRIDE, CMP_HIDDEN = 32, 16, 128
SEL_BLOCK, N_SELECT, WINDOW = 64, 16, 512

LANES = 128
VMEM_LIMIT = 48 * 1024 * 1024
PAGES_PER_STEP = 8
TQ = 256
ROW_TILE = 256


def _cparams(sem):
    return pltpu.CompilerParams(dimension_semantics=sem, vmem_limit_bytes=VMEM_LIMIT)


def _dot(a, b):
    return jnp.dot(a, b, preferred_element_type=F32)


def _dot_nt(a, b):
    return lax.dot_general(a, b, (((1,), (1,)), ((), ())), preferred_element_type=F32)


def _silu(x):
    return x / (1.0 + jnp.exp(-x))


def _sigmoid(x):
    return 1.0 / (1.0 + jnp.exp(-x))


def _split_dot(x, w_bf16):
    hi = x.astype(BF16)
    lo = (x - hi.astype(F32)).astype(BF16)
    return _dot(hi, w_bf16) + _dot(lo, w_bf16)


def _rope128(x, c, s1, s2, half):
    return x * c + pltpu.roll(x, LANES - half, 1) * s1 + pltpu.roll(x, half, 1) * s2


def _rope_tables(pos, rot_dim, theta, period):
    half = rot_dim // 2
    inv = theta ** (-jnp.arange(half, dtype=F32) / half)
    ang = pos.astype(F32)[:, None] * inv
    cos, sin = jnp.cos(ang), jnp.sin(ang)
    n = pos.shape[0]
    ones = jnp.ones((n, period - rot_dim), F32)
    zeros_h = jnp.zeros((n, half), F32)
    zeros_r = jnp.zeros((n, period - rot_dim), F32)
    c = jnp.concatenate([cos, cos, ones], axis=1)
    s1 = jnp.concatenate([-sin, zeros_h, zeros_r], axis=1)
    s2 = jnp.concatenate([zeros_h, sin, zeros_r], axis=1)
    rep = LANES // period
    return tuple(jnp.tile(t, (1, rep))[None] for t in (c, s1, s2))


def _adaln_body(c_ref, w_ref, b_ref, o_ref):
    s = _silu(c_ref[...]).astype(BF16)
    o_ref[0] = _dot(s, w_ref[0].astype(BF16)) + b_ref[0]


def _adaln(c_all, ada_w, ada_b):
    depth, d, d3 = ada_w.shape
    n = c_all.shape[0]
    nt = d3 // d
    return pl.pallas_call(
        _adaln_body,
        out_shape=jax.ShapeDtypeStruct((depth, n, d3), F32),
        grid=(depth, nt),
        in_specs=[pl.BlockSpec((n, d), lambda l, j: (0, 0)),
                  pl.BlockSpec((1, d, d), lambda l, j: (l, 0, j)),
                  pl.BlockSpec((1, 1, d), lambda l, j: (l, 0, j))],
        out_specs=pl.BlockSpec((1, n, d), lambda l, j: (l, 0, j)),
        compiler_params=_cparams(("arbitrary", "arbitrary")),
    )(c_all, ada_w, ada_b.reshape(depth, 1, d3))


def _row_call(body, x, mods, row_inputs, full_inputs, out_defs, tm):
    bt, r, _ = x.shape
    tm = min(tm, r)
    grid = (bt, r // tm)

    def tiled(a):
        lead = a.shape[0]
        if a.shape[1] == 1:
            return pl.BlockSpec((1, 1, a.shape[2]), lambda b, i: (b, 0, 0))
        if lead == 1 and bt > 1:
            return pl.BlockSpec((1, tm, a.shape[2]), lambda b, i: (0, i, 0))
        return pl.BlockSpec((1, tm, a.shape[2]), lambda b, i: (b, i, 0))

    def full(a):
        nd = a.ndim
        return pl.BlockSpec(a.shape, lambda b, i: (0,) * nd)

    in_specs = [tiled(x)] + [tiled(m) for m in mods] + [tiled(t) for t in row_inputs] + [full(w) for w in full_inputs]
    out_shape = [jax.ShapeDtypeStruct((bt, r, w), dt) for w, dt in out_defs]
    out_specs = [pl.BlockSpec((1, tm, w), lambda b, i: (b, i, 0)) for w, _ in out_defs]
    return pl.pallas_call(
        body, out_shape=out_shape, grid=grid, in_specs=in_specs, out_specs=out_specs,
        compiler_params=_cparams(("arbitrary", "arbitrary")),
    )(x, *mods, *row_inputs, *full_inputs)


def _modulated(x_ref, sh_ref, sc_ref):
    return (x_ref[0] * (1.0 + sc_ref[0]) + sh_ref[0]).astype(BF16)


def _finish(o, z_ref, x_ref, g_ref, w_ref, lng_ref, lnb_ref, y_ref):
    gated = (o * _silu(z_ref[0])).astype(BF16)
    y = _dot(gated, w_ref[...])
    r = ALPHA * x_ref[0] + (1.0 + g_ref[0]) * y
    mu = jnp.mean(r, axis=-1, keepdims=True)
    d = r - mu
    var = jnp.mean(d * d, axis=-1, keepdims=True)
    y_ref[0] = d * lax.rsqrt(var + NORM_EPS) * lng_ref[...] + lnb_ref[...]


def _out_plain_body(x_ref, g_ref, o_ref, z_ref, w_ref, lng_ref, lnb_ref, y_ref):
    _finish(o_ref[0].astype(F32), z_ref, x_ref, g_ref, w_ref, lng_ref, lnb_ref, y_ref)


def _out_proj(x, gate, o, z, w_out, ln_g, ln_b):
    d = x.shape[-1]
    (y,) = _row_call(_out_plain_body, x, [gate], [o, z], [w_out.astype(BF16), ln_g.reshape(1, d), ln_b.reshape(1, d)],
                     [(d, F32)], ROW_TILE)
    return y


def _sb_proj_body(x_ref, sh_ref, sc_ref, wq_ref, wkv_ref, wz_ref, q_ref, kv_ref, z_ref):
    h = _modulated(x_ref, sh_ref, sc_ref)
    q_ref[0] = (_dot(h, wq_ref[...]) * (SB_HEAD_DIM ** -0.5)).astype(BF16)
    kv_ref[0] = _dot(h, wkv_ref[...])
    z_ref[0] = _dot(h, wz_ref[...])


def _sb_proj(x, sh, sc, w_in):
    nq = SB_HEADS * SB_HEAD_DIM
    nkv = 2 * SB_KV_HEADS * SB_HEAD_DIM
    w = w_in.astype(BF16)
    return _row_call(_sb_proj_body, x, [sh, sc], [], [w[:, :nq], w[:, nq:nq + nkv], w[:, nq + nkv:]],
                     [(nq, BF16), (nkv, F32), (nq, F32)], ROW_TILE)


def _softplus(z):
    return jnp.maximum(z, 0.0) + jnp.log(1.0 + jnp.exp(-jnp.abs(z)))


def _sb_tile(q2, k, v, mask, tri, r_prev, keys_on_lanes=False):
    z = _dot(q2, k) if keys_on_lanes else _dot_nt(q2, k)
    sp = _softplus(z)
    u = sp if mask is None else jnp.where(mask, sp, 0.0)
    log_rem = _split_dot(u, tri) + r_prev
    a = jnp.exp(z - sp - log_rem)
    if mask is not None:
        a = jnp.where(mask, a, 0.0)
    a = a.astype(BF16)
    pv = _dot_nt(a, v) if keys_on_lanes else _dot(a, v)
    return pv, r_prev + jnp.sum(u, axis=-1, keepdims=True)


def _tri(tk):
    return (lax.broadcasted_iota(jnp.int32, (tk, tk), 0) > lax.broadcasted_iota(jnp.int32, (tk, tk), 1)).astype(BF16)


def _sb_prompt_body(q_ref, k_ref, v_ref, o_ref, acc_ref, r_ref, *, tq):
    c = pl.program_id(1)
    i = pl.program_id(2)
    half = (c // 2) % 2
    lane = lax.broadcasted_iota(jnp.int32, (tq, LANES), 1)
    in_half = (lane // SB_HEAD_DIM) == half
    q = q_ref[0].astype(F32)
    qr = pltpu.roll(q, SB_HEAD_DIM, 1)
    first = half == 0
    qa = jnp.where(in_half, jnp.where(first, q, qr), 0.0)
    qb = jnp.where(in_half, jnp.where(first, qr, q), 0.0)
    q2 = jnp.concatenate([qa, qb], axis=0).astype(BF16)
    tri = _tri(tq)
    qpos = i * tq + lax.broadcasted_iota(jnp.int32, (tq, tq), 0)
    qpos2 = jnp.concatenate([qpos, qpos], axis=0)
    koff = lax.broadcasted_iota(jnp.int32, (2 * tq, tq), 1)
    acc_ref[...] = jnp.zeros_like(acc_ref)
    r_ref[...] = jnp.zeros_like(r_ref)

    def step(jj, carry):
        j = i - jj
        start = pl.multiple_of(j * tq, tq)
        k = k_ref[0, pl.ds(start, tq), :].astype(BF16)
        v = v_ref[0, pl.ds(start, tq), :].astype(BF16)
        mask = (start + koff) < qpos2
        pv, r_new = _sb_tile(q2, k, v, mask, tri, r_ref[...])
        acc_ref[...] += pv
        r_ref[...] = r_new
        return carry

    lax.fori_loop(0, i + 1, step, 0)
    acc = acc_ref[...]
    oa, ob = acc[:tq], acc[tq:]
    oar, obr = pltpu.roll(oa, SB_HEAD_DIM, 1), pltpu.roll(ob, SB_HEAD_DIM, 1)
    low = lane < SB_HEAD_DIM
    o_ref[0] = jnp.where(low, jnp.where(first, oa, oar), jnp.where(first, obr, ob))


def _sb_prompt_attn(q, kv):
    b, t, _ = q.shape
    tq = min(TQ, t)
    nchunk = SB_HEADS * SB_HEAD_DIM // LANES
    kchunks = SB_KV_HEADS * SB_HEAD_DIM // LANES
    return pl.pallas_call(
        functools.partial(_sb_prompt_body, tq=tq),
        out_shape=jax.ShapeDtypeStruct((b, t, nchunk * LANES), F32),
        grid=(b, nchunk, t // tq),
        in_specs=[pl.BlockSpec((1, tq, LANES), lambda b_, c, i: (b_, i, c)),
                  pl.BlockSpec((1, t, LANES), lambda b_, c, i: (b_, 0, c // 4)),
                  pl.BlockSpec((1, t, LANES), lambda b_, c, i: (b_, 0, kchunks + c // 4))],
        out_specs=pl.BlockSpec((1, tq, LANES), lambda b_, c, i: (b_, i, c)),
        scratch_shapes=[pltpu.VMEM((2 * tq, LANES), F32), pltpu.VMEM((2 * tq, 1), F32)],
        compiler_params=_cparams(("arbitrary", "arbitrary", "arbitrary")),
    )(q, kv, kv)


def _paged_call(body, page_table, cache, layer, row_block, dense, out_defs, scratch, descending=False, shared=()):
    nb, npg = page_table.shape
    pps = min(PAGES_PER_STEP, npg)
    nsteps = npg // pps
    page = cache.shape[3]
    rb, ridx = row_block

    def page_spec(s):
        def imap(b, c, pt):
            p = c * pps + s
            if descending:
                p = npg - 1 - p
            return (layer, pt[b, p], ridx, 0)
        return pl.BlockSpec((1, 1, rb, page), imap)

    def shared_spec(a):
        nd = a.ndim
        return pl.BlockSpec(a.shape, lambda b, c, pt: (0,) * nd)

    in_specs = [pl.BlockSpec((1,) + a.shape[1:], lambda b, c, pt: (b, 0, 0)) for a in dense]
    in_specs += [shared_spec(a) for a in shared]
    in_specs += [page_spec(s) for s in range(pps)]
    out_shape = [jax.ShapeDtypeStruct((nb, r, w), dt) for r, w, dt in out_defs]
    out_specs = [pl.BlockSpec((1, r, w), lambda b, c, pt: (b, 0, 0)) for r, w, _ in out_defs]
    grid_spec = pltpu.PrefetchScalarGridSpec(num_scalar_prefetch=1, grid=(nb, nsteps), in_specs=in_specs,
                                             out_specs=out_specs, scratch_shapes=scratch)
    return pl.pallas_call(
        functools.partial(body, n_dense=len(dense) + len(shared), pps=pps, nsteps=nsteps),
        out_shape=out_shape, grid_spec=grid_spec,
        compiler_params=_cparams(("arbitrary", "arbitrary")),
    )(page_table, *dense, *shared, *([cache] * pps))


def _as_page(rows, page):
    return jnp.pad(rows.transpose(0, 2, 1), ((0, 0), (0, 0), (0, page - rows.shape[1])))


def _sb_sample_body(pt_ref, q_ref, new_ref, *rest, n_dense, pps, nsteps, dt):
    pages, (o_ref, acc_ref, r_ref) = rest[:pps], rest[pps:]
    c = pl.program_id(1)
    nk = SB_KV_HEADS * SB_HEAD_DIM
    q2 = q_ref[0]
    m, tk = q2.shape[0], new_ref.shape[2]
    tri = _tri(tk)

    @pl.when(c == 0)
    def _():
        row_t = (lax.broadcasted_iota(jnp.int32, (m, tk), 0) // (SB_HEADS // SB_KV_HEADS)) % dt
        key = lax.broadcasted_iota(jnp.int32, (m, tk), 1)
        mask = (key < row_t) & (key < dt)
        nw = new_ref[0]
        pv, r_new = _sb_tile(q2, nw[:nk].astype(BF16), nw[nk:].astype(BF16), mask, tri, jnp.zeros((m, 1), F32), True)
        acc_ref[...] = pv
        r_ref[...] = r_new

    for s in range(pps):
        pg = pages[s][0, 0]
        pv, r_new = _sb_tile(q2, pg[:nk].astype(BF16), pg[nk:].astype(BF16), None, tri, r_ref[...], True)
        acc_ref[...] += pv
        r_ref[...] = r_new

    @pl.when(c == nsteps - 1)
    def _():
        o_ref[0] = acc_ref[...]


def _group_block_diag(q, groups):
    b, t, f = q.shape
    d = SB_HEAD_DIM
    n = f // (groups * d)
    q5 = q.reshape(b, t, groups, n, d)
    eye = jnp.eye(groups, dtype=q.dtype)
    out = q5.transpose(0, 2, 1, 3, 4)[:, :, :, :, None, :] * eye[None, :, None, None, :, None]
    return out.reshape(b, groups * t * n, groups * d)


def _group_diag_extract(acc, groups, t):
    b, r, f = acc.shape
    w = f // groups
    n = r // (groups * t)
    a5 = acc.reshape(b, groups, t * n, groups, w)
    picked = jnp.stack([a5[:, g, :, g, :] for g in range(groups)], axis=1)
    return picked.reshape(b, groups, t, n, w).transpose(0, 2, 1, 3, 4).reshape(b, t, groups * n * w)


def _sb_sample_attn(q, new_kv, cache, layer, page_table):
    b, dt, _ = q.shape
    page = cache.shape[3]
    qbd = _group_block_diag(q, SB_KV_HEADS)
    m = qbd.shape[1]
    nk = SB_KV_HEADS * SB_HEAD_DIM
    (acc,) = _paged_call(
        functools.partial(_sb_sample_body, dt=dt), page_table, cache, layer, (2 * nk, 0),
        [qbd, _as_page(new_kv, page)], [(m, nk, F32)],
        [pltpu.VMEM((m, nk), F32), pltpu.VMEM((m, 1), F32)], descending=True)
    return _group_diag_extract(acc, SB_KV_HEADS, dt)


def _page_view(cache):
    nl, pool, page = cache.shape[:3]
    feat = math.prod(cache.shape[3:])
    return jnp.moveaxis(cache.reshape(nl, pool, page, feat), 2, 3)


def _split_mod(mod, per_row_t=None):
    d = mod.shape[-1] // 3
    parts = [mod[:, k * d:(k + 1) * d] for k in range(3)]
    if per_row_t is None:
        return [p[:, None, :] for p in parts]
    return [jnp.repeat(p, per_row_t, axis=0)[None] for p in parts]


def _sb_layer(xp, xs, mod_p, mod_s, w_in, w_out, ln_g, ln_b, cache, layer, page_table):
    db, dt, d = xs.shape
    sh_p, sc_p, gt_p = _split_mod(mod_p)
    sh_s, sc_s, gt_s = _split_mod(mod_s, dt)
    xs_flat = xs.reshape(1, db * dt, d)
    qp, kvp, zp = _sb_proj(xp, sh_p, sc_p, w_in)
    qs, kvs, zs = _sb_proj(xs_flat, sh_s, sc_s, w_in)
    op = _sb_prompt_attn(qp, kvp)
    os_ = _sb_sample_attn(qs.reshape(db, dt, -1), kvs.reshape(db, dt, -1), cache, layer, page_table)
    yp = _out_proj(xp, gt_p, op, zp, w_out, ln_g, ln_b)
    ys = _out_proj(xs_flat, gt_s, os_.reshape(1, db * dt, -1), zs, w_out, ln_g, ln_b)
    rows_shape = (2, SB_KV_HEADS, SB_HEAD_DIM)
    return (yp, ys.reshape(db, dt, d), kvp.reshape(kvp.shape[:2] + rows_shape), kvs.reshape((db, dt) + rows_shape))


def _softmax_tile(s, mask, v, m_ref, l_ref, acc_ref, keys_on_lanes=False):
    if mask is not None:
        s = jnp.where(mask, s, NEG_INF)
    m_prev = m_ref[...]
    m_new = jnp.maximum(m_prev, jnp.max(s, axis=-1, keepdims=True))
    p = jnp.exp(s - m_new)
    if mask is not None:
        p = jnp.where(mask, p, 0.0)
    scale = jnp.exp(m_prev - m_new)
    pb = p.astype(BF16)
    pv = _dot_nt(pb, v) if keys_on_lanes else _dot(pb, v)
    l_ref[...] = scale * l_ref[...] + jnp.sum(p, axis=-1, keepdims=True)
    acc_ref[...] = scale * acc_ref[...] + pv
    m_ref[...] = m_new


def _softmax_init(m_ref, l_ref, acc_ref):
    m_ref[...] = jnp.full_like(m_ref, NEG_INF)
    l_ref[...] = jnp.zeros_like(l_ref)
    acc_ref[...] = jnp.zeros_like(acc_ref)


def _softmax_scratch(m, f):
    return [pltpu.VMEM((m, 1), F32), pltpu.VMEM((m, 1), F32), pltpu.VMEM((m, f), F32)]


def _rms(x, g):
    return x * lax.rsqrt(jnp.mean(x * x, axis=-1, keepdims=True) + NORM_EPS) * g


def _mla_proj_body(x_ref, sh_ref, sc_ref, c_ref, s1_ref, s2_ref, wcq_ref, wckv_ref, wpe_ref, wz_ref, g_ref,
                   cq_ref, rows_ref, kk_ref, z_ref):
    h = _modulated(x_ref, sh_ref, sc_ref)
    cq_ref[0] = _dot(h, wcq_ref[...])
    lat = _rms(_dot(h, wckv_ref[...]), g_ref[...])
    pe = _rope128(_dot(h, wpe_ref[...]), c_ref[0], s1_ref[0], s2_ref[0], MLA_ROPE_DIM // 2)
    rows_ref[0, :, :MLA_KV_RANK] = lat
    rows_ref[0, :, MLA_KV_RANK:] = pe[:, :MLA_ROPE_DIM]
    kk_ref[0] = jnp.concatenate([lat, pe], axis=1).astype(BF16)
    z_ref[0] = _dot(h, wz_ref[...])


def _mla_proj(x, sh, sc, tables, w_in, kv_norm):
    r, pe = MLA_KV_RANK, MLA_ROPE_DIM
    w = w_in.astype(BF16)
    n0, n1, n2 = MLA_Q_RANK, MLA_Q_RANK + r, MLA_Q_RANK + r + pe
    wpe = jnp.pad(w[:, n1:n2], ((0, 0), (0, LANES - pe)))
    nz = w.shape[1] - n2
    return _row_call(_mla_proj_body, x, [sh, sc], list(tables),
                     [w[:, :n0], w[:, n0:n1], wpe, w[:, n2:], kv_norm.reshape(1, r)],
                     [(n0, F32), (r + pe, F32), (2 * LANES, BF16), (nz, F32)], ROW_TILE)


def _mla_q_body(cq_ref, c_ref, s1_ref, s2_ref, g_ref, wn_ref, wuk_ref, wp_ref, q_ref):
    nb = _rms(cq_ref[0], g_ref[...]).astype(BF16)
    scale = (MLA_NOPE_DIM + MLA_ROPE_DIM) ** -0.5
    c, s1, s2 = c_ref[0], s1_ref[0], s2_ref[0]
    for h in range(MLA_HEADS):
        nope = _dot(nb, wn_ref[h]).astype(BF16)
        q_ref[0, :, 2 * h * LANES:(2 * h + 1) * LANES] = (_dot(nope, wuk_ref[h]) * scale).astype(BF16)
        pe = _rope128(_dot(nb, wp_ref[h]), c, s1, s2, MLA_ROPE_DIM // 2)
        q_ref[0, :, (2 * h + 1) * LANES:(2 * h + 2) * LANES] = (pe * scale).astype(BF16)


def _mla_q(cq, tables, q_norm, w_q_up, w_uk):
    hds, dn, dr = MLA_HEADS, MLA_NOPE_DIM, MLA_ROPE_DIM
    wq = w_q_up.astype(BF16).reshape(MLA_Q_RANK, hds, dn + dr)
    wn = wq[:, :, :dn].transpose(1, 0, 2)
    wp = jnp.pad(wq[:, :, dn:].transpose(1, 0, 2), ((0, 0), (0, 0), (0, LANES - dr)))
    wuk = w_uk.astype(BF16).transpose(1, 2, 0)
    (q,) = _row_call(_mla_q_body, cq, [], list(tables), [q_norm.reshape(1, MLA_Q_RANK), wn, wuk, wp],
                     [(hds * 2 * LANES, BF16)], ROW_TILE)
    return q


def _causal_mask(i, j, tq, rows):
    qpos = i * tq + lax.broadcasted_iota(jnp.int32, (tq, tq), 0)
    kpos = j * tq + lax.broadcasted_iota(jnp.int32, (tq, tq), 1)
    mask = kpos <= qpos
    return mask if rows == 1 else jnp.concatenate([mask] * rows, axis=0)


def _mla_prompt_body(q_ref, kk_ref, o_ref, m_ref, l_ref, acc_ref, *, tq):
    i = pl.program_id(2)
    q = q_ref[0]
    _softmax_init(m_ref, l_ref, acc_ref)

    def step(j, carry):
        k = kk_ref[0, pl.ds(pl.multiple_of(j * tq, tq), tq), :]
        _softmax_tile(_dot_nt(q, k), _causal_mask(i, j, tq, 1), k[:, :MLA_KV_RANK], m_ref, l_ref, acc_ref)
        return carry

    lax.fori_loop(0, i + 1, step, 0)
    o_ref[0] = (acc_ref[...] / l_ref[...]).astype(BF16)


def _mla_prompt_attn(q, kk):
    b, t, _ = kk.shape
    tq = min(TQ, t)
    return pl.pallas_call(
        functools.partial(_mla_prompt_body, tq=tq),
        out_shape=jax.ShapeDtypeStruct((b, t, MLA_HEADS * MLA_KV_RANK), BF16),
        grid=(b, MLA_HEADS, t // tq),
        in_specs=[pl.BlockSpec((1, tq, 2 * LANES), lambda b_, h, i: (b_, i, h)),
                  pl.BlockSpec((1, t, 2 * LANES), lambda b_, h, i: (b_, 0, 0))],
        out_specs=pl.BlockSpec((1, tq, MLA_KV_RANK), lambda b_, h, i: (b_, i, h)),
        scratch_shapes=_softmax_scratch(tq, MLA_KV_RANK),
        compiler_params=_cparams(("arbitrary", "arbitrary", "arbitrary")),
    )(q, kk)


def _new_tile_mask(m, tk, rows_per_token, dt, inclusive):
    row_t = (lax.broadcasted_iota(jnp.int32, (m, tk), 0) // rows_per_token) % dt
    key = lax.broadcasted_iota(jnp.int32, (m, tk), 1)
    return ((key <= row_t) if inclusive else (key < row_t)) & (key < dt)


def _mla_sample_body(pt_ref, q_ref, new_ref, *rest, n_dense, pps, nsteps, dt):
    pages, (o_ref, m_ref, l_ref, acc_ref) = rest[:pps], rest[pps:]
    c = pl.program_id(1)
    q = q_ref[0]
    m, tk = q.shape[0], new_ref.shape[2]

    @pl.when(c == 0)
    def _():
        _softmax_init(m_ref, l_ref, acc_ref)
        nw = new_ref[0].astype(BF16)
        _softmax_tile(_dot(q, nw), _new_tile_mask(m, tk, MLA_HEADS, dt, True), nw[:MLA_KV_RANK], m_ref, l_ref, acc_ref, True)

    for s in range(pps):
        pg = pages[s][0, 0].astype(BF16)
        _softmax_tile(_dot(q, pg), None, pg[:MLA_KV_RANK], m_ref, l_ref, acc_ref, True)

    @pl.when(c == nsteps - 1)
    def _():
        o_ref[0] = (acc_ref[...] / l_ref[...]).astype(BF16)


def _mla_sample_attn(q, new_rows, cache, layer, page_table):
    b, dt, _ = q.shape
    feat = MLA_KV_RANK + MLA_ROPE_DIM
    q160 = q.reshape(b, dt * MLA_HEADS, 2 * LANES)[:, :, :feat]
    m = dt * MLA_HEADS
    (o,) = _paged_call(
        functools.partial(_mla_sample_body, dt=dt), page_table, cache, layer, (feat, 0),
        [q160, _as_page(new_rows, cache.shape[3])], [(m, MLA_KV_RANK, BF16)], _softmax_scratch(m, MLA_KV_RANK))
    return o.reshape(b, dt, MLA_HEADS * MLA_KV_RANK)


def _out_mla_body(x_ref, g_ref, o_ref, z_ref, wuv_ref, w_ref, lng_ref, lnb_ref, y_ref):
    _finish(_dot(o_ref[0], wuv_ref[...]), z_ref, x_ref, g_ref, w_ref, lng_ref, lnb_ref, y_ref)


def _mla_out_proj(x, gate, o_lat, z, w_uv, w_out, ln_g, ln_b):
    d = x.shape[-1]
    eye = jnp.eye(MLA_HEADS, dtype=BF16)
    wuv_bd = (w_uv.astype(BF16).transpose(1, 0, 2)[:, :, None, :] * eye[:, None, :, None]).reshape(
        MLA_HEADS * MLA_KV_RANK, MLA_HEADS * MLA_V_DIM)
    (y,) = _row_call(_out_mla_body, x, [gate], [o_lat, z],
                     [wuv_bd, w_out.astype(BF16), ln_g.reshape(1, d), ln_b.reshape(1, d)], [(d, F32)], ROW_TILE)
    return y


def _mla_layer(xp, xs, mod_p, mod_s, pos_p, pos_s, w_in, q_norm, w_q_up, kv_norm, w_uk, w_uv, w_out, ln_g, ln_b,
               cache, layer, page_table):
    db, dt, d = xs.shape
    sh_p, sc_p, gt_p = _split_mod(mod_p)
    sh_s, sc_s, gt_s = _split_mod(mod_s, dt)
    xs_flat = xs.reshape(1, db * dt, d)
    tab_p = _rope_tables(pos_p, MLA_ROPE_DIM, MLA_ROPE_THETA, MLA_ROPE_DIM)
    tab_s = _rope_tables(jnp.tile(pos_s, db), MLA_ROPE_DIM, MLA_ROPE_THETA, MLA_ROPE_DIM)
    cq_p, rows_p, kk_p, z_p = _mla_proj(xp, sh_p, sc_p, tab_p, w_in, kv_norm)
    cq_s, rows_s, _, z_s = _mla_proj(xs_flat, sh_s, sc_s, tab_s, w_in, kv_norm)
    q_p = _mla_q(cq_p, tab_p, q_norm, w_q_up, w_uk)
    q_s = _mla_q(cq_s, tab_s, q_norm, w_q_up, w_uk)
    o_p = _mla_prompt_attn(q_p, kk_p)
    rows_s = rows_s.reshape(db, dt, -1)
    o_s = _mla_sample_attn(q_s.reshape(db, dt, -1), rows_s, cache, layer, page_table)
    yp = _mla_out_proj(xp, gt_p, o_p, z_p, w_uv, w_out, ln_g, ln_b)
    ys = _mla_out_proj(xs_flat, gt_s, o_s.reshape(1, db * dt, -1), z_s, w_uv, w_out, ln_g, ln_b)
    return yp, ys.reshape(db, dt, d), rows_p, rows_s


def _diff_proj_body(x_ref, sh_ref, sc_ref, c_ref, s1_ref, s2_ref, wq_ref, wk_ref, wv_ref, wz_ref,
                    q_ref, kv_ref, z_ref):
    h = _modulated(x_ref, sh_ref, sc_ref)
    c, s1, s2 = c_ref[0], s1_ref[0], s2_ref[0]
    half = DIFF_HEAD_DIM // 8
    q = _dot(h, wq_ref[...])
    for ch in range(q.shape[1] // LANES):
        qc = _rope128(q[:, ch * LANES:(ch + 1) * LANES], c, s1, s2, half)
        q_ref[0, :, ch * LANES:(ch + 1) * LANES] = (qc * (DIFF_HEAD_DIM ** -0.5)).astype(BF16)
    k = _dot(h, wk_ref[...])
    nk = k.shape[1]
    for ch in range(nk // LANES):
        kv_ref[0, :, ch * LANES:(ch + 1) * LANES] = _rope128(k[:, ch * LANES:(ch + 1) * LANES], c, s1, s2, half)
    kv_ref[0, :, nk:] = _dot(h, wv_ref[...])
    z_ref[0] = _dot(h, wz_ref[...])


def _diff_proj(x, sh, sc, tables, w_in):
    hd, g, d = DIFF_HEADS, DIFF_KV_HEADS, DIFF_HEAD_DIM
    n1 = hd * 2 * d
    n2 = n1 + g * 2 * d
    n3 = n2 + g * 2 * d
    w = w_in.astype(BF16)
    return _row_call(_diff_proj_body, x, [sh, sc], list(tables), [w[:, :n1], w[:, n1:n2], w[:, n2:n3], w[:, n3:]],
                     [(n1, BF16), (n3 - n1, F32), (w.shape[1] - n3, F32)], ROW_TILE)


def _diff_lambda(lam, lam_init):
    a = jnp.sum(lam[0:1] * lam[1:2], axis=-1, keepdims=True)
    b = jnp.sum(lam[2:3] * lam[3:4], axis=-1, keepdims=True)
    return jnp.exp(a) - jnp.exp(b) + lam_init


def _diff_combine(acc, l, lam, sub, lam_init, n):
    o = acc[:n] / l[:n] - lam * (acc[n:] / l[n:])
    return _rms(o, sub) * (1.0 - lam_init)


def _diff_prompt_body(q_ref, k_ref, v_ref, lam_ref, sub_ref, o_ref, m_ref, l_ref, acc_ref, *, tq, lam_init):
    i = pl.program_id(2)
    lane = lax.broadcasted_iota(jnp.int32, (tq, LANES), 1)
    q = q_ref[0]
    zero = jnp.zeros_like(q)
    q2 = jnp.concatenate([jnp.where(lane < DIFF_HEAD_DIM, q, zero), jnp.where(lane >= DIFF_HEAD_DIM, q, zero)], axis=0)
    _softmax_init(m_ref, l_ref, acc_ref)

    def step(j, carry):
        start = pl.multiple_of(j * tq, tq)
        k = k_ref[0, pl.ds(start, tq), :].astype(BF16)
        v = v_ref[0, pl.ds(start, tq), :].astype(BF16)
        _softmax_tile(_dot_nt(q2, k), _causal_mask(i, j, tq, 2), v, m_ref, l_ref, acc_ref)
        return carry

    lax.fori_loop(0, i + 1, step, 0)
    o_ref[0] = _diff_combine(acc_ref[...], l_ref[...], _diff_lambda(lam_ref[...], lam_init), sub_ref[...], lam_init, tq)


def _diff_prompt_attn(q, kv, lam, subln, lam_init):
    b, t, _ = q.shape
    tq = min(TQ, t)
    hpg = DIFF_HEADS // DIFF_KV_HEADS
    return pl.pallas_call(
        functools.partial(_diff_prompt_body, tq=tq, lam_init=lam_init),
        out_shape=jax.ShapeDtypeStruct((b, t, DIFF_HEADS * LANES), F32),
        grid=(b, DIFF_HEADS, t // tq),
        in_specs=[pl.BlockSpec((1, tq, LANES), lambda b_, n, i: (b_, i, n)),
                  pl.BlockSpec((1, t, LANES), lambda b_, n, i: (b_, 0, n // hpg)),
                  pl.BlockSpec((1, t, LANES), lambda b_, n, i: (b_, 0, DIFF_KV_HEADS + n // hpg)),
                  pl.BlockSpec(lam.shape, lambda b_, n, i: (0, 0)),
                  pl.BlockSpec((1, LANES), lambda b_, n, i: (0, 0))],
        out_specs=pl.BlockSpec((1, tq, LANES), lambda b_, n, i: (b_, i, n)),
        scratch_shapes=_softmax_scratch(2 * tq, LANES),
        compiler_params=_cparams(("arbitrary", "arbitrary", "arbitrary")),
    )(q, kv, kv, lam, subln.reshape(1, LANES))


def _diff_sample_body(pt_ref, q_ref, new_ref, lam_ref, sub_ref, *rest, n_dense, pps, nsteps, dt, lam_init):
    pages, (o_ref, m_ref, l_ref, acc_ref) = rest[:pps], rest[pps:]
    c = pl.program_id(1)
    g = DIFF_KV_HEADS
    stride = 2 * g
    tk = new_ref.shape[1] // stride
    mg = q_ref.shape[1] // g
    rows = [slice(gi * mg, (gi + 1) * mg) for gi in range(g)]

    def tile(ref, lead, mask):
        for gi in range(g):
            k = ref[lead + (pl.ds(gi, tk, stride=stride), slice(None))].astype(BF16)
            v = ref[lead + (pl.ds(g + gi, tk, stride=stride), slice(None))].astype(BF16)
            r = rows[gi]
            _softmax_tile(_dot_nt(q_ref[0, r, :], k), mask, v, m_ref.at[r], l_ref.at[r], acc_ref.at[r])

    @pl.when(c == 0)
    def _():
        _softmax_init(m_ref, l_ref, acc_ref)
        tile(new_ref, (0,), _new_tile_mask(mg, tk, DIFF_HEADS // g, dt, True))

    for s in range(pps):
        tile(pages[s], (0, 0), None)

    @pl.when(c == nsteps - 1)
    def _():
        lam = _diff_lambda(lam_ref[0], lam_init)
        n = mg // 2
        for gi in range(g):
            r = rows[gi]
            o_ref[0, gi * n:(gi + 1) * n, :] = _diff_combine(acc_ref[r, :], l_ref[r, :], lam, sub_ref[0], lam_init, n)


def _diff_sample_attn(q, new_kv, cache, layer, page_table, lam, subln, lam_init):
    b, dt, _ = q.shape
    g, d = DIFF_KV_HEADS, DIFF_HEAD_DIM
    n = DIFF_HEADS // g
    page = cache.shape[2] // (2 * g)
    q6 = q.reshape(b, dt, g, n, 2, d).transpose(0, 2, 4, 1, 3, 5)
    eye = jnp.eye(2, dtype=q.dtype)
    qrows = (q6[..., None, :] * eye[None, None, :, None, None, :, None]).reshape(b, g * 2 * dt * n, 2 * d)
    new_page = jnp.pad(new_kv.reshape(b, dt, 2 * g, LANES), ((0, 0), (0, page - dt), (0, 0), (0, 0)))
    new_page = new_page.reshape(b, page * 2 * g, LANES)
    m = qrows.shape[1]
    (o,) = _paged_call(
        functools.partial(_diff_sample_body, dt=dt, lam_init=lam_init), page_table, cache, layer,
        (cache.shape[2], 0), [qrows, new_page, jnp.broadcast_to(lam, (b,) + lam.shape),
                              jnp.broadcast_to(subln.reshape(1, 1, LANES), (b, 1, LANES))],
        [(m // 2, LANES, F32)], _softmax_scratch(m, LANES))
    return o.reshape(b, g, dt, n, LANES).transpose(0, 2, 1, 3, 4).reshape(b, dt, DIFF_HEADS * LANES)


def _diff_layer(xp, xs, mod_p, mod_s, pos_p, pos_s, layer_idx, w_in, lam, subln, w_out, ln_g, ln_b,
                cache, layer, page_table):
    db, dt, d = xs.shape
    lam_init = 0.8 - 0.6 * math.exp(-0.3 * layer_idx)
    sh_p, sc_p, gt_p = _split_mod(mod_p)
    sh_s, sc_s, gt_s = _split_mod(mod_s, dt)
    xs_flat = xs.reshape(1, db * dt, d)
    tab_p = _rope_tables(pos_p, DIFF_HEAD_DIM // 4, ROPE_THETA, DIFF_HEAD_DIM)
    tab_s = _rope_tables(jnp.tile(pos_s, db), DIFF_HEAD_DIM // 4, ROPE_THETA, DIFF_HEAD_DIM)
    q_p, kv_p, z_p = _diff_proj(xp, sh_p, sc_p, tab_p, w_in)
    q_s, kv_s, z_s = _diff_proj(xs_flat, sh_s, sc_s, tab_s, w_in)
    o_p = _diff_prompt_attn(q_p, kv_p, lam, subln, lam_init)
    kv_s = kv_s.reshape(db, dt, -1)
    nl, pool, page = cache.shape[:3]
    cache_rows = cache.reshape(nl, pool, page * 2 * DIFF_KV_HEADS, LANES)
    o_s = _diff_sample_attn(q_s.reshape(db, dt, -1), kv_s, cache_rows, layer, page_table, lam, subln, lam_init)
    yp = _out_proj(xp, gt_p, o_p, z_p, w_out, ln_g, ln_b)
    ys = _out_proj(xs_flat, gt_s, o_s.reshape(1, db * dt, -1), z_s, w_out, ln_g, ln_b)
    rows_shape = (2, DIFF_KV_HEADS, 2 * DIFF_HEAD_DIM)
    return yp, ys.reshape(db, dt, d), kv_p.reshape(kv_p.shape[:2] + rows_shape), kv_s.reshape((db, dt) + rows_shape)


NSA_HPG = NSA_HEADS // NSA_KV_GROUPS
GL_PAD = LANES


def _nsa_proj_body(x_ref, sh_ref, sc_ref, c_ref, s1_ref, s2_ref, wq_ref, wkv_ref, ww_ref, wg_ref, wz_ref,
                   q_ref, kv_ref, w_ref, gl_ref, z_ref):
    h = _modulated(x_ref, sh_ref, sc_ref)
    c, s1, s2 = c_ref[0], s1_ref[0], s2_ref[0]
    half = NSA_HEAD_DIM // 8
    q = _dot(h, wq_ref[...])
    for ch in range(q.shape[1] // LANES):
        qc = _rope128(q[:, ch * LANES:(ch + 1) * LANES], c, s1, s2, half)
        q_ref[0, :, ch * LANES:(ch + 1) * LANES] = (qc * (NSA_HEAD_DIM ** -0.5)).astype(BF16)
    kv = _dot(h, wkv_ref[...])
    kv_ref[0, :, :2 * LANES] = kv[:, :2 * LANES]
    kv_ref[0, :, 2 * LANES:3 * LANES] = _rope128(kv[:, 2 * LANES:3 * LANES], c, s1, s2, half)
    kv_ref[0, :, 3 * LANES:] = kv[:, 3 * LANES:]
    w = _dot(h, ww_ref[...])
    w_ref[0, :, :LANES] = _rope128(w[:, :LANES], c, s1, s2, half)
    w_ref[0, :, LANES:] = w[:, LANES:]
    gl_ref[0] = _dot(h, wg_ref[...])
    z_ref[0] = _dot(h, wz_ref[...])


def _nsa_proj(x, sh, sc, tables, w_in):
    hd, g, d = NSA_HEADS, NSA_KV_GROUPS, NSA_HEAD_DIM
    n1 = hd * d
    n2 = n1 + 4 * g * d
    n3 = n2 + 2 * g * d
    n4 = n3 + 3 * hd
    w = w_in.astype(BF16)
    wg = jnp.pad(w[:, n3:n4], ((0, 0), (0, GL_PAD - 3 * hd)))
    return _row_call(_nsa_proj_body, x, [sh, sc], list(tables), [w[:, :n1], w[:, n1:n2], w[:, n2:n3], wg, w[:, n4:]],
                     [(n1, BF16), (n2 - n1, F32), (n3 - n2, F32), (GL_PAD, F32), (w.shape[1] - n4, F32)], ROW_TILE)


def _cmp_weights(pe, w1, b1, w2, b2):
    g = NSA_KV_GROUPS
    eye = jnp.eye(g, dtype=BF16)
    w1b = w1.astype(BF16)
    bd = (w1b[:, :, None, :, None, :] * eye[None, None, :, None, :, None]).reshape(
        2, CMP_BLOCK, g * NSA_HEAD_DIM, g * CMP_HIDDEN)
    wcat = jnp.concatenate([bd[:, :CMP_STRIDE], bd[:, CMP_STRIDE:]], axis=3)
    wcat = wcat.reshape(2, CMP_STRIDE // 2, 2 * g * NSA_HEAD_DIM, 2 * g * CMP_HIDDEN)
    w1f = w1b.reshape(2, CMP_BLOCK * NSA_HEAD_DIM, CMP_HIDDEN)
    pef = jnp.pad(pe.astype(BF16).reshape(2, 1, CMP_BLOCK * NSA_HEAD_DIM), ((0, 0), (0, 7), (0, 0)))
    w2b = w2.astype(BF16)
    w2bd = (w2b[:, None, :, None, :] * eye[None, :, None, :, None]).reshape(2, g * CMP_HIDDEN, g * NSA_HEAD_DIM)
    b2t = jnp.tile(b2, (1, g)).reshape(2, 1, g * NSA_HEAD_DIM)
    return [wcat, w1f, pef, b1.reshape(2, 1, CMP_HIDDEN), w2bd, b2t]


def _compress(get_rows, nblk, wcat_ref, w1f_ref, pe_ref, b1_ref, w2_ref, b2_ref):
    outs = []
    nh = NSA_KV_GROUPS * CMP_HIDDEN
    for comp in range(2):
        acc = jnp.zeros((nblk, 2 * nh), F32)
        for pp in range(CMP_STRIDE // 2):
            x = jnp.concatenate([get_rows(comp, 2 * pp), get_rows(comp, 2 * pp + 1)], axis=1).astype(BF16)
            acc = acc + _dot(x, wcat_ref[comp, pp])
        bias = _dot(pe_ref[comp], w1f_ref[comp])[0:1] + b1_ref[comp]
        bias = jnp.concatenate([bias] * NSA_KV_GROUPS, axis=1)
        hid = _silu(acc[:, :nh] + pltpu.roll(acc[:, nh:], nblk - 1, 0) + bias).astype(BF16)
        outs.append(_dot(hid, w2_ref[comp]) + b2_ref[comp])
    return outs


def _nsa_compress_body(kc_ref, vc_ref, wcat_ref, w1f_ref, pe_ref, b1_ref, w2_ref, b2_ref, k_ref, v_ref, *, nblk):
    def get_rows(comp, p):
        return (kc_ref, vc_ref)[comp][0, pl.ds(p, nblk, stride=CMP_STRIDE), :]
    k, v = _compress(get_rows, nblk, wcat_ref, w1f_ref, pe_ref, b1_ref, w2_ref, b2_ref)
    k_ref[0] = k
    v_ref[0] = v


def _nsa_compress(kv, cw):
    b, t, f = kv.shape
    nblk = t // CMP_STRIDE
    full = lambda a: pl.BlockSpec(a.shape, lambda b_: (0,) * a.ndim)
    return pl.pallas_call(
        functools.partial(_nsa_compress_body, nblk=nblk),
        out_shape=[jax.ShapeDtypeStruct((b, nblk, LANES), F32)] * 2,
        grid=(b,),
        in_specs=[pl.BlockSpec((1, t, LANES), lambda b_: (b_, 0, 0)),
                  pl.BlockSpec((1, t, LANES), lambda b_: (b_, 0, 1))] + [full(a) for a in cw],
        out_specs=[pl.BlockSpec((1, nblk, LANES), lambda b_: (b_, 0, 0))] * 2,
        compiler_params=_cparams(("arbitrary",)),
    )(kv, kv, *cw)


def _place_in_half(x, src_half, dst_half):
    return jnp.where(dst_half == src_half, x, pltpu.roll(x, LANES // 2, 1))


def _pair_queries(q, dst_half):
    q = q.astype(F32)
    lane = lax.broadcasted_iota(jnp.int32, q.shape, 1)
    keep = (lane // (LANES // 2)) == dst_half
    qa = jnp.where(keep, _place_in_half(q, 0, dst_half), 0.0)
    qb = jnp.where(keep, _place_in_half(q, 1, dst_half), 0.0)
    return jnp.concatenate([qa, qb], axis=0).astype(BF16)


def _pair_outputs(o2, src_half):
    tq = o2.shape[0] // 2
    lane = lax.broadcasted_iota(jnp.int32, (tq, LANES), 1)
    return jnp.where(lane < LANES // 2, _place_in_half(o2[:tq], src_half, 0), _place_in_half(o2[tq:], src_half, 1))


def _select_blocks(imp, cur, n_sel, k_eff):
    sp, n = imp.shape
    blk = lax.broadcasted_iota(jnp.int32, (sp, n), 0)
    blkf = blk.astype(F32)
    forced = (blk == 0) | (blk == cur) | (blk == cur - 1)
    allowed = (blk <= cur) & (blk < n_sel)
    score = jnp.where(blk > cur, NEG_INF, jnp.where(forced, FORCE_SCORE, imp))
    lowest = -3.0e38
    score = jnp.where(blk < n_sel, score, lowest)
    sel = jnp.zeros((sp, n), F32)
    for _ in range(k_eff):
        mx = jnp.max(score, axis=0, keepdims=True)
        first = jnp.min(jnp.where(score == mx, blkf, float(sp)), axis=0, keepdims=True)
        hit = blkf == first
        sel = jnp.where(hit, 1.0, sel)
        score = jnp.where(hit, lowest, score)
    return jnp.where(allowed, sel, 0.0)


def _overlap_t(n_sel_pad, n_cmp_pad, n_sel, n_cmp):
    ci = np.arange(n_cmp_pad)[None, :] * CMP_STRIDE
    sj = np.arange(n_sel_pad)[:, None] * SEL_BLOCK
    ov = (ci < sj + SEL_BLOCK) & (ci + CMP_BLOCK > sj)
    ov &= (np.arange(n_cmp_pad)[None, :] < n_cmp) & (np.arange(n_sel_pad)[:, None] < n_sel)
    return jnp.asarray(ov, BF16)


def _cmp_softmax(s, mask):
    s = jnp.where(mask, s, NEG_INF)
    p = jnp.where(mask, jnp.exp(s - jnp.max(s, axis=-1, keepdims=True)), 0.0)
    l = jnp.sum(p, axis=-1, keepdims=True)
    return p / jnp.where(l > 0.0, l, 1.0)


def _nsa_cmpsel_body(q_ref, k_ref, v_ref, ov_ref, o_ref, sel_ref, *, tq, n_sel, k_eff):
    g = pl.program_id(1)
    i = pl.program_id(2)
    nblk = k_ref.shape[1]
    nch = NSA_HPG // 2
    q2s = [_pair_queries(q_ref[0, :, ch * LANES:(ch + 1) * LANES], g) for ch in range(nch)]
    qall = jnp.concatenate(q2s, axis=0)
    s = _dot_nt(qall, k_ref[0].astype(BF16))
    m = qall.shape[0]
    qpos = i * tq + lax.broadcasted_iota(jnp.int32, (m, nblk), 0) % tq
    cmp_end = lax.broadcasted_iota(jnp.int32, (m, nblk), 1) * CMP_STRIDE + (CMP_BLOCK - 1)
    p = _cmp_softmax(s, cmp_end <= qpos)
    o = _dot(p.astype(BF16), v_ref[0].astype(BF16))
    for ch in range(nch):
        o_ref[0, :, ch * LANES:(ch + 1) * LANES] = _pair_outputs(o[2 * ch * tq:(2 * ch + 2) * tq], g)
    psum = p[:tq]
    for n in range(1, NSA_HPG):
        psum = psum + p[n * tq:(n + 1) * tq]
    hi = psum.astype(BF16)
    lo = (psum - hi.astype(F32)).astype(BF16)
    imp_t = _dot_nt(ov_ref[...], hi) + _dot_nt(ov_ref[...], lo)
    sp = imp_t.shape[0]
    cur = (i * tq + lax.broadcasted_iota(jnp.int32, (1, tq), 1)) // SEL_BLOCK
    sel_t = _select_blocks(imp_t, cur, n_sel, k_eff)
    if sp < LANES:
        sel_t = jnp.concatenate([sel_t, jnp.zeros((LANES - sp, tq), F32)], axis=0)
    sel_ref[0, 0] = sel_t.T


def _nsa_cmpsel(q, kcmp, vcmp):
    b, t, f = q.shape
    tq = min(TQ, t)
    nblk = kcmp.shape[1]
    n_sel = -(-t // SEL_BLOCK)
    assert n_sel <= LANES
    sp = -(-n_sel // 8) * 8
    ov = _overlap_t(sp, nblk, n_sel, nblk - 1)
    gw = f // NSA_KV_GROUPS
    return pl.pallas_call(
        functools.partial(_nsa_cmpsel_body, tq=tq, n_sel=n_sel, k_eff=min(N_SELECT, n_sel)),
        out_shape=[jax.ShapeDtypeStruct((b, t, f), F32), jax.ShapeDtypeStruct((b, NSA_KV_GROUPS, t, LANES), F32)],
        grid=(b, NSA_KV_GROUPS, t // tq),
        in_specs=[pl.BlockSpec((1, tq, gw), lambda b_, g, i: (b_, i, g)),
                  pl.BlockSpec((1, nblk, LANES), lambda b_, g, i: (b_, 0, 0)),
                  pl.BlockSpec((1, nblk, LANES), lambda b_, g, i: (b_, 0, 0)),
                  pl.BlockSpec(ov.shape, lambda b_, g, i: (0, 0))],
        out_specs=[pl.BlockSpec((1, tq, gw), lambda b_, g, i: (b_, i, g)),
                   pl.BlockSpec((1, 1, tq, LANES), lambda b_, g, i: (b_, g, i, 0))],
        compiler_params=_cparams(("arbitrary", "arbitrary", "arbitrary")),
    )(q, kcmp, vcmp, ov)


def _block_expand(base_pos, n_blocks_pad, tk):
    blk = lax.broadcasted_iota(jnp.int32, (n_blocks_pad, tk), 0)
    pos = base_pos + lax.broadcasted_iota(jnp.int32, (n_blocks_pad, tk), 1)
    return (blk == pos // SEL_BLOCK).astype(BF16)


def _nsa_prompt_body(*refs, tq, mode):
    if mode == "sel":
        q_ref, k_ref, v_ref, sel_ref, o_ref, m_ref, l_ref, acc_ref = refs
    else:
        q_ref, k_ref, v_ref, o_ref, m_ref, l_ref, acc_ref = refs
    c = pl.program_id(1)
    i = pl.program_id(2)
    g = c // (NSA_HPG // 2)
    q2 = _pair_queries(q_ref[0], g)
    _softmax_init(m_ref, l_ref, acc_ref)
    qpos = i * tq + lax.broadcasted_iota(jnp.int32, (tq, tq), 0)
    koff = lax.broadcasted_iota(jnp.int32, (tq, tq), 1)
    if mode == "sel":
        selb = sel_ref[0, 0].astype(BF16)

    def step(j, carry):
        start = pl.multiple_of(j * tq, tq)
        k = k_ref[0, pl.ds(start, tq), :].astype(BF16)
        v = v_ref[0, pl.ds(start, tq), :].astype(BF16)
        kpos = start + koff
        if mode == "sel":
            mask = (kpos <= qpos) & (_dot(selb, _block_expand(start, LANES, tq)) > 0.5)
        else:
            mask = (kpos <= qpos) & (qpos - kpos < WINDOW)
        _softmax_tile(_dot_nt(q2, k), jnp.concatenate([mask, mask], axis=0), v, m_ref, l_ref, acc_ref)
        return carry

    j_lo = 0 if mode == "sel" else jnp.maximum(i - (WINDOW + tq - 1) // tq, 0)
    lax.fori_loop(j_lo, i + 1, step, 0)
    o_ref[0] = _pair_outputs(acc_ref[...] / l_ref[...], g)


def _nsa_prompt_attn(q, kv, k_chunk, v_chunk, sel=None):
    b, t, f = q.shape
    tq = min(TQ, t)
    cpg = NSA_HPG // 2
    in_specs = [pl.BlockSpec((1, tq, LANES), lambda b_, c, i: (b_, i, c)),
                pl.BlockSpec((1, t, LANES), lambda b_, c, i: (b_, 0, k_chunk)),
                pl.BlockSpec((1, t, LANES), lambda b_, c, i: (b_, 0, v_chunk))]
    args = [q, kv, kv]
    if sel is not None:
        in_specs.append(pl.BlockSpec((1, 1, tq, LANES), lambda b_, c, i: (b_, c // cpg, i, 0)))
        args.append(sel)
    return pl.pallas_call(
        functools.partial(_nsa_prompt_body, tq=tq, mode="sel" if sel is not None else "win"),
        out_shape=jax.ShapeDtypeStruct((b, t, f), F32),
        grid=(b, f // LANES, t // tq),
        in_specs=in_specs,
        out_specs=pl.BlockSpec((1, tq, LANES), lambda b_, c, i: (b_, i, c)),
        scratch_shapes=_softmax_scratch(2 * tq, LANES),
        compiler_params=_cparams(("arbitrary", "arbitrary", "arbitrary")),
    )(*args)


def _out_nsa_body(x_ref, g_ref, oc_ref, os_ref, ow_ref, gl_ref, z_ref, e_ref, w_ref, lng_ref, lnb_ref, y_ref):
    sig = _sigmoid(gl_ref[0])
    o = (_split_dot(sig, e_ref[0]) * oc_ref[0] + _split_dot(sig, e_ref[1]) * os_ref[0]
         + _split_dot(sig, e_ref[2]) * ow_ref[0])
    _finish(o, z_ref, x_ref, g_ref, w_ref, lng_ref, lnb_ref, y_ref)


def _nsa_out_proj(x, gate, o_c, o_s, o_w, gl, z, w_out, ln_g, ln_b):
    d = x.shape[-1]
    e = np.zeros((3, GL_PAD, NSA_HEADS * NSA_HEAD_DIM), np.float32)
    for br in range(3):
        for h in range(NSA_HEADS):
            e[br, 3 * h + br, h * NSA_HEAD_DIM:(h + 1) * NSA_HEAD_DIM] = 1.0
    (y,) = _row_call(_out_nsa_body, x, [gate], [o_c, o_s, o_w, gl, z],
                     [jnp.asarray(e, BF16), w_out.astype(BF16), ln_g.reshape(1, d), ln_b.reshape(1, d)],
                     [(d, F32)], ROW_TILE)
    return y


def _row_token(m, n, rows_per_token, dt):
    return (lax.broadcasted_iota(jnp.int32, (m, n), 0) // rows_per_token) % dt


def _nsa_sample_cmp_body(pt_ref, q_ref, new_ref, ov_ref, wcat_ref, w1f_ref, pe_ref, b1_ref, w2_ref, b2_ref, *rest,
                         n_dense, pps, nsteps, dt, past, nblk):
    pages, (oc_ref, imp_ref, xk_ref, xv_ref) = rest[:pps], rest[pps:]
    c = pl.program_id(1)
    page = pages[0].shape[3]

    @pl.when(c == 0)
    def _():
        tail = xk_ref.shape[0] - past
        nw = new_ref[0]
        for x_ref, lo in ((xk_ref, 0), (xv_ref, LANES)):
            x_ref[pl.ds(past, tail), :] = jnp.zeros((tail, LANES), F32)
            x_ref[pl.ds(past, nw.shape[0]), :] = nw[:, lo:lo + LANES]

    for s in range(pps):
        pg = pages[s][0, 0]
        base = pl.multiple_of((c * pps + s) * page, page)
        xk_ref[pl.ds(base, page), :] = pg[:LANES].T
        xv_ref[pl.ds(base, page), :] = pg[LANES:].T

    @pl.when(c == nsteps - 1)
    def _():
        def get_rows(comp, p):
            return (xk_ref, xv_ref)[comp][pl.ds(p, nblk, stride=CMP_STRIDE), :]
        k, v = _compress(get_rows, nblk, wcat_ref, w1f_ref, pe_ref, b1_ref, w2_ref, b2_ref)
        q = q_ref[0]
        m = q.shape[0]
        s = _dot_nt(q, k.astype(BF16))
        qpos = past + _row_token(m, nblk, NSA_HPG, dt)
        cmp_end = lax.broadcasted_iota(jnp.int32, (m, nblk), 1) * CMP_STRIDE + (CMP_BLOCK - 1)
        p = _cmp_softmax(s, cmp_end <= qpos)
        oc_ref[0] = _dot(p.astype(BF16), v.astype(BF16))
        psum = jnp.sum(p.reshape(m // NSA_HPG, NSA_HPG, nblk), axis=1)
        imp_ref[0] = _split_dot(psum, ov_ref[...])


def _select_body(imp_ref, cur_ref, sel_ref, *, n_sel, k_eff):
    sel_ref[...] = _select_blocks(imp_ref[...], cur_ref[...], n_sel, k_eff)


def _nsa_sample_select(imp, past, dt, n_sel):
    b, r, spl = imp.shape
    sp = -(-n_sel // 8) * 8
    n = b * r
    npad = -(-n // LANES) * LANES
    imp_t = jnp.pad(imp.reshape(n, spl)[:, :sp].T, ((0, 0), (0, npad - n)))
    cur = jnp.tile((past + jnp.arange(dt, dtype=jnp.int32)) // SEL_BLOCK, n // dt)
    cur = jnp.pad(cur, (0, npad - n)).reshape(1, npad)
    sel_t = pl.pallas_call(
        functools.partial(_select_body, n_sel=n_sel, k_eff=min(N_SELECT, n_sel)),
        out_shape=jax.ShapeDtypeStruct((sp, npad), F32),
        compiler_params=pltpu.CompilerParams(vmem_limit_bytes=VMEM_LIMIT),
    )(imp_t, cur)
    return jnp.pad(sel_t[:, :n].T, ((0, 0), (0, spl - sp))).reshape(b, r, spl)


def _nsa_sample_sel_body(pt_ref, q_ref, sel_ref, new_ref, st_ref, neww_ref, *rest, n_dense, pps, nsteps, dt, past):
    pages, (os_ref, ow_ref, m_ref, l_ref, acc_ref, m2_ref, l2_ref, acc2_ref) = rest[:pps], rest[pps:]
    c = pl.program_id(1)
    q = q_ref[0]
    selb = sel_ref[0]
    m, spl = selb.shape
    page = pages[0].shape[3]

    def sel_tile(kv_t, base_pos, extra):
        tk = kv_t.shape[1]
        mask = _dot(selb, _block_expand(base_pos, spl, tk)) > 0.5
        if extra is not None:
            mask = mask & extra
        _softmax_tile(_dot(q, kv_t[:LANES].astype(BF16)), mask, kv_t[LANES:].astype(BF16), m_ref, l_ref, acc_ref, True)

    @pl.when(c == 0)
    def _():
        _softmax_init(m_ref, l_ref, acc_ref)
        _softmax_init(m2_ref, l2_ref, acc2_ref)
        tk = new_ref.shape[2]
        new_mask = _new_tile_mask(m, tk, NSA_HPG, dt, True)
        sel_tile(new_ref[0], past, new_mask)
        st = st_ref[0]
        wlen = st.shape[1]
        diff = (past + _row_token(m, wlen, NSA_HPG, dt)) - (past - wlen + lax.broadcasted_iota(jnp.int32, (m, wlen), 1))
        kpos_ok = (past - wlen + lax.broadcasted_iota(jnp.int32, (m, wlen), 1)) >= 0
        _softmax_tile(_dot(q, st[:LANES].astype(BF16)), (diff >= 0) & (diff < WINDOW) & kpos_ok,
                      st[LANES:].astype(BF16), m2_ref, l2_ref, acc2_ref, True)
        nw = neww_ref[0]
        diff_n = _row_token(m, tk, NSA_HPG, dt) - lax.broadcasted_iota(jnp.int32, (m, tk), 1)
        _softmax_tile(_dot(q, nw[:LANES].astype(BF16)), new_mask & (diff_n < WINDOW),
                      nw[LANES:].astype(BF16), m2_ref, l2_ref, acc2_ref, True)

    for s in range(pps):
        sel_tile(pages[s][0, 0], (c * pps + s) * page, None)

    @pl.when(c == nsteps - 1)
    def _():
        os_ref[0] = acc_ref[...] / l_ref[...]
        ow_ref[0] = acc2_ref[...] / l2_ref[...]


def _nsa_sample_attn(q, new_kv, new_w, cache, win_state, layer, page_table, cw):
    b, dt, _ = q.shape
    g = NSA_KV_GROUPS
    npg, page = page_table.shape[1], cache.shape[3]
    past = npg * page
    tk_all = past + dt
    n_chunks = -(-tk_all // CMP_STRIDE)
    nblk = -(-n_chunks // LANES) * LANES
    n_sel = -(-tk_all // SEL_BLOCK)
    spl = -(-n_sel // LANES) * LANES
    qg = _group_block_diag(q, g)
    m = qg.shape[1]
    ov = _overlap_t(spl, nblk, n_sel, n_chunks - 1).T
    new_cmp = jnp.pad(new_kv[:, :, :2 * LANES], ((0, 0), (0, CMP_STRIDE - dt), (0, 0)))
    o_c, imp = _paged_call(
        functools.partial(_nsa_sample_cmp_body, dt=dt, past=past, nblk=nblk), page_table, cache, layer, (2 * LANES, 0),
        [qg, new_cmp], [(m, LANES, F32), (m // NSA_HPG, spl, F32)],
        [pltpu.VMEM((nblk * CMP_STRIDE, LANES), F32)] * 2, shared=[ov] + list(cw))
    sel = _nsa_sample_select(imp, past, dt, n_sel)
    sel_rows = jnp.repeat(sel, NSA_HPG, axis=1).astype(BF16)
    o_s, o_w = _paged_call(
        functools.partial(_nsa_sample_sel_body, dt=dt, past=past), page_table, cache, layer, (2 * LANES, 1),
        [qg, sel_rows, _as_page(new_kv[:, :, 2 * LANES:], page), win_state, _as_page(new_w, page)],
        [(m, LANES, F32), (m, LANES, F32)], _softmax_scratch(m, LANES) + _softmax_scratch(m, LANES))
    return tuple(_group_diag_extract(o, g, dt) for o in (o_c, o_s, o_w))


def _nsa_layer(xp, xs, mod_p, mod_s, pos_p, pos_s, w_in, pe, w1, b1, w2, b2, w_out, ln_g, ln_b,
               cache, win_state, layer, page_table):
    db, dt, d = xs.shape
    g, hd = NSA_KV_GROUPS, NSA_HEAD_DIM
    sh_p, sc_p, gt_p = _split_mod(mod_p)
    sh_s, sc_s, gt_s = _split_mod(mod_s, dt)
    xs_flat = xs.reshape(1, db * dt, d)
    tab_p = _rope_tables(pos_p, hd // 4, ROPE_THETA, hd)
    tab_s = _rope_tables(jnp.tile(pos_s, db), hd // 4, ROPE_THETA, hd)
    cw = _cmp_weights(pe, w1, b1, w2, b2)
    q_p, kv_p, w_p, gl_p, z_p = _nsa_proj(xp, sh_p, sc_p, tab_p, w_in)
    q_s, kv_s, w_s, gl_s, z_s = _nsa_proj(xs_flat, sh_s, sc_s, tab_s, w_in)
    kcmp, vcmp = _nsa_compress(kv_p, cw)
    oc_p, sel = _nsa_cmpsel(q_p, kcmp, vcmp)
    os_p = _nsa_prompt_attn(q_p, kv_p, 2, 3, sel)
    ow_p = _nsa_prompt_attn(q_p, w_p, 0, 1)
    yp = _nsa_out_proj(xp, gt_p, oc_p, os_p, ow_p, gl_p, z_p, w_out, ln_g, ln_b)
    kv_s, w_s = kv_s.reshape(db, dt, -1), w_s.reshape(db, dt, -1)
    state = _page_view(win_state)[layer]
    oc_s, os_s, ow_s = _nsa_sample_attn(q_s.reshape(db, dt, -1), kv_s, w_s, cache, state, layer, page_table, cw)
    flat = lambda a: a.reshape(1, db * dt, -1)
    ys = _nsa_out_proj(xs_flat, gt_s, flat(oc_s), flat(os_s), flat(ow_s), gl_s, z_s, w_out, ln_g, ln_b)
    t = xp.shape[1]
    rows_p = kv_p.reshape(kv_p.shape[:2] + (4, g, hd))
    rows_s = kv_s.reshape(db, dt, 4, g, hd)
    win_p = w_p[:, t - min(WINDOW, t):].reshape(xp.shape[0], min(WINDOW, t), 2, g, hd)
    wrows = jnp.concatenate([win_state[layer], w_s.reshape(db, dt, 2, g, hd)], axis=1)
    win_s = wrows[:, wrows.shape[1] - min(WINDOW, wrows.shape[1]):]
    return yp, ys.reshape(db, dt, d), rows_p, rows_s, win_p, win_s


def kernel(x_prompt, x_sample, cache_sb_kv, cache_mla_latent, cache_diff_kv, cache_nsa_kv, state_nsa_window,
           page_table, c_prompt, c_sample, ada_w, ada_b, ln_g, ln_b, sb_w_in, sb_w_out,
           mla_w_in, mla_q_norm, mla_w_q_up, mla_kv_norm, mla_w_uk, mla_w_uv, mla_w_out,
           diff_w_in, diff_lam, diff_subln, diff_w_out,
           nsa_w_in, nsa_cmp_pe, nsa_cmp_w1, nsa_cmp_b1, nsa_cmp_w2, nsa_cmp_b2, nsa_w_out):
    nb = x_prompt.shape[0]
    mods = _adaln(jnp.concatenate([c_prompt, c_sample], axis=0), ada_w, ada_b)
    past = page_table.shape[1] * cache_sb_kv.shape[2]
    pos_p = jnp.arange(x_prompt.shape[1], dtype=jnp.int32)
    pos_s = past + jnp.arange(x_sample.shape[1], dtype=jnp.int32)
    xp, xs, sb_p, sb_s = _sb_layer(x_prompt, x_sample, mods[0, :nb], mods[0, nb:], sb_w_in[0], sb_w_out[0],
                                   ln_g[0], ln_b[0], _page_view(cache_sb_kv), 0, page_table)
    xp, xs, mla_p, mla_s = _mla_layer(xp, xs, mods[1, :nb], mods[1, nb:], pos_p, pos_s, mla_w_in[0], mla_q_norm[0],
                                      mla_w_q_up[0], mla_kv_norm[0], mla_w_uk[0], mla_w_uv[0], mla_w_out[0],
                                      ln_g[1], ln_b[1], _page_view(cache_mla_latent), 0, page_table)
    xp, xs, diff_p, diff_s = _diff_layer(xp, xs, mods[2, :nb], mods[2, nb:], pos_p, pos_s, 2, diff_w_in[0],
                                         diff_lam[0], diff_subln[0], diff_w_out[0], ln_g[2], ln_b[2],
                                         cache_diff_kv, 0, page_table)
    xp, xs, nsa_p, nsa_s, win_p, win_s = _nsa_layer(
        xp, xs, mods[3, :nb], mods[3, nb:], pos_p, pos_s, nsa_w_in[0], nsa_cmp_pe[0], nsa_cmp_w1[0], nsa_cmp_b1[0],
        nsa_cmp_w2[0], nsa_cmp_b2[0], nsa_w_out[0], ln_g[3], ln_b[3], _page_view(cache_nsa_kv), state_nsa_window,
        0, page_table)
    return (xp, xs, sb_p[None], sb_s[None], mla_p[None], mla_s[None], diff_p[None], diff_s[None],
            nsa_p[None], nsa_s[None], win_p[None], win_s[None])
```

```python
import functools
import math

import numpy as np
import jax
import jax.numpy as jnp
from jax import lax
from jax.experimental import pallas as pl
from jax.experimental.pallas import tpu as pltpu

F32 = jnp.float32
BF16 = jnp.bfloat16

DEPTH = 4
ALPHA = (2 * DEPTH) ** 0.25
NORM_EPS = 1e-5
NEG_INF = -1e30
FORCE_SCORE = 1e9
ROPE_THETA = 500000.0

SB_HEADS, SB_KV_HEADS, SB_HEAD_DIM = 16, 4, 64
MLA_HEADS, MLA_Q_RANK, MLA_KV_RANK = 16, 256, 128
MLA_NOPE_DIM, MLA_ROPE_DIM, MLA_V_DIM = 64, 32, 64
MLA_ROPE_THETA = 10000.0
DIFF_HEADS, DIFF_KV_HEADS, DIFF_HEAD_DIM = 8, 2, 64
NSA_HEADS, NSA_KV_GROUPS, NSA_HEAD_DIM = 16, 2, 64
CMP_BLOCK, CMP_STRIDE, CMP_HIDDEN = 32, 16, 128
SEL_BLOCK, N_SELECT, WINDOW = 64, 16, 512

LANES = 128
VMEM_LIMIT = 48 * 1024 * 1024
PAGES_PER_STEP = 8
TQ = 256
ROW_TILE = 256


def _cparams(sem):
    return pltpu.CompilerParams(dimension_semantics=sem, vmem_limit_bytes=VMEM_LIMIT)


def _dot(a, b):
    return jnp.dot(a, b, preferred_element_type=F32)


def _dot_nt(a, b):
    return lax.dot_general(a, b, (((1,), (1,)), ((), ())), preferred_element_type=F32)


def _silu(x):
    return x / (1.0 + jnp.exp(-x))


def _sigmoid(x):
    return 1.0 / (1.0 + jnp.exp(-x))


def _split_dot(x, w_bf16):
    hi = x.astype(BF16)
    lo = (x - hi.astype(F32)).astype(BF16)
    return _dot(hi, w_bf16) + _dot(lo, w_bf16)


def _rope128(x, c, s1, s2, half):
    return x * c + pltpu.roll(x, LANES - half, 1) * s1 + pltpu.roll(x, half, 1) * s2


def _rope_tables(pos, rot_dim, theta, period):
    half = rot_dim // 2
    inv = theta ** (-jnp.arange(half, dtype=F32) / half)
    ang = pos.astype(F32)[:, None] * inv
    cos, sin = jnp.cos(ang), jnp.sin(ang)
    n = pos.shape[0]
    ones = jnp.ones((n, period - rot_dim), F32)
    zeros_h = jnp.zeros((n, half), F32)
    zeros_r = jnp.zeros((n, period - rot_dim), F32)
    c = jnp.concatenate([cos, cos, ones], axis=1)
    s1 = jnp.concatenate([-sin, zeros_h, zeros_r], axis=1)
    s2 = jnp.concatenate([zeros_h, sin, zeros_r], axis=1)
    rep = LANES // period
    return tuple(jnp.tile(t, (1, rep))[None] for t in (c, s1, s2))


def _adaln_body(c_ref, w_ref, b_ref, o_ref):
    s = _silu(c_ref[...]).astype(BF16)
    o_ref[0] = _dot(s, w_ref[0].astype(BF16)) + b_ref[0]


def _adaln(c_all, ada_w, ada_b):
    depth, d, d3 = ada_w.shape
    n = c_all.shape[0]
    nt = d3 // d
    return pl.pallas_call(
        _adaln_body,
        out_shape=jax.ShapeDtypeStruct((depth, n, d3), F32),
        grid=(depth, nt),
        in_specs=[pl.BlockSpec((n, d), lambda l, j: (0, 0)),
                  pl.BlockSpec((1, d, d), lambda l, j: (l, 0, j)),
                  pl.BlockSpec((1, 1, d), lambda l, j: (l, 0, j))],
        out_specs=pl.BlockSpec((1, n, d), lambda l, j: (l, 0, j)),
        compiler_params=_cparams(("arbitrary", "arbitrary")),
    )(c_all, ada_w, ada_b.reshape(depth, 1, d3))


def _row_call(body, x, mods, row_inputs, full_inputs, out_defs, tm):
    bt, r, _ = x.shape
    tm = min(tm, r)
    grid = (bt, r // tm)

    def tiled(a):
        lead = a.shape[0]
        if a.shape[1] == 1:
            return pl.BlockSpec((1, 1, a.shape[2]), lambda b, i: (b, 0, 0))
        if lead == 1 and bt > 1:
            return pl.BlockSpec((1, tm, a.shape[2]), lambda b, i: (0, i, 0))
        return pl.BlockSpec((1, tm, a.shape[2]), lambda b, i: (b, i, 0))

    def full(a):
        nd = a.ndim
        return pl.BlockSpec(a.shape, lambda b, i: (0,) * nd)

    in_specs = [tiled(x)] + [tiled(m) for m in mods] + [tiled(t) for t in row_inputs] + [full(w) for w in full_inputs]
    out_shape = [jax.ShapeDtypeStruct((bt, r, w), dt) for w, dt in out_defs]
    out_specs = [pl.BlockSpec((1, tm, w), lambda b, i: (b, i, 0)) for w, _ in out_defs]
    return pl.pallas_call(
        body, out_shape=out_shape, grid=grid, in_specs=in_specs, out_specs=out_specs,
        compiler_params=_cparams(("arbitrary", "arbitrary")),
    )(x, *mods, *row_inputs, *full_inputs)


def _modulated(x_ref, sh_ref, sc_ref):
    return (x_ref[0] * (1.0 + sc_ref[0]) + sh_ref[0]).astype(BF16)


def _finish(o, z_ref, x_ref, g_ref, w_ref, lng_ref, lnb_ref, y_ref):
    gated = (o * _silu(z_ref[0])).astype(BF16)
    y = _dot(gated, w_ref[...])
    r = ALPHA * x_ref[0] + (1.0 + g_ref[0]) * y
    mu = jnp.mean(r, axis=-1, keepdims=True)
    d = r - mu
    var = jnp.mean(d * d, axis=-1, keepdims=True)
    y_ref[0] = d * lax.rsqrt(var + NORM_EPS) * lng_ref[...] + lnb_ref[...]


def _out_plain_body(x_ref, g_ref, o_ref, z_ref, w_ref, lng_ref, lnb_ref, y_ref):
    _finish(o_ref[0].astype(F32), z_ref, x_ref, g_ref, w_ref, lng_ref, lnb_ref, y_ref)


def _out_proj(x, gate, o, z, w_out, ln_g, ln_b):
    d = x.shape[-1]
    (y,) = _row_call(_out_plain_body, x, [gate], [o, z], [w_out.astype(BF16), ln_g.reshape(1, d), ln_b.reshape(1, d)],
                     [(d, F32)], ROW_TILE)
    return y


def _sb_proj_body(x_ref, sh_ref, sc_ref, wq_ref, wkv_ref, wz_ref, q_ref, kv_ref, z_ref):
    h = _modulated(x_ref, sh_ref, sc_ref)
    q_ref[0] = (_dot(h, wq_ref[...]) * (SB_HEAD_DIM ** -0.5)).astype(BF16)
    kv_ref[0] = _dot(h, wkv_ref[...])
    z_ref[0] = _dot(h, wz_ref[...])


def _sb_proj(x, sh, sc, w_in):
    nq = SB_HEADS * SB_HEAD_DIM
    nkv = 2 * SB_KV_HEADS * SB_HEAD_DIM
    w = w_in.astype(BF16)
    return _row_call(_sb_proj_body, x, [sh, sc], [], [w[:, :nq], w[:, nq:nq + nkv], w[:, nq + nkv:]],
                     [(nq, BF16), (nkv, F32), (nq, F32)], ROW_TILE)


def _softplus(z):
    return jnp.maximum(z, 0.0) + jnp.log(1.0 + jnp.exp(-jnp.abs(z)))


def _sb_tile(q2, k, v, mask, tri, r_prev, keys_on_lanes=False):
    z = _dot(q2, k) if keys_on_lanes else _dot_nt(q2, k)
    sp = _softplus(z)
    u = sp if mask is None else jnp.where(mask, sp, 0.0)
    log_rem = _split_dot(u, tri) + r_prev
    a = jnp.exp(z - sp - log_rem)
    if mask is not None:
        a = jnp.where(mask, a, 0.0)
    a = a.astype(BF16)
    pv = _dot_nt(a, v) if keys_on_lanes else _dot(a, v)
    return pv, r_prev + jnp.sum(u, axis=-1, keepdims=True)


def _tri(tk):
    return (lax.broadcasted_iota(jnp.int32, (tk, tk), 0) > lax.broadcasted_iota(jnp.int32, (tk, tk), 1)).astype(BF16)


def _sb_prompt_body(q_ref, k_ref, v_ref, o_ref, acc_ref, *, tq):
    c = pl.program_id(1)
    i = pl.program_id(2)
    half = (c // 2) % 2
    lane = lax.broadcasted_iota(jnp.int32, (tq, LANES), 1)
    in_half = (lane // SB_HEAD_DIM) == half
    q = q_ref[0].astype(F32)
    qr = pltpu.roll(q, SB_HEAD_DIM, 1)
    first = half == 0
    qa = jnp.where(in_half, jnp.where(first, q, qr), 0.0)
    qb = jnp.where(in_half, jnp.where(first, qr, q), 0.0)
    q2 = jnp.concatenate([qa, qb], axis=0).astype(BF16)
    tri = _tri(tq)
    row = lax.broadcasted_iota(jnp.int32, (tq, tq), 0)
    col = lax.broadcasted_iota(jnp.int32, (tq, tq), 1)
    diag_mask = jnp.concatenate([col < row] * 2, axis=0)
    acc_ref[...] = jnp.zeros_like(acc_ref)

    def load(ref, j):
        return ref[0, pl.ds(pl.multiple_of(j * tq, tq), tq), :].astype(BF16)

    def weights(z, mask, r_prev):
        sp = _softplus(z)
        u = sp if mask is None else jnp.where(mask, sp, 0.0)
        a = jnp.exp(z - sp - (_split_dot(u, tri) + r_prev))
        if mask is not None:
            a = jnp.where(mask, a, 0.0)
        return a.astype(BF16), r_prev + jnp.sum(u, axis=-1, keepdims=True)

    a_prev, r = weights(_dot_nt(q2, load(k_ref, i)), diag_mask, jnp.zeros((2 * tq, 1), F32))
    z = _dot_nt(q2, load(k_ref, jnp.maximum(i - 1, 0)))

    def step(jj, carry):
        z, a_prev, r = carry
        j = i - jj
        acc_ref[...] += _dot(a_prev, load(v_ref, j + 1))
        z_next = _dot_nt(q2, load(k_ref, jnp.maximum(j - 1, 0)))
        a, r = weights(z, None, r)
        return z_next, a, r

    _, a_prev, _ = lax.fori_loop(1, i + 1, step, (z, a_prev, r))
    acc = acc_ref[...] + _dot(a_prev, load(v_ref, 0))
    oa, ob = acc[:tq], acc[tq:]
    oar, obr = pltpu.roll(oa, SB_HEAD_DIM, 1), pltpu.roll(ob, SB_HEAD_DIM, 1)
    low = lane < SB_HEAD_DIM
    o_ref[0] = jnp.where(low, jnp.where(first, oa, oar), jnp.where(first, obr, ob))


def _sb_prompt_attn(q, kv):
    b, t, _ = q.shape
    tq = min(TQ, t)
    nchunk = SB_HEADS * SB_HEAD_DIM // LANES
    kchunks = SB_KV_HEADS * SB_HEAD_DIM // LANES
    return pl.pallas_call(
        functools.partial(_sb_prompt_body, tq=tq),
        out_shape=jax.ShapeDtypeStruct((b, t, nchunk * LANES), F32),
        grid=(b, nchunk, t // tq),
        in_specs=[pl.BlockSpec((1, tq, LANES), lambda b_, c, i: (b_, i, c)),
                  pl.BlockSpec((1, t, LANES), lambda b_, c, i: (b_, 0, c // 4)),
                  pl.BlockSpec((1, t, LANES), lambda b_, c, i: (b_, 0, kchunks + c // 4))],
        out_specs=pl.BlockSpec((1, tq, LANES), lambda b_, c, i: (b_, i, c)),
        scratch_shapes=[pltpu.VMEM((2 * tq, LANES), F32)],
        compiler_params=_cparams(("arbitrary", "arbitrary", "arbitrary")),
    )(q, kv, kv)


def _paged_call(body, page_table, cache, layer, row_block, dense, out_defs, scratch, descending=False, shared=()):
    nb, npg = page_table.shape
    pps = min(PAGES_PER_STEP, npg)
    nsteps = npg // pps
    page = cache.shape[3]
    rb, ridx = row_block

    def page_spec(s):
        def imap(b, c, pt):
            p = c * pps + s
            if descending:
                p = npg - 1 - p
            return (layer, pt[b, p], ridx, 0)
        return pl.BlockSpec((1, 1, rb, page), imap)

    def shared_spec(a):
        nd = a.ndim
        return pl.BlockSpec(a.shape, lambda b, c, pt: (0,) * nd)

    in_specs = [pl.BlockSpec((1,) + a.shape[1:], lambda b, c, pt: (b, 0, 0)) for a in dense]
    in_specs += [shared_spec(a) for a in shared]
    in_specs += [page_spec(s) for s in range(pps)]
    out_shape = [jax.ShapeDtypeStruct((nb, r, w), dt) for r, w, dt in out_defs]
    out_specs = [pl.BlockSpec((1, r, w), lambda b, c, pt: (b, 0, 0)) for r, w, _ in out_defs]
    grid_spec = pltpu.PrefetchScalarGridSpec(num_scalar_prefetch=1, grid=(nb, nsteps), in_specs=in_specs,
                                             out_specs=out_specs, scratch_shapes=scratch)
    return pl.pallas_call(
        functools.partial(body, n_dense=len(dense) + len(shared), pps=pps, nsteps=nsteps),
        out_shape=out_shape, grid_spec=grid_spec,
        compiler_params=_cparams(("arbitrary", "arbitrary")),
    )(page_table, *dense, *shared, *([cache] * pps))


def _as_page(rows, page):
    return jnp.pad(rows.transpose(0, 2, 1), ((0, 0), (0, 0), (0, page - rows.shape[1])))


def _sb_sample_body(pt_ref, q_ref, new_ref, *rest, n_dense, pps, nsteps, dt):
    pages, (o_ref, acc_ref, r_ref) = rest[:pps], rest[pps:]
    c = pl.program_id(1)
    nk = SB_KV_HEADS * SB_HEAD_DIM
    q2 = q_ref[0]
    m, tk = q2.shape[0], new_ref.shape[2]
    tri = _tri(tk)

    @pl.when(c == 0)
    def _():
        row_t = (lax.broadcasted_iota(jnp.int32, (m, tk), 0) // (SB_HEADS // SB_KV_HEADS)) % dt
        key = lax.broadcasted_iota(jnp.int32, (m, tk), 1)
        mask = (key < row_t) & (key < dt)
        nw = new_ref[0]
        pv, r_new = _sb_tile(q2, nw[:nk].astype(BF16), nw[nk:].astype(BF16), mask, tri, jnp.zeros((m, 1), F32), True)
        acc_ref[...] = pv
        r_ref[...] = r_new

    pgs = [pages[s][0, 0] for s in range(pps)]
    zs = [_dot(q2, pg[:nk].astype(BF16)) for pg in pgs]
    sps = [_softplus(z) for z in zs]
    cums = _split_dot(jnp.concatenate(sps, axis=0), tri)
    r = r_ref[...]
    pv = jnp.zeros(acc_ref.shape, F32)
    for s in range(pps):
        a = jnp.exp(zs[s] - sps[s] - (cums[s * m:(s + 1) * m] + r))
        pv = pv + _dot_nt(a.astype(BF16), pgs[s][nk:].astype(BF16))
        r = r + jnp.sum(sps[s], axis=-1, keepdims=True)
    acc_ref[...] += pv
    r_ref[...] = r

    @pl.when(c == nsteps - 1)
    def _():
        o_ref[0] = acc_ref[...]


def _group_block_diag(q, groups):
    b, t, f = q.shape
    d = SB_HEAD_DIM
    n = f // (groups * d)
    q5 = q.reshape(b, t, groups, n, d)
    eye = jnp.eye(groups, dtype=q.dtype)
    out = q5.transpose(0, 2, 1, 3, 4)[:, :, :, :, None, :] * eye[None, :, None, None, :, None]
    return out.reshape(b, groups * t * n, groups * d)


def _group_diag_extract(acc, groups, t):
    b, r, f = acc.shape
    w = f // groups
    n = r // (groups * t)
    a5 = acc.reshape(b, groups, t * n, groups, w)
    picked = jnp.stack([a5[:, g, :, g, :] for g in range(groups)], axis=1)
    return picked.reshape(b, groups, t, n, w).transpose(0, 2, 1, 3, 4).reshape(b, t, groups * n * w)


def _sb_sample_attn(q, new_kv, cache, layer, page_table):
    b, dt, _ = q.shape
    page = cache.shape[3]
    qbd = _group_block_diag(q, SB_KV_HEADS)
    m = qbd.shape[1]
    nk = SB_KV_HEADS * SB_HEAD_DIM
    (acc,) = _paged_call(
        functools.partial(_sb_sample_body, dt=dt), page_table, cache, layer, (2 * nk, 0),
        [qbd, _as_page(new_kv, page)], [(m, nk, F32)],
        [pltpu.VMEM((m, nk), F32), pltpu.VMEM((m, 1), F32)], descending=True)
    return _group_diag_extract(acc, SB_KV_HEADS, dt)


def _page_view(cache):
    nl, pool, page = cache.shape[:3]
    feat = math.prod(cache.shape[3:])
    return jnp.moveaxis(cache.reshape(nl, pool, page, feat), 2, 3)


def _split_mod(mod, per_row_t=None):
    d = mod.shape[-1] // 3
    parts = [mod[:, k * d:(k + 1) * d] for k in range(3)]
    if per_row_t is None:
        return [p[:, None, :] for p in parts]
    return [jnp.repeat(p, per_row_t, axis=0)[None] for p in parts]


def _sb_layer(xp, xs, mod_p, mod_s, w_in, w_out, ln_g, ln_b, cache, layer, page_table):
    db, dt, d = xs.shape
    sh_p, sc_p, gt_p = _split_mod(mod_p)
    sh_s, sc_s, gt_s = _split_mod(mod_s, dt)
    xs_flat = xs.reshape(1, db * dt, d)
    qp, kvp, zp = _sb_proj(xp, sh_p, sc_p, w_in)
    qs, kvs, zs = _sb_proj(xs_flat, sh_s, sc_s, w_in)
    op = _sb_prompt_attn(qp, kvp)
    os_ = _sb_sample_attn(qs.reshape(db, dt, -1), kvs.reshape(db, dt, -1), cache, layer, page_table)
    yp = _out_proj(xp, gt_p, op, zp, w_out, ln_g, ln_b)
    ys = _out_proj(xs_flat, gt_s, os_.reshape(1, db * dt, -1), zs, w_out, ln_g, ln_b)
    rows_shape = (2, SB_KV_HEADS, SB_HEAD_DIM)
    return (yp, ys.reshape(db, dt, d), kvp.reshape(kvp.shape[:2] + rows_shape), kvs.reshape((db, dt) + rows_shape))


def _softmax_tile(s, mask, v, m_ref, l_ref, acc_ref, keys_on_lanes=False):
    if mask is not None:
        s = jnp.where(mask, s, NEG_INF)
    m_prev = m_ref[...]
    m_new = jnp.maximum(m_prev, jnp.max(s, axis=-1, keepdims=True))
    p = jnp.exp(s - m_new)
    if mask is not None:
        p = jnp.where(mask, p, 0.0)
    scale = jnp.exp(m_prev - m_new)
    pb = p.astype(BF16)
    pv = _dot_nt(pb, v) if keys_on_lanes else _dot(pb, v)
    l_ref[...] = scale * l_ref[...] + jnp.sum(p, axis=-1, keepdims=True)
    acc_ref[...] = scale * acc_ref[...] + pv
    m_ref[...] = m_new


def _softmax_init(m_ref, l_ref, acc_ref):
    m_ref[...] = jnp.full_like(m_ref, NEG_INF)
    l_ref[...] = jnp.zeros_like(l_ref)
    acc_ref[...] = jnp.zeros_like(acc_ref)


def _softmax_scratch(m, f):
    return [pltpu.VMEM((m, 1), F32), pltpu.VMEM((m, 1), F32), pltpu.VMEM((m, f), F32)]


def _softmax_pages(q, kts, vts, mask, m_ref, l_ref, acc_ref):
    s = jnp.concatenate([_dot(q, kt) for kt in kts], axis=1)
    p, scale = _softmax_stage(s, mask, m_ref, l_ref)
    tk = kts[0].shape[1]
    pv = _dot_nt(p[:, :tk], vts[0])
    for n in range(1, len(vts)):
        pv = pv + _dot_nt(p[:, n * tk:(n + 1) * tk], vts[n])
    acc_ref[...] = scale * acc_ref[...] + pv


def _rms(x, g):
    return x * lax.rsqrt(jnp.mean(x * x, axis=-1, keepdims=True) + NORM_EPS) * g


def _mla_proj_body(x_ref, sh_ref, sc_ref, c_ref, s1_ref, s2_ref, wcq_ref, wckv_ref, wpe_ref, wz_ref, g_ref,
                   cq_ref, rows_ref, kk_ref, z_ref):
    h = _modulated(x_ref, sh_ref, sc_ref)
    cq_ref[0] = _dot(h, wcq_ref[...])
    lat = _rms(_dot(h, wckv_ref[...]), g_ref[...])
    pe = _rope128(_dot(h, wpe_ref[...]), c_ref[0], s1_ref[0], s2_ref[0], MLA_ROPE_DIM // 2)
    rows_ref[0, :, :MLA_KV_RANK] = lat
    rows_ref[0, :, MLA_KV_RANK:] = pe[:, :MLA_ROPE_DIM]
    kk_ref[0] = jnp.concatenate([lat, pe], axis=1).astype(BF16)
    z_ref[0] = _dot(h, wz_ref[...])


def _mla_proj(x, sh, sc, tables, w_in, kv_norm):
    r, pe = MLA_KV_RANK, MLA_ROPE_DIM
    w = w_in.astype(BF16)
    n0, n1, n2 = MLA_Q_RANK, MLA_Q_RANK + r, MLA_Q_RANK + r + pe
    wpe = jnp.pad(w[:, n1:n2], ((0, 0), (0, LANES - pe)))
    nz = w.shape[1] - n2
    return _row_call(_mla_proj_body, x, [sh, sc], list(tables),
                     [w[:, :n0], w[:, n0:n1], wpe, w[:, n2:], kv_norm.reshape(1, r)],
                     [(n0, F32), (r + pe, F32), (2 * LANES, BF16), (nz, F32)], ROW_TILE)


def _mla_q_body(cq_ref, c_ref, s1_ref, s2_ref, g_ref, wn_ref, wuk_ref, wp_ref, q_ref):
    nb = _rms(cq_ref[0], g_ref[...]).astype(BF16)
    scale = (MLA_NOPE_DIM + MLA_ROPE_DIM) ** -0.5
    c, s1, s2 = c_ref[0], s1_ref[0], s2_ref[0]
    for h in range(MLA_HEADS):
        nope = _dot(nb, wn_ref[h]).astype(BF16)
        q_ref[0, :, 2 * h * LANES:(2 * h + 1) * LANES] = (_dot(nope, wuk_ref[h]) * scale).astype(BF16)
        pe = _rope128(_dot(nb, wp_ref[h]), c, s1, s2, MLA_ROPE_DIM // 2)
        q_ref[0, :, (2 * h + 1) * LANES:(2 * h + 2) * LANES] = (pe * scale).astype(BF16)


def _mla_q(cq, tables, q_norm, w_q_up, w_uk):
    hds, dn, dr = MLA_HEADS, MLA_NOPE_DIM, MLA_ROPE_DIM
    wq = w_q_up.astype(BF16).reshape(MLA_Q_RANK, hds, dn + dr)
    wn = wq[:, :, :dn].transpose(1, 0, 2)
    wp = jnp.pad(wq[:, :, dn:].transpose(1, 0, 2), ((0, 0), (0, 0), (0, LANES - dr)))
    wuk = w_uk.astype(BF16).transpose(1, 2, 0)
    (q,) = _row_call(_mla_q_body, cq, [], list(tables), [q_norm.reshape(1, MLA_Q_RANK), wn, wuk, wp],
                     [(hds * 2 * LANES, BF16)], ROW_TILE)
    return q


def _causal_mask(i, j, tq, rows):
    qpos = i * tq + lax.broadcasted_iota(jnp.int32, (tq, tq), 0)
    kpos = j * tq + lax.broadcasted_iota(jnp.int32, (tq, tq), 1)
    mask = kpos <= qpos
    return mask if rows == 1 else jnp.concatenate([mask] * rows, axis=0)


def _softmax_stage(s, mask, m_ref, l_ref):
    if mask is not None:
        s = jnp.where(mask, s, NEG_INF)
    m_prev = m_ref[...]
    m_new = jnp.maximum(m_prev, jnp.max(s, axis=-1, keepdims=True))
    p = jnp.exp(s - m_new)
    if mask is not None:
        p = jnp.where(mask, p, 0.0)
    scale = jnp.exp(m_prev - m_new)
    l_ref[...] = scale * l_ref[...] + jnp.sum(p, axis=-1, keepdims=True)
    m_ref[...] = m_new
    return p.astype(BF16), scale


def _flash_loop(q, load_k, load_v, mask_fn, j_lo, i, m_ref, l_ref, acc_ref):
    _softmax_init(m_ref, l_ref, acc_ref)
    s0 = _dot_nt(q, load_k(j_lo))
    m, tk = s0.shape

    def pv_update(p_prev, scale_prev, j_prev):
        acc_ref[...] = scale_prev * acc_ref[...] + _dot(p_prev, load_v(j_prev))

    def step(j, carry):
        s, p_prev, scale_prev = carry
        pv_update(p_prev, scale_prev, jnp.maximum(j - 1, j_lo))
        s_next = _dot_nt(q, load_k(j + 1))
        p, scale = _softmax_stage(s, mask_fn(j, False), m_ref, l_ref)
        return s_next, p, scale

    init = (s0, jnp.zeros((m, tk), BF16), jnp.ones((m, 1), F32))
    s, p_prev, scale_prev = lax.fori_loop(j_lo, i, step, init)
    pv_update(p_prev, scale_prev, jnp.maximum(i - 1, j_lo))
    p, scale = _softmax_stage(s, mask_fn(i, True), m_ref, l_ref)
    pv_update(p, scale, i)


def _mla_prompt_body(q_ref, kk_ref, o_ref, m_ref, l_ref, acc_ref, *, tq):
    i = pl.program_id(2)

    def load_k(j):
        return kk_ref[0, pl.ds(pl.multiple_of(j * tq, tq), tq), :]

    def load_v(j):
        return kk_ref[0, pl.ds(pl.multiple_of(j * tq, tq), tq), :MLA_KV_RANK]

    def mask_fn2(j, diag):
        return _causal_mask(i, j, tq, 2) if diag else None

    q = q_ref[0]
    q2 = jnp.concatenate([q[:, :2 * LANES], q[:, 2 * LANES:]], axis=0)
    _flash_loop(q2, load_k, load_v, mask_fn2, 0, i, m_ref, l_ref, acc_ref)
    o = (acc_ref[...] / l_ref[...]).astype(BF16)
    o_ref[0, :, :MLA_KV_RANK] = o[:tq]
    o_ref[0, :, MLA_KV_RANK:] = o[tq:]


def _mla_prompt_attn(q, kk):
    b, t, _ = kk.shape
    tq = min(TQ, t)
    return pl.pallas_call(
        functools.partial(_mla_prompt_body, tq=tq),
        out_shape=jax.ShapeDtypeStruct((b, t, MLA_HEADS * MLA_KV_RANK), BF16),
        grid=(b, MLA_HEADS // 2, t // tq),
        in_specs=[pl.BlockSpec((1, tq, 4 * LANES), lambda b_, h, i: (b_, i, h)),
                  pl.BlockSpec((1, t, 2 * LANES), lambda b_, h, i: (b_, 0, 0))],
        out_specs=pl.BlockSpec((1, tq, 2 * MLA_KV_RANK), lambda b_, h, i: (b_, i, h)),
        scratch_shapes=_softmax_scratch(2 * tq, MLA_KV_RANK),
        compiler_params=_cparams(("arbitrary", "arbitrary", "arbitrary")),
    )(q, kk)


def _new_tile_mask(m, tk, rows_per_token, dt, inclusive):
    row_t = (lax.broadcasted_iota(jnp.int32, (m, tk), 0) // rows_per_token) % dt
    key = lax.broadcasted_iota(jnp.int32, (m, tk), 1)
    return ((key <= row_t) if inclusive else (key < row_t)) & (key < dt)


def _mla_sample_body(pt_ref, q_ref, new_ref, *rest, n_dense, pps, nsteps, dt):
    pages, (o_ref, m_ref, l_ref, acc_ref) = rest[:pps], rest[pps:]
    c = pl.program_id(1)
    q = q_ref[0]
    m, tk = q.shape[0], new_ref.shape[2]

    @pl.when(c == 0)
    def _():
        _softmax_init(m_ref, l_ref, acc_ref)
        nw = new_ref[0].astype(BF16)
        _softmax_tile(_dot(q, nw), _new_tile_mask(m, tk, MLA_HEADS, dt, True), nw[:MLA_KV_RANK], m_ref, l_ref, acc_ref, True)

    pgs = [pages[s][0, 0].astype(BF16) for s in range(pps)]
    _softmax_pages(q, pgs, [pg[:MLA_KV_RANK] for pg in pgs], None, m_ref, l_ref, acc_ref)

    @pl.when(c == nsteps - 1)
    def _():
        o_ref[0] = (acc_ref[...] / l_ref[...]).astype(BF16)


def _mla_sample_attn(q, new_rows, cache, layer, page_table):
    b, dt, _ = q.shape
    feat = MLA_KV_RANK + MLA_ROPE_DIM
    q160 = q.reshape(b, dt * MLA_HEADS, 2 * LANES)[:, :, :feat]
    m = dt * MLA_HEADS
    (o,) = _paged_call(
        functools.partial(_mla_sample_body, dt=dt), page_table, cache, layer, (feat, 0),
        [q160, _as_page(new_rows, cache.shape[3])], [(m, MLA_KV_RANK, BF16)], _softmax_scratch(m, MLA_KV_RANK))
    return o.reshape(b, dt, MLA_HEADS * MLA_KV_RANK)


def _out_mla_body(x_ref, g_ref, o_ref, z_ref, wuv_ref, w_ref, lng_ref, lnb_ref, y_ref):
    _finish(_dot(o_ref[0], wuv_ref[...]), z_ref, x_ref, g_ref, w_ref, lng_ref, lnb_ref, y_ref)


def _mla_out_proj(x, gate, o_lat, z, w_uv, w_out, ln_g, ln_b):
    d = x.shape[-1]
    eye = jnp.eye(MLA_HEADS, dtype=BF16)
    wuv_bd = (w_uv.astype(BF16).transpose(1, 0, 2)[:, :, None, :] * eye[:, None, :, None]).reshape(
        MLA_HEADS * MLA_KV_RANK, MLA_HEADS * MLA_V_DIM)
    (y,) = _row_call(_out_mla_body, x, [gate], [o_lat, z],
                     [wuv_bd, w_out.astype(BF16), ln_g.reshape(1, d), ln_b.reshape(1, d)], [(d, F32)], ROW_TILE)
    return y


def _mla_layer(xp, xs, mod_p, mod_s, pos_p, pos_s, w_in, q_norm, w_q_up, kv_norm, w_uk, w_uv, w_out, ln_g, ln_b,
               cache, layer, page_table):
    db, dt, d = xs.shape
    sh_p, sc_p, gt_p = _split_mod(mod_p)
    sh_s, sc_s, gt_s = _split_mod(mod_s, dt)
    xs_flat = xs.reshape(1, db * dt, d)
    tab_p = _rope_tables(pos_p, MLA_ROPE_DIM, MLA_ROPE_THETA, MLA_ROPE_DIM)
    tab_s = _rope_tables(jnp.tile(pos_s, db), MLA_ROPE_DIM, MLA_ROPE_THETA, MLA_ROPE_DIM)
    cq_p, rows_p, kk_p, z_p = _mla_proj(xp, sh_p, sc_p, tab_p, w_in, kv_norm)
    cq_s, rows_s, _, z_s = _mla_proj(xs_flat, sh_s, sc_s, tab_s, w_in, kv_norm)
    q_p = _mla_q(cq_p, tab_p, q_norm, w_q_up, w_uk)
    q_s = _mla_q(cq_s, tab_s, q_norm, w_q_up, w_uk)
    o_p = _mla_prompt_attn(q_p, kk_p)
    rows_s = rows_s.reshape(db, dt, -1)
    o_s = _mla_sample_attn(q_s.reshape(db, dt, -1), rows_s, cache, layer, page_table)
    yp = _mla_out_proj(xp, gt_p, o_p, z_p, w_uv, w_out, ln_g, ln_b)
    ys = _mla_out_proj(xs_flat, gt_s, o_s.reshape(1, db * dt, -1), z_s, w_uv, w_out, ln_g, ln_b)
    return yp, ys.reshape(db, dt, d), rows_p, rows_s


def _diff_proj_body(x_ref, sh_ref, sc_ref, c_ref, s1_ref, s2_ref, wq_ref, wk_ref, wv_ref, wz_ref,
                    q_ref, kv_ref, z_ref):
    h = _modulated(x_ref, sh_ref, sc_ref)
    c, s1, s2 = c_ref[0], s1_ref[0], s2_ref[0]
    half = DIFF_HEAD_DIM // 8
    q = _dot(h, wq_ref[...])
    for ch in range(q.shape[1] // LANES):
        qc = _rope128(q[:, ch * LANES:(ch + 1) * LANES], c, s1, s2, half)
        q_ref[0, :, ch * LANES:(ch + 1) * LANES] = (qc * (DIFF_HEAD_DIM ** -0.5)).astype(BF16)
    k = _dot(h, wk_ref[...])
    nk = k.shape[1]
    for ch in range(nk // LANES):
        kv_ref[0, :, ch * LANES:(ch + 1) * LANES] = _rope128(k[:, ch * LANES:(ch + 1) * LANES], c, s1, s2, half)
    kv_ref[0, :, nk:] = _dot(h, wv_ref[...])
    z_ref[0] = _dot(h, wz_ref[...])


def _diff_proj(x, sh, sc, tables, w_in):
    hd, g, d = DIFF_HEADS, DIFF_KV_HEADS, DIFF_HEAD_DIM
    n1 = hd * 2 * d
    n2 = n1 + g * 2 * d
    n3 = n2 + g * 2 * d
    w = w_in.astype(BF16)
    return _row_call(_diff_proj_body, x, [sh, sc], list(tables), [w[:, :n1], w[:, n1:n2], w[:, n2:n3], w[:, n3:]],
                     [(n1, BF16), (n3 - n1, F32), (w.shape[1] - n3, F32)], ROW_TILE)


def _diff_lambda(lam, lam_init):
    a = jnp.sum(lam[0:1] * lam[1:2], axis=-1, keepdims=True)
    b = jnp.sum(lam[2:3] * lam[3:4], axis=-1, keepdims=True)
    return jnp.exp(a) - jnp.exp(b) + lam_init


def _diff_combine(acc, l, lam, sub, lam_init, n):
    o = acc[:n] / l[:n] - lam * (acc[n:] / l[n:])
    return _rms(o, sub) * (1.0 - lam_init)


def _diff_prompt_body(q_ref, k_ref, v_ref, lam_ref, sub_ref, o_ref, m_ref, l_ref, acc_ref, *, tq, lam_init):
    i = pl.program_id(2)
    lane = lax.broadcasted_iota(jnp.int32, (tq, LANES), 1)
    q = q_ref[0]
    zero = jnp.zeros_like(q)
    q2 = jnp.concatenate([jnp.where(lane < DIFF_HEAD_DIM, q, zero), jnp.where(lane >= DIFF_HEAD_DIM, q, zero)], axis=0)
    def load_k(j):
        return k_ref[0, pl.ds(pl.multiple_of(j * tq, tq), tq), :].astype(BF16)

    def load_v(j):
        return v_ref[0, pl.ds(pl.multiple_of(j * tq, tq), tq), :].astype(BF16)

    def mask_fn(j, diag):
        return _causal_mask(i, j, tq, 2) if diag else None

    _flash_loop(q2, load_k, load_v, mask_fn, 0, i, m_ref, l_ref, acc_ref)
    o_ref[0] = _diff_combine(acc_ref[...], l_ref[...], _diff_lambda(lam_ref[...], lam_init), sub_ref[...], lam_init, tq)


def _diff_prompt_attn(q, kv, lam, subln, lam_init):
    b, t, _ = q.shape
    tq = min(TQ, t)
    hpg = DIFF_HEADS // DIFF_KV_HEADS
    return pl.pallas_call(
        functools.partial(_diff_prompt_body, tq=tq, lam_init=lam_init),
        out_shape=jax.ShapeDtypeStruct((b, t, DIFF_HEADS * LANES), F32),
        grid=(b, DIFF_HEADS, t // tq),
        in_specs=[pl.BlockSpec((1, tq, LANES), lambda b_, n, i: (b_, i, n)),
                  pl.BlockSpec((1, t, LANES), lambda b_, n, i: (b_, 0, n // hpg)),
                  pl.BlockSpec((1, t, LANES), lambda b_, n, i: (b_, 0, DIFF_KV_HEADS + n // hpg)),
                  pl.BlockSpec(lam.shape, lambda b_, n, i: (0, 0)),
                  pl.BlockSpec((1, LANES), lambda b_, n, i: (0, 0))],
        out_specs=pl.BlockSpec((1, tq, LANES), lambda b_, n, i: (b_, i, n)),
        scratch_shapes=_softmax_scratch(2 * tq, LANES),
        compiler_params=_cparams(("arbitrary", "arbitrary", "arbitrary")),
    )(q, kv, kv, lam, subln.reshape(1, LANES))


def _diff_sample_body(pt_ref, q_ref, new_ref, lam_ref, sub_ref, *rest, n_dense, pps, nsteps, dt, lam_init):
    pages, (o_ref, m_ref, l_ref, acc_ref) = rest[:pps], rest[pps:]
    c = pl.program_id(1)
    g = DIFF_KV_HEADS
    stride = 2 * g
    nrow = new_ref.shape[1]
    q = q_ref[0]
    m = q.shape[0]
    mg = m // g
    rows = [slice(gi * mg, (gi + 1) * mg) for gi in range(g)]

    def tiles(pgs, pos_ok):
        n = len(pgs)
        pb = [pg.astype(BF16) for pg in pgs]
        s = jnp.concatenate([_dot_nt(q, p) for p in pb], axis=1)
        col = lax.broadcasted_iota(jnp.int32, (m, n * nrow), 1)
        row_g = lax.broadcasted_iota(jnp.int32, (m, n * nrow), 0) // mg
        mask = (col % stride) == row_g
        if pos_ok is not None:
            mask = mask & pos_ok
        s = jnp.where(mask, s, NEG_INF)
        m_prev = m_ref[...]
        m_new = jnp.maximum(m_prev, jnp.max(s, axis=-1, keepdims=True))
        p = jnp.where(mask, jnp.exp(s - m_new), 0.0)
        scale = jnp.exp(m_prev - m_new)
        l_ref[...] = scale * l_ref[...] + jnp.sum(p, axis=-1, keepdims=True)
        m_ref[...] = m_new
        pv = jnp.zeros(acc_ref.shape, F32)
        for k in range(n):
            pk = pltpu.roll(p[:, k * nrow:(k + 1) * nrow], g, 1).astype(BF16)
            pv = pv + _dot(pk, pb[k])
        acc_ref[...] = scale * acc_ref[...] + pv

    @pl.when(c == 0)
    def _():
        _softmax_init(m_ref, l_ref, acc_ref)
        pos = lax.broadcasted_iota(jnp.int32, (m, nrow), 1) // stride
        tok = _row_token(m, nrow, DIFF_HEADS // g, dt)
        tiles([new_ref[0]], (pos <= tok) & (pos < dt))

    tiles([pages[s][0, 0] for s in range(pps)], None)

    @pl.when(c == nsteps - 1)
    def _():
        lam = _diff_lambda(lam_ref[0], lam_init)
        n = mg // 2
        for gi in range(g):
            r = rows[gi]
            o_ref[0, gi * n:(gi + 1) * n, :] = _diff_combine(acc_ref[r, :], l_ref[r, :], lam, sub_ref[0], lam_init, n)


def _diff_sample_attn(q, new_kv, cache, layer, page_table, lam, subln, lam_init):
    b, dt, _ = q.shape
    g, d = DIFF_KV_HEADS, DIFF_HEAD_DIM
    n = DIFF_HEADS // g
    page = cache.shape[2] // (2 * g)
    q6 = q.reshape(b, dt, g, n, 2, d).transpose(0, 2, 4, 1, 3, 5)
    eye = jnp.eye(2, dtype=q.dtype)
    qrows = (q6[..., None, :] * eye[None, None, :, None, None, :, None]).reshape(b, g * 2 * dt * n, 2 * d)
    new_page = jnp.pad(new_kv.reshape(b, dt, 2 * g, LANES), ((0, 0), (0, page - dt), (0, 0), (0, 0)))
    new_page = new_page.reshape(b, page * 2 * g, LANES)
    m = qrows.shape[1]
    (o,) = _paged_call(
        functools.partial(_diff_sample_body, dt=dt, lam_init=lam_init), page_table, cache, layer,
        (cache.shape[2], 0), [qrows, new_page, jnp.broadcast_to(lam, (b,) + lam.shape),
                              jnp.broadcast_to(subln.reshape(1, 1, LANES), (b, 1, LANES))],
        [(m // 2, LANES, F32)], _softmax_scratch(m, LANES))
    return o.reshape(b, g, dt, n, LANES).transpose(0, 2, 1, 3, 4).reshape(b, dt, DIFF_HEADS * LANES)


def _diff_layer(xp, xs, mod_p, mod_s, pos_p, pos_s, layer_idx, w_in, lam, subln, w_out, ln_g, ln_b,
                cache, layer, page_table):
    db, dt, d = xs.shape
    lam_init = 0.8 - 0.6 * math.exp(-0.3 * layer_idx)
    sh_p, sc_p, gt_p = _split_mod(mod_p)
    sh_s, sc_s, gt_s = _split_mod(mod_s, dt)
    xs_flat = xs.reshape(1, db * dt, d)
    tab_p = _rope_tables(pos_p, DIFF_HEAD_DIM // 4, ROPE_THETA, DIFF_HEAD_DIM)
    tab_s = _rope_tables(jnp.tile(pos_s, db), DIFF_HEAD_DIM // 4, ROPE_THETA, DIFF_HEAD_DIM)
    q_p, kv_p, z_p = _diff_proj(xp, sh_p, sc_p, tab_p, w_in)
    q_s, kv_s, z_s = _diff_proj(xs_flat, sh_s, sc_s, tab_s, w_in)
    o_p = _diff_prompt_attn(q_p, kv_p, lam, subln, lam_init)
    kv_s = kv_s.reshape(db, dt, -1)
    nl, pool, page = cache.shape[:3]
    cache_rows = cache.reshape(nl, pool, page * 2 * DIFF_KV_HEADS, LANES)
    o_s = _diff_sample_attn(q_s.reshape(db, dt, -1), kv_s, cache_rows, layer, page_table, lam, subln, lam_init)
    yp = _out_proj(xp, gt_p, o_p, z_p, w_out, ln_g, ln_b)
    ys = _out_proj(xs_flat, gt_s, o_s.reshape(1, db * dt, -1), z_s, w_out, ln_g, ln_b)
    rows_shape = (2, DIFF_KV_HEADS, 2 * DIFF_HEAD_DIM)
    return yp, ys.reshape(db, dt, d), kv_p.reshape(kv_p.shape[:2] + rows_shape), kv_s.reshape((db, dt) + rows_shape)


NSA_HPG = NSA_HEADS // NSA_KV_GROUPS
GL_PAD = LANES


def _nsa_proj_body(x_ref, sh_ref, sc_ref, c_ref, s1_ref, s2_ref, wq_ref, wkv_ref, ww_ref, wg_ref, wz_ref,
                   q_ref, kv_ref, w_ref, gl_ref, z_ref):
    h = _modulated(x_ref, sh_ref, sc_ref)
    c, s1, s2 = c_ref[0], s1_ref[0], s2_ref[0]
    half = NSA_HEAD_DIM // 8
    q = _dot(h, wq_ref[...])
    for ch in range(q.shape[1] // LANES):
        qc = _rope128(q[:, ch * LANES:(ch + 1) * LANES], c, s1, s2, half)
        q_ref[0, :, ch * LANES:(ch + 1) * LANES] = (qc * (NSA_HEAD_DIM ** -0.5)).astype(BF16)
    kv = _dot(h, wkv_ref[...])
    kv_ref[0, :, :2 * LANES] = kv[:, :2 * LANES]
    kv_ref[0, :, 2 * LANES:3 * LANES] = _rope128(kv[:, 2 * LANES:3 * LANES], c, s1, s2, half)
    kv_ref[0, :, 3 * LANES:] = kv[:, 3 * LANES:]
    w = _dot(h, ww_ref[...])
    w_ref[0, :, :LANES] = _rope128(w[:, :LANES], c, s1, s2, half)
    w_ref[0, :, LANES:] = w[:, LANES:]
    gl_ref[0] = _dot(h, wg_ref[...])
    z_ref[0] = _dot(h, wz_ref[...])


def _nsa_proj(x, sh, sc, tables, w_in):
    hd, g, d = NSA_HEADS, NSA_KV_GROUPS, NSA_HEAD_DIM
    n1 = hd * d
    n2 = n1 + 4 * g * d
    n3 = n2 + 2 * g * d
    n4 = n3 + 3 * hd
    w = w_in.astype(BF16)
    wg = jnp.pad(w[:, n3:n4], ((0, 0), (0, GL_PAD - 3 * hd)))
    return _row_call(_nsa_proj_body, x, [sh, sc], list(tables), [w[:, :n1], w[:, n1:n2], w[:, n2:n3], wg, w[:, n4:]],
                     [(n1, BF16), (n2 - n1, F32), (n3 - n2, F32), (GL_PAD, F32), (w.shape[1] - n4, F32)], ROW_TILE)


def _cmp_weights(pe, w1, b1, w2, b2):
    g = NSA_KV_GROUPS
    eye = jnp.eye(g, dtype=BF16)
    w1b = w1.astype(BF16)
    bd = (w1b[:, :, None, :, None, :] * eye[None, None, :, None, :, None]).reshape(
        2, CMP_BLOCK, g * NSA_HEAD_DIM, g * CMP_HIDDEN)
    wcat = jnp.concatenate([bd[:, :CMP_STRIDE], bd[:, CMP_STRIDE:]], axis=3)
    wcat = wcat.reshape(2, CMP_STRIDE // 2, 2 * g * NSA_HEAD_DIM, 2 * g * CMP_HIDDEN)
    w1f = w1b.reshape(2, CMP_BLOCK * NSA_HEAD_DIM, CMP_HIDDEN)
    pef = jnp.pad(pe.astype(BF16).reshape(2, 1, CMP_BLOCK * NSA_HEAD_DIM), ((0, 0), (0, 7), (0, 0)))
    w2b = w2.astype(BF16)
    w2bd = (w2b[:, None, :, None, :] * eye[None, :, None, :, None]).reshape(2, g * CMP_HIDDEN, g * NSA_HEAD_DIM)
    b2t = jnp.tile(b2, (1, g)).reshape(2, 1, g * NSA_HEAD_DIM)
    return [wcat, w1f, pef, b1.reshape(2, 1, CMP_HIDDEN), w2bd, b2t]


def _compress(get_rows, nblk, wcat_ref, w1f_ref, pe_ref, b1_ref, w2_ref, b2_ref):
    outs = []
    nh = NSA_KV_GROUPS * CMP_HIDDEN
    for comp in range(2):
        acc = jnp.zeros((nblk, 2 * nh), F32)
        for pp in range(CMP_STRIDE // 2):
            x = jnp.concatenate([get_rows(comp, 2 * pp), get_rows(comp, 2 * pp + 1)], axis=1).astype(BF16)
            acc = acc + _dot(x, wcat_ref[comp, pp])
        bias = _dot(pe_ref[comp], w1f_ref[comp])[0:1] + b1_ref[comp]
        bias = jnp.concatenate([bias] * NSA_KV_GROUPS, axis=1)
        hid = _silu(acc[:, :nh] + pltpu.roll(acc[:, nh:], nblk - 1, 0) + bias).astype(BF16)
        outs.append(_dot(hid, w2_ref[comp]) + b2_ref[comp])
    return outs


def _nsa_compress_body(kc_ref, vc_ref, wcat_ref, w1f_ref, pe_ref, b1_ref, w2_ref, b2_ref, k_ref, v_ref, *, nblk):
    def get_rows(comp, p):
        return (kc_ref, vc_ref)[comp][0, pl.ds(p, nblk, stride=CMP_STRIDE), :]
    k, v = _compress(get_rows, nblk, wcat_ref, w1f_ref, pe_ref, b1_ref, w2_ref, b2_ref)
    k_ref[0] = k
    v_ref[0] = v


def _nsa_compress(kv, cw):
    b, t, f = kv.shape
    nblk = t // CMP_STRIDE
    full = lambda a: pl.BlockSpec(a.shape, lambda b_: (0,) * a.ndim)
    return pl.pallas_call(
        functools.partial(_nsa_compress_body, nblk=nblk),
        out_shape=[jax.ShapeDtypeStruct((b, nblk, LANES), F32)] * 2,
        grid=(b,),
        in_specs=[pl.BlockSpec((1, t, LANES), lambda b_: (b_, 0, 0)),
                  pl.BlockSpec((1, t, LANES), lambda b_: (b_, 0, 1))] + [full(a) for a in cw],
        out_specs=[pl.BlockSpec((1, nblk, LANES), lambda b_: (b_, 0, 0))] * 2,
        compiler_params=_cparams(("arbitrary",)),
    )(kv, kv, *cw)


def _place_in_half(x, src_half, dst_half):
    return jnp.where(dst_half == src_half, x, pltpu.roll(x, LANES // 2, 1))


def _pair_queries(q, dst_half):
    q = q.astype(F32)
    lane = lax.broadcasted_iota(jnp.int32, q.shape, 1)
    keep = (lane // (LANES // 2)) == dst_half
    qa = jnp.where(keep, _place_in_half(q, 0, dst_half), 0.0)
    qb = jnp.where(keep, _place_in_half(q, 1, dst_half), 0.0)
    return jnp.concatenate([qa, qb], axis=0).astype(BF16)


def _pair_outputs(o2, src_half):
    tq = o2.shape[0] // 2
    lane = lax.broadcasted_iota(jnp.int32, (tq, LANES), 1)
    return jnp.where(lane < LANES // 2, _place_in_half(o2[:tq], src_half, 0), _place_in_half(o2[tq:], src_half, 1))


def _select_blocks(imp, cur, n_sel, k_eff):
    sp, n = imp.shape
    blk = lax.broadcasted_iota(jnp.int32, (sp, n), 0)
    blkf = blk.astype(F32)
    forced = (blk == 0) | (blk == cur) | (blk == cur - 1)
    allowed = (blk <= cur) & (blk < n_sel)
    score = jnp.where(blk > cur, NEG_INF, jnp.where(forced, FORCE_SCORE, imp))
    lowest = -3.0e38
    score = jnp.where(blk < n_sel, score, lowest)
    sel = jnp.zeros((sp, n), F32)
    for _ in range(k_eff):
        mx = jnp.max(score, axis=0, keepdims=True)
        first = jnp.min(jnp.where(score == mx, blkf, float(sp)), axis=0, keepdims=True)
        hit = blkf == first
        sel = jnp.where(hit, 1.0, sel)
        score = jnp.where(hit, lowest, score)
    return jnp.where(allowed, sel, 0.0)


def _overlap_t(n_sel_pad, n_cmp_pad, n_sel, n_cmp):
    ci = np.arange(n_cmp_pad)[None, :] * CMP_STRIDE
    sj = np.arange(n_sel_pad)[:, None] * SEL_BLOCK
    ov = (ci < sj + SEL_BLOCK) & (ci + CMP_BLOCK > sj)
    ov &= (np.arange(n_cmp_pad)[None, :] < n_cmp) & (np.arange(n_sel_pad)[:, None] < n_sel)
    return jnp.asarray(ov, BF16)


def _cmp_softmax(s, mask):
    s = jnp.where(mask, s, NEG_INF)
    p = jnp.where(mask, jnp.exp(s - jnp.max(s, axis=-1, keepdims=True)), 0.0)
    l = jnp.sum(p, axis=-1, keepdims=True)
    return p / jnp.where(l > 0.0, l, 1.0)


def _nsa_cmpsel_body(q_ref, k_ref, v_ref, ov_ref, o_ref, sel_ref, *, tq, n_sel, k_eff):
    g = pl.program_id(1)
    i = pl.program_id(2)
    nblk = k_ref.shape[1]
    nch = NSA_HPG // 2
    q2s = [_pair_queries(q_ref[0, :, ch * LANES:(ch + 1) * LANES], g) for ch in range(nch)]
    qall = jnp.concatenate(q2s, axis=0)
    s = _dot_nt(qall, k_ref[0].astype(BF16))
    m = qall.shape[0]
    qpos = i * tq + lax.broadcasted_iota(jnp.int32, (m, nblk), 0) % tq
    cmp_end = lax.broadcasted_iota(jnp.int32, (m, nblk), 1) * CMP_STRIDE + (CMP_BLOCK - 1)
    p = _cmp_softmax(s, cmp_end <= qpos)
    o = _dot(p.astype(BF16), v_ref[0].astype(BF16))
    for ch in range(nch):
        o_ref[0, :, ch * LANES:(ch + 1) * LANES] = _pair_outputs(o[2 * ch * tq:(2 * ch + 2) * tq], g)
    psum = p[:tq]
    for n in range(1, NSA_HPG):
        psum = psum + p[n * tq:(n + 1) * tq]
    hi = psum.astype(BF16)
    lo = (psum - hi.astype(F32)).astype(BF16)
    imp_t = _dot_nt(ov_ref[...], hi) + _dot_nt(ov_ref[...], lo)
    sp = imp_t.shape[0]
    cur = (i * tq + lax.broadcasted_iota(jnp.int32, (1, tq), 1)) // SEL_BLOCK
    sel_t = _select_blocks(imp_t, cur, n_sel, k_eff)
    if sp < LANES:
        sel_t = jnp.concatenate([sel_t, jnp.zeros((LANES - sp, tq), F32)], axis=0)
    sel_ref[0, 0] = sel_t.T


def _nsa_cmpsel(q, kcmp, vcmp):
    b, t, f = q.shape
    tq = min(TQ, t)
    nblk = kcmp.shape[1]
    n_sel = -(-t // SEL_BLOCK)
    assert n_sel <= LANES
    sp = -(-n_sel // 8) * 8
    ov = _overlap_t(sp, nblk, n_sel, nblk - 1)
    gw = f // NSA_KV_GROUPS
    return pl.pallas_call(
        functools.partial(_nsa_cmpsel_body, tq=tq, n_sel=n_sel, k_eff=min(N_SELECT, n_sel)),
        out_shape=[jax.ShapeDtypeStruct((b, t, f), F32), jax.ShapeDtypeStruct((b, NSA_KV_GROUPS, t, LANES), F32)],
        grid=(b, NSA_KV_GROUPS, t // tq),
        in_specs=[pl.BlockSpec((1, tq, gw), lambda b_, g, i: (b_, i, g)),
                  pl.BlockSpec((1, nblk, LANES), lambda b_, g, i: (b_, 0, 0)),
                  pl.BlockSpec((1, nblk, LANES), lambda b_, g, i: (b_, 0, 0)),
                  pl.BlockSpec(ov.shape, lambda b_, g, i: (0, 0))],
        out_specs=[pl.BlockSpec((1, tq, gw), lambda b_, g, i: (b_, i, g)),
                   pl.BlockSpec((1, 1, tq, LANES), lambda b_, g, i: (b_, g, i, 0))],
        compiler_params=_cparams(("arbitrary", "arbitrary", "arbitrary")),
    )(q, kcmp, vcmp, ov)


def _block_expand(base_pos, n_blocks_pad, tk):
    blk = lax.broadcasted_iota(jnp.int32, (n_blocks_pad, tk), 0)
    pos = base_pos + lax.broadcasted_iota(jnp.int32, (n_blocks_pad, tk), 1)
    return (blk == pos // SEL_BLOCK).astype(BF16)


def _nsa_prompt_body(*refs, tq, mode):
    if mode == "sel":
        q_ref, k_ref, v_ref, sel_ref, o_ref, m_ref, l_ref, acc_ref = refs
    else:
        q_ref, k_ref, v_ref, o_ref, m_ref, l_ref, acc_ref = refs
    c = pl.program_id(1)
    i = pl.program_id(2)
    g = c // (NSA_HPG // 2)
    q2 = _pair_queries(q_ref[0], g)
    qpos = i * tq + lax.broadcasted_iota(jnp.int32, (tq, tq), 0)
    koff = lax.broadcasted_iota(jnp.int32, (tq, tq), 1)
    if mode == "sel":
        selb = sel_ref[0, 0].astype(BF16)

    def load_k(j):
        return k_ref[0, pl.ds(pl.multiple_of(j * tq, tq), tq), :].astype(BF16)

    def load_v(j):
        return v_ref[0, pl.ds(pl.multiple_of(j * tq, tq), tq), :].astype(BF16)

    def mask_fn(j, diag):
        start = j * tq
        if mode == "sel":
            mask = _dot(selb, _block_expand(start, LANES, tq)) > 0.5
            if diag:
                mask = mask & (start + koff <= qpos)
        else:
            mask = (start + koff <= qpos) & (qpos - (start + koff) < WINDOW)
        return jnp.concatenate([mask, mask], axis=0)

    j_lo = 0 if mode == "sel" else jnp.maximum(i - (WINDOW + tq - 1) // tq, 0)
    _flash_loop(q2, load_k, load_v, mask_fn, j_lo, i, m_ref, l_ref, acc_ref)
    o_ref[0] = _pair_outputs(acc_ref[...] / l_ref[...], g)


def _nsa_prompt_attn(q, kv, k_chunk, v_chunk, sel=None):
    b, t, f = q.shape
    tq = min(TQ, t)
    cpg = NSA_HPG // 2
    in_specs = [pl.BlockSpec((1, tq, LANES), lambda b_, c, i: (b_, i, c)),
                pl.BlockSpec((1, t, LANES), lambda b_, c, i: (b_, 0, k_chunk)),
                pl.BlockSpec((1, t, LANES), lambda b_, c, i: (b_, 0, v_chunk))]
    args = [q, kv, kv]
    if sel is not None:
        in_specs.append(pl.BlockSpec((1, 1, tq, LANES), lambda b_, c, i: (b_, c // cpg, i, 0)))
        args.append(sel)
    return pl.pallas_call(
        functools.partial(_nsa_prompt_body, tq=tq, mode="sel" if sel is not None else "win"),
        out_shape=jax.ShapeDtypeStruct((b, t, f), F32),
        grid=(b, f // LANES, t // tq),
        in_specs=in_specs,
        out_specs=pl.BlockSpec((1, tq, LANES), lambda b_, c, i: (b_, i, c)),
        scratch_shapes=_softmax_scratch(2 * tq, LANES),
        compiler_params=_cparams(("arbitrary", "arbitrary", "arbitrary")),
    )(*args)


def _out_nsa_body(x_ref, g_ref, oc_ref, os_ref, ow_ref, gl_ref, z_ref, e_ref, w_ref, lng_ref, lnb_ref, y_ref):
    sig = _sigmoid(gl_ref[0])
    o = (_split_dot(sig, e_ref[0]) * oc_ref[0] + _split_dot(sig, e_ref[1]) * os_ref[0]
         + _split_dot(sig, e_ref[2]) * ow_ref[0])
    _finish(o, z_ref, x_ref, g_ref, w_ref, lng_ref, lnb_ref, y_ref)


def _nsa_out_proj(x, gate, o_c, o_s, o_w, gl, z, w_out, ln_g, ln_b):
    d = x.shape[-1]
    e = np.zeros((3, GL_PAD, NSA_HEADS * NSA_HEAD_DIM), np.float32)
    for br in range(3):
        for h in range(NSA_HEADS):
            e[br, 3 * h + br, h * NSA_HEAD_DIM:(h + 1) * NSA_HEAD_DIM] = 1.0
    (y,) = _row_call(_out_nsa_body, x, [gate], [o_c, o_s, o_w, gl, z],
                     [jnp.asarray(e, BF16), w_out.astype(BF16), ln_g.reshape(1, d), ln_b.reshape(1, d)],
                     [(d, F32)], ROW_TILE)
    return y


def _row_token(m, n, rows_per_token, dt):
    return (lax.broadcasted_iota(jnp.int32, (m, n), 0) // rows_per_token) % dt


def _nsa_sample_cmp_body(pt_ref, q_ref, new_ref, ov_ref, wcat_ref, w1f_ref, pe_ref, b1_ref, w2_ref, b2_ref, *rest,
                         n_dense, pps, nsteps, dt, past, nblk):
    pages, (oc_ref, imp_ref, xk_ref, xv_ref) = rest[:pps], rest[pps:]
    c = pl.program_id(1)
    page = pages[0].shape[3]

    @pl.when(c == 0)
    def _():
        tail = xk_ref.shape[0] - past
        nw = new_ref[0]
        for x_ref, lo in ((xk_ref, 0), (xv_ref, LANES)):
            x_ref[pl.ds(past, tail), :] = jnp.zeros((tail, LANES), F32)
            x_ref[pl.ds(past, nw.shape[0]), :] = nw[:, lo:lo + LANES]

    for s in range(pps):
        pg = pages[s][0, 0]
        base = pl.multiple_of((c * pps + s) * page, page)
        xk_ref[pl.ds(base, page), :] = pg[:LANES].T
        xv_ref[pl.ds(base, page), :] = pg[LANES:].T

    @pl.when(c == nsteps - 1)
    def _():
        def get_rows(comp, p):
            return (xk_ref, xv_ref)[comp][pl.ds(p, nblk, stride=CMP_STRIDE), :]
        k, v = _compress(get_rows, nblk, wcat_ref, w1f_ref, pe_ref, b1_ref, w2_ref, b2_ref)
        q = q_ref[0]
        m = q.shape[0]
        s = _dot_nt(q, k.astype(BF16))
        qpos = past + _row_token(m, nblk, NSA_HPG, dt)
        cmp_end = lax.broadcasted_iota(jnp.int32, (m, nblk), 1) * CMP_STRIDE + (CMP_BLOCK - 1)
        p = _cmp_softmax(s, cmp_end <= qpos)
        oc_ref[0] = _dot(p.astype(BF16), v.astype(BF16))
        psum = jnp.sum(p.reshape(m // NSA_HPG, NSA_HPG, nblk), axis=1)
        imp_ref[0] = _split_dot(psum, ov_ref[...])


def _select_body(imp_ref, cur_ref, sel_ref, *, n_sel, k_eff):
    sel_ref[...] = _select_blocks(imp_ref[...], cur_ref[...], n_sel, k_eff)


def _nsa_sample_select(imp, past, dt, n_sel):
    b, r, spl = imp.shape
    sp = -(-n_sel // 8) * 8
    n = b * r
    npad = -(-n // LANES) * LANES
    imp_t = jnp.pad(imp.reshape(n, spl)[:, :sp].T, ((0, 0), (0, npad - n)))
    cur = jnp.tile((past + jnp.arange(dt, dtype=jnp.int32)) // SEL_BLOCK, n // dt)
    cur = jnp.pad(cur, (0, npad - n)).reshape(1, npad)
    sel_t = pl.pallas_call(
        functools.partial(_select_body, n_sel=n_sel, k_eff=min(N_SELECT, n_sel)),
        out_shape=jax.ShapeDtypeStruct((sp, npad), F32),
        compiler_params=pltpu.CompilerParams(vmem_limit_bytes=VMEM_LIMIT),
    )(imp_t, cur)
    return jnp.pad(sel_t[:, :n].T, ((0, 0), (0, spl - sp))).reshape(b, r, spl)


def _nsa_sample_sel_body(pt_ref, q_ref, sel_ref, new_ref, st_ref, neww_ref, *rest, n_dense, pps, nsteps, dt, past):
    pages, (os_ref, ow_ref, m_ref, l_ref, acc_ref, m2_ref, l2_ref, acc2_ref) = rest[:pps], rest[pps:]
    c = pl.program_id(1)
    q = q_ref[0]
    selb = sel_ref[0]
    m, spl = selb.shape
    page = pages[0].shape[3]

    def sel_tile(kv_t, base_pos, extra):
        tk = kv_t.shape[1]
        mask = _dot(selb, _block_expand(base_pos, spl, tk)) > 0.5
        if extra is not None:
            mask = mask & extra
        _softmax_tile(_dot(q, kv_t[:LANES].astype(BF16)), mask, kv_t[LANES:].astype(BF16), m_ref, l_ref, acc_ref, True)

    @pl.when(c == 0)
    def _():
        _softmax_init(m_ref, l_ref, acc_ref)
        _softmax_init(m2_ref, l2_ref, acc2_ref)
        tk = new_ref.shape[2]
        new_mask = _new_tile_mask(m, tk, NSA_HPG, dt, True)
        sel_tile(new_ref[0], past, new_mask)
        st = st_ref[0]
        wlen = st.shape[1]
        diff = (past + _row_token(m, wlen, NSA_HPG, dt)) - (past - wlen + lax.broadcasted_iota(jnp.int32, (m, wlen), 1))
        kpos_ok = (past - wlen + lax.broadcasted_iota(jnp.int32, (m, wlen), 1)) >= 0
        _softmax_tile(_dot(q, st[:LANES].astype(BF16)), (diff >= 0) & (diff < WINDOW) & kpos_ok,
                      st[LANES:].astype(BF16), m2_ref, l2_ref, acc2_ref, True)
        nw = neww_ref[0]
        diff_n = _row_token(m, tk, NSA_HPG, dt) - lax.broadcasted_iota(jnp.int32, (m, tk), 1)
        _softmax_tile(_dot(q, nw[:LANES].astype(BF16)), new_mask & (diff_n < WINDOW),
                      nw[LANES:].astype(BF16), m2_ref, l2_ref, acc2_ref, True)

    pgs = [pages[s][0, 0] for s in range(pps)]
    mask = _dot(selb, _block_expand(c * (pps * page), spl, pps * page)) > 0.5
    _softmax_pages(q, [pg[:LANES].astype(BF16) for pg in pgs], [pg[LANES:].astype(BF16) for pg in pgs], mask,
                   m_ref, l_ref, acc_ref)

    @pl.when(c == nsteps - 1)
    def _():
        os_ref[0] = acc_ref[...] / l_ref[...]
        ow_ref[0] = acc2_ref[...] / l2_ref[...]


def _nsa_sample_attn(q, new_kv, new_w, cache, win_state, layer, page_table, cw):
    b, dt, _ = q.shape
    g = NSA_KV_GROUPS
    npg, page = page_table.shape[1], cache.shape[3]
    past = npg * page
    tk_all = past + dt
    n_chunks = -(-tk_all // CMP_STRIDE)
    nblk = -(-n_chunks // LANES) * LANES
    n_sel = -(-tk_all // SEL_BLOCK)
    spl = -(-n_sel // LANES) * LANES
    qg = _group_block_diag(q, g)
    m = qg.shape[1]
    ov = _overlap_t(spl, nblk, n_sel, n_chunks - 1).T
    new_cmp = jnp.pad(new_kv[:, :, :2 * LANES], ((0, 0), (0, CMP_STRIDE - dt), (0, 0)))
    o_c, imp = _paged_call(
        functools.partial(_nsa_sample_cmp_body, dt=dt, past=past, nblk=nblk), page_table, cache, layer, (2 * LANES, 0),
        [qg, new_cmp], [(m, LANES, F32), (m // NSA_HPG, spl, F32)],
        [pltpu.VMEM((nblk * CMP_STRIDE, LANES), F32)] * 2, shared=[ov] + list(cw))
    sel = _nsa_sample_select(imp, past, dt, n_sel)
    sel_rows = jnp.repeat(sel, NSA_HPG, axis=1).astype(BF16)
    o_s, o_w = _paged_call(
        functools.partial(_nsa_sample_sel_body, dt=dt, past=past), page_table, cache, layer, (2 * LANES, 1),
        [qg, sel_rows, _as_page(new_kv[:, :, 2 * LANES:], page), win_state, _as_page(new_w, page)],
        [(m, LANES, F32), (m, LANES, F32)], _softmax_scratch(m, LANES) + _softmax_scratch(m, LANES))
    return tuple(_group_diag_extract(o, g, dt) for o in (o_c, o_s, o_w))


def _nsa_layer(xp, xs, mod_p, mod_s, pos_p, pos_s, w_in, pe, w1, b1, w2, b2, w_out, ln_g, ln_b,
               cache, win_state, layer, page_table):
    db, dt, d = xs.shape
    g, hd = NSA_KV_GROUPS, NSA_HEAD_DIM
    sh_p, sc_p, gt_p = _split_mod(mod_p)
    sh_s, sc_s, gt_s = _split_mod(mod_s, dt)
    xs_flat = xs.reshape(1, db * dt, d)
    tab_p = _rope_tables(pos_p, hd // 4, ROPE_THETA, hd)
    tab_s = _rope_tables(jnp.tile(pos_s, db), hd // 4, ROPE_THETA, hd)
    cw = _cmp_weights(pe, w1, b1, w2, b2)
    q_p, kv_p, w_p, gl_p, z_p = _nsa_proj(xp, sh_p, sc_p, tab_p, w_in)
    q_s, kv_s, w_s, gl_s, z_s = _nsa_proj(xs_flat, sh_s, sc_s, tab_s, w_in)
    kcmp, vcmp = _nsa_compress(kv_p, cw)
    oc_p, sel = _nsa_cmpsel(q_p, kcmp, vcmp)
    os_p = _nsa_prompt_attn(q_p, kv_p, 2, 3, sel)
    ow_p = _nsa_prompt_attn(q_p, w_p, 0, 1)
    yp = _nsa_out_proj(xp, gt_p, oc_p, os_p, ow_p, gl_p, z_p, w_out, ln_g, ln_b)
    kv_s, w_s = kv_s.reshape(db, dt, -1), w_s.reshape(db, dt, -1)
    state = _page_view(win_state)[layer]
    oc_s, os_s, ow_s = _nsa_sample_attn(q_s.reshape(db, dt, -1), kv_s, w_s, cache, state, layer, page_table, cw)
    flat = lambda a: a.reshape(1, db * dt, -1)
    ys = _nsa_out_proj(xs_flat, gt_s, flat(oc_s), flat(os_s), flat(ow_s), gl_s, z_s, w_out, ln_g, ln_b)
    t = xp.shape[1]
    rows_p = kv_p.reshape(kv_p.shape[:2] + (4, g, hd))
    rows_s = kv_s.reshape(db, dt, 4, g, hd)
    win_p = w_p[:, t - min(WINDOW, t):].reshape(xp.shape[0], min(WINDOW, t), 2, g, hd)
    wrows = jnp.concatenate([win_state[layer], w_s.reshape(db, dt, 2, g, hd)], axis=1)
    win_s = wrows[:, wrows.shape[1] - min(WINDOW, wrows.shape[1]):]
    return yp, ys.reshape(db, dt, d), rows_p, rows_s, win_p, win_s


def kernel(x_prompt, x_sample, cache_sb_kv, cache_mla_latent, cache_diff_kv, cache_nsa_kv, state_nsa_window,
           page_table, c_prompt, c_sample, ada_w, ada_b, ln_g, ln_b, sb_w_in, sb_w_out,
           mla_w_in, mla_q_norm, mla_w_q_up, mla_kv_norm, mla_w_uk, mla_w_uv, mla_w_out,
           diff_w_in, diff_lam, diff_subln, diff_w_out,
           nsa_w_in, nsa_cmp_pe, nsa_cmp_w1, nsa_cmp_b1, nsa_cmp_w2, nsa_cmp_b2, nsa_w_out):
    nb = x_prompt.shape[0]
    mods = _adaln(jnp.concatenate([c_prompt, c_sample], axis=0), ada_w, ada_b)
    past = page_table.shape[1] * cache_sb_kv.shape[2]
    pos_p = jnp.arange(x_prompt.shape[1], dtype=jnp.int32)
    pos_s = past + jnp.arange(x_sample.shape[1], dtype=jnp.int32)
    xp, xs, sb_p, sb_s = _sb_layer(x_prompt, x_sample, mods[0, :nb], mods[0, nb:], sb_w_in[0], sb_w_out[0],
                                   ln_g[0], ln_b[0], _page_view(cache_sb_kv), 0, page_table)
    xp, xs, mla_p, mla_s = _mla_layer(xp, xs, mods[1, :nb], mods[1, nb:], pos_p, pos_s, mla_w_in[0], mla_q_norm[0],
                                      mla_w_q_up[0], mla_kv_norm[0], mla_w_uk[0], mla_w_uv[0], mla_w_out[0],
                                      ln_g[1], ln_b[1], _page_view(cache_mla_latent), 0, page_table)
    xp, xs, diff_p, diff_s = _diff_layer(xp, xs, mods[2, :nb], mods[2, nb:], pos_p, pos_s, 2, diff_w_in[0],
                                         diff_lam[0], diff_subln[0], diff_w_out[0], ln_g[2], ln_b[2],
                                         cache_diff_kv, 0, page_table)
    xp, xs, nsa_p, nsa_s, win_p, win_s = _nsa_layer(
        xp, xs, mods[3, :nb], mods[3, nb:], pos_p, pos_s, nsa_w_in[0], nsa_cmp_pe[0], nsa_cmp_w1[0], nsa_cmp_b1[0],
        nsa_cmp_w2[0], nsa_cmp_b2[0], nsa_w_out[0], ln_g[3], ln_b[3], _page_view(cache_nsa_kv), state_nsa_window,
        0, page_table)
    return (xp, xs, sb_p[None], sb_s[None], mla_p[None], mla_s[None], diff_p[None], diff_s[None],
            nsa_p[None], nsa_s[None], win_p[None], win_s[None])
```

```python
import functools
import math

import numpy as np
import jax
import jax.numpy as jnp
from jax import lax
from jax.experimental import pallas as pl
from jax.experimental.pallas import tpu as pltpu

F32 = jnp.float32
BF16 = jnp.bfloat16

DEPTH = 4
ALPHA = (2 * DEPTH) ** 0.25
NORM_EPS = 1e-5
NEG_INF = -1e30
FORCE_SCORE = 1e9
ROPE_THETA = 500000.0

SB_HEADS, SB_KV_HEADS, SB_HEAD_DIM = 16, 4, 64
MLA_HEADS, MLA_Q_RANK, MLA_KV_RANK = 16, 256, 128
MLA_NOPE_DIM, MLA_ROPE_DIM, MLA_V_DIM = 64, 32, 64
MLA_ROPE_THETA = 10000.0
DIFF_HEADS, DIFF_KV_HEADS, DIFF_HEAD_DIM = 8, 2, 64
NSA_HEADS, NSA_KV_GROUPS, NSA_HEAD_DIM = 16, 2, 64
CMP_BLOCK, CMP_STRIDE, CMP_HIDDEN = 32, 16, 128
SEL_BLOCK, N_SELECT, WINDOW = 64, 16, 512

LANES = 128
VMEM_LIMIT = 48 * 1024 * 1024
PAGES_PER_STEP = 16
TQ = 256
ROW_TILE = 256


def _cparams(sem):
    return pltpu.CompilerParams(dimension_semantics=sem, vmem_limit_bytes=VMEM_LIMIT)


def _dot(a, b):
    return jnp.dot(a, b, preferred_element_type=F32)


def _dot_nt(a, b):
    return lax.dot_general(a, b, (((1,), (1,)), ((), ())), preferred_element_type=F32)


def _silu(x):
    return x / (1.0 + jnp.exp(-x))


def _sigmoid(x):
    return 1.0 / (1.0 + jnp.exp(-x))


def _split_dot(x, w_bf16):
    hi = x.astype(BF16)
    lo = (x - hi.astype(F32)).astype(BF16)
    return _dot(hi, w_bf16) + _dot(lo, w_bf16)


def _rope128(x, c, s1, s2, half):
    return x * c + pltpu.roll(x, LANES - half, 1) * s1 + pltpu.roll(x, half, 1) * s2


def _rope_tables(pos, rot_dim, theta, period):
    half = rot_dim // 2
    inv = theta ** (-jnp.arange(half, dtype=F32) / half)
    ang = pos.astype(F32)[:, None] * inv
    cos, sin = jnp.cos(ang), jnp.sin(ang)
    n = pos.shape[0]
    ones = jnp.ones((n, period - rot_dim), F32)
    zeros_h = jnp.zeros((n, half), F32)
    zeros_r = jnp.zeros((n, period - rot_dim), F32)
    c = jnp.concatenate([cos, cos, ones], axis=1)
    s1 = jnp.concatenate([-sin, zeros_h, zeros_r], axis=1)
    s2 = jnp.concatenate([zeros_h, sin, zeros_r], axis=1)
    rep = LANES // period
    return tuple(jnp.tile(t, (1, rep))[None] for t in (c, s1, s2))


def _adaln_body(c_ref, w_ref, b_ref, o_ref):
    s = _silu(c_ref[...]).astype(BF16)
    o_ref[0] = _dot(s, w_ref[0].astype(BF16)) + b_ref[0]


def _adaln(c_all, ada_w, ada_b):
    depth, d, d3 = ada_w.shape
    n = c_all.shape[0]
    nt = d3 // d
    return pl.pallas_call(
        _adaln_body,
        out_shape=jax.ShapeDtypeStruct((depth, n, d3), F32),
        grid=(depth, nt),
        in_specs=[pl.BlockSpec((n, d), lambda l, j: (0, 0)),
                  pl.BlockSpec((1, d, d), lambda l, j: (l, 0, j)),
                  pl.BlockSpec((1, 1, d), lambda l, j: (l, 0, j))],
        out_specs=pl.BlockSpec((1, n, d), lambda l, j: (l, 0, j)),
        compiler_params=_cparams(("arbitrary", "arbitrary")),
    )(c_all, ada_w, ada_b.reshape(depth, 1, d3))


def _row_call(body, x, mods, row_inputs, full_inputs, out_defs, tm):
    bt, r, _ = x.shape
    tm = min(tm, r)
    grid = (bt, r // tm)

    def tiled(a):
        lead = a.shape[0]
        if a.shape[1] == 1:
            return pl.BlockSpec((1, 1, a.shape[2]), lambda b, i: (b, 0, 0))
        if lead == 1 and bt > 1:
            return pl.BlockSpec((1, tm, a.shape[2]), lambda b, i: (0, i, 0))
        return pl.BlockSpec((1, tm, a.shape[2]), lambda b, i: (b, i, 0))

    def full(a):
        nd = a.ndim
        return pl.BlockSpec(a.shape, lambda b, i: (0,) * nd)

    in_specs = [tiled(x)] + [tiled(m) for m in mods] + [tiled(t) for t in row_inputs] + [full(w) for w in full_inputs]
    out_shape, out_specs = [], []
    for w, dt in out_defs:
        if w == "t":
            out_shape.append(jax.ShapeDtypeStruct((bt, dt, r // tm, LANES, tm), BF16))
            out_specs.append(pl.BlockSpec((1, dt, 1, LANES, tm), lambda b, i: (b, 0, i, 0, 0)))
        else:
            out_shape.append(jax.ShapeDtypeStruct((bt, r, w), dt))
            out_specs.append(pl.BlockSpec((1, tm, w), lambda b, i: (b, i, 0)))
    return pl.pallas_call(
        body, out_shape=out_shape, grid=grid, in_specs=in_specs, out_specs=out_specs,
        compiler_params=_cparams(("arbitrary", "arbitrary")),
    )(x, *mods, *row_inputs, *full_inputs)


def _modulated(x_ref, sh_ref, sc_ref):
    return (x_ref[0] * (1.0 + sc_ref[0]) + sh_ref[0]).astype(BF16)


def _store_transposed(t_ref, x):
    for ch in range(x.shape[1] // LANES):
        t_ref[0, ch, 0] = x[:, ch * LANES:(ch + 1) * LANES].T.astype(BF16)


def _finish(o, z_ref, x_ref, g_ref, w_ref, lng_ref, lnb_ref, y_ref):
    gated = (o * _silu(z_ref[0])).astype(BF16)
    y = _dot(gated, w_ref[...])
    r = ALPHA * x_ref[0] + (1.0 + g_ref[0]) * y
    mu = jnp.mean(r, axis=-1, keepdims=True)
    d = r - mu
    var = jnp.mean(d * d, axis=-1, keepdims=True)
    y_ref[0] = d * lax.rsqrt(var + NORM_EPS) * lng_ref[...] + lnb_ref[...]


def _out_plain_body(x_ref, g_ref, o_ref, z_ref, w_ref, lng_ref, lnb_ref, y_ref):
    _finish(o_ref[0].astype(F32), z_ref, x_ref, g_ref, w_ref, lng_ref, lnb_ref, y_ref)


def _out_proj(x, gate, o, z, w_out, ln_g, ln_b):
    d = x.shape[-1]
    (y,) = _row_call(_out_plain_body, x, [gate], [o, z], [w_out.astype(BF16), ln_g.reshape(1, d), ln_b.reshape(1, d)],
                     [(d, F32)], ROW_TILE)
    return y


def _sb_proj_body(x_ref, sh_ref, sc_ref, wq_ref, wkv_ref, wz_ref, q_ref, kv_ref, z_ref, vt_ref):
    h = _modulated(x_ref, sh_ref, sc_ref)
    q_ref[0] = (_dot(h, wq_ref[...]) * (SB_HEAD_DIM ** -0.5)).astype(BF16)
    kv = _dot(h, wkv_ref[...])
    kv_ref[0] = kv
    _store_transposed(vt_ref, kv[:, kv.shape[1] // 2:])
    z_ref[0] = _dot(h, wz_ref[...])


def _sb_proj(x, sh, sc, w_in):
    nq = SB_HEADS * SB_HEAD_DIM
    nkv = 2 * SB_KV_HEADS * SB_HEAD_DIM
    w = w_in.astype(BF16)
    return _row_call(_sb_proj_body, x, [sh, sc], [], [w[:, :nq], w[:, nq:nq + nkv], w[:, nq + nkv:]],
                     [(nq, BF16), (nkv, F32), (nq, F32), ("t", nkv // (2 * LANES))], ROW_TILE)


def _softplus(z):
    return jnp.maximum(z, 0.0) + jnp.log(1.0 + jnp.exp(-jnp.abs(z)))


def _sb_tile(q2, k, v, mask, tri, r_prev, keys_on_lanes=False):
    z = _dot(q2, k) if keys_on_lanes else _dot_nt(q2, k)
    sp = _softplus(z)
    u = sp if mask is None else jnp.where(mask, sp, 0.0)
    log_rem = _split_dot(u, tri) + r_prev
    a = jnp.exp(z - sp - log_rem)
    if mask is not None:
        a = jnp.where(mask, a, 0.0)
    a = a.astype(BF16)
    pv = _dot_nt(a, v) if keys_on_lanes else _dot(a, v)
    return pv, r_prev + jnp.sum(u, axis=-1, keepdims=True)


def _tri(tk):
    return (lax.broadcasted_iota(jnp.int32, (tk, tk), 0) > lax.broadcasted_iota(jnp.int32, (tk, tk), 1)).astype(BF16)


def _sb_prompt_body(q_ref, k_ref, vt_ref, o_ref, acc_ref, *, tq):
    c = pl.program_id(1)
    i = pl.program_id(2)
    half = (c // 2) % 2
    lane = lax.broadcasted_iota(jnp.int32, (tq, LANES), 1)
    in_half = (lane // SB_HEAD_DIM) == half
    q = q_ref[0].astype(F32)
    qr = pltpu.roll(q, SB_HEAD_DIM, 1)
    first = half == 0
    qa = jnp.where(in_half, jnp.where(first, q, qr), 0.0)
    qb = jnp.where(in_half, jnp.where(first, qr, q), 0.0)
    q2 = jnp.concatenate([qa, qb], axis=0).astype(BF16)
    row = lax.broadcasted_iota(jnp.int32, (tq, tq), 0)
    col = lax.broadcasted_iota(jnp.int32, (tq, tq), 1)
    tri_t = (col > row).astype(BF16)
    diag_mask = jnp.concatenate([row < col] * 2, axis=1)
    acc_ref[...] = jnp.zeros_like(acc_ref)

    def scores(j):
        k = k_ref[0, pl.ds(pl.multiple_of(j * tq, tq), tq), :].astype(BF16)
        return _dot_nt(k, q2)

    def weights(z, mask, r_prev):
        sp = _softplus(z)
        u = sp if mask is None else jnp.where(mask, sp, 0.0)
        cum = _dot(tri_t, u.astype(BF16))
        a = jnp.exp(z - sp - (cum + r_prev))
        if mask is not None:
            a = jnp.where(mask, a, 0.0)
        return a.astype(BF16), r_prev + jnp.sum(u, axis=0, keepdims=True)

    a_prev, r = weights(scores(i), diag_mask, jnp.zeros((1, 2 * tq), F32))
    z = scores(jnp.maximum(i - 1, 0))

    def step(jj, carry):
        z, a_prev, r = carry
        j = i - jj
        acc_ref[...] += _dot(vt_ref[0, 0, j + 1], a_prev)
        z_next = scores(jnp.maximum(j - 1, 0))
        a, r = weights(z, None, r)
        return z_next, a, r

    _, a_prev, _ = lax.fori_loop(1, i + 1, step, (z, a_prev, r))
    acc_t = acc_ref[...] + _dot(vt_ref[0, 0, 0], a_prev)
    oa, ob = acc_t[:, :tq].T, acc_t[:, tq:].T
    oar, obr = pltpu.roll(oa, SB_HEAD_DIM, 1), pltpu.roll(ob, SB_HEAD_DIM, 1)
    low = lane < SB_HEAD_DIM
    o_ref[0] = jnp.where(low, jnp.where(first, oa, oar), jnp.where(first, obr, ob))


def _vt_spec(vt, chunk_of):
    return pl.BlockSpec((1, 1) + vt.shape[2:], lambda b_, c, i: (b_, chunk_of(c), 0, 0, 0))


def _sb_prompt_attn(q, kv, vt):
    b, t, _ = q.shape
    tq = min(TQ, t)
    assert vt.shape[-1] == tq
    nchunk = SB_HEADS * SB_HEAD_DIM // LANES
    return pl.pallas_call(
        functools.partial(_sb_prompt_body, tq=tq),
        out_shape=jax.ShapeDtypeStruct((b, t, nchunk * LANES), F32),
        grid=(b, nchunk, t // tq),
        in_specs=[pl.BlockSpec((1, tq, LANES), lambda b_, c, i: (b_, i, c)),
                  pl.BlockSpec((1, t, LANES), lambda b_, c, i: (b_, 0, c // 4)),
                  _vt_spec(vt, lambda c: c // 4)],
        out_specs=pl.BlockSpec((1, tq, LANES), lambda b_, c, i: (b_, i, c)),
        scratch_shapes=[pltpu.VMEM((LANES, 2 * tq), F32)],
        compiler_params=_cparams(("arbitrary", "arbitrary", "arbitrary")),
    )(q, kv, vt)


def _paged_call(body, page_table, cache, layer, row_block, dense, out_defs, scratch, descending=False, shared=()):
    nb, npg = page_table.shape
    pps = min(PAGES_PER_STEP, npg)
    nsteps = npg // pps
    page = cache.shape[3]
    rb, ridx = row_block

    def page_spec(s):
        def imap(b, c, pt):
            p = c * pps + s
            if descending:
                p = npg - 1 - p
            return (layer, pt[b, p], ridx, 0)
        return pl.BlockSpec((1, 1, rb, page), imap)

    def shared_spec(a):
        nd = a.ndim
        return pl.BlockSpec(a.shape, lambda b, c, pt: (0,) * nd)

    in_specs = [pl.BlockSpec((1,) + a.shape[1:], lambda b, c, pt: (b, 0, 0)) for a in dense]
    in_specs += [shared_spec(a) for a in shared]
    in_specs += [page_spec(s) for s in range(pps)]
    out_shape = [jax.ShapeDtypeStruct((nb, r, w), dt) for r, w, dt in out_defs]
    out_specs = [pl.BlockSpec((1, r, w), lambda b, c, pt: (b, 0, 0)) for r, w, _ in out_defs]
    grid_spec = pltpu.PrefetchScalarGridSpec(num_scalar_prefetch=1, grid=(nb, nsteps), in_specs=in_specs,
                                             out_specs=out_specs, scratch_shapes=scratch)
    return pl.pallas_call(
        functools.partial(body, n_dense=len(dense) + len(shared), pps=pps, nsteps=nsteps),
        out_shape=out_shape, grid_spec=grid_spec,
        compiler_params=_cparams(("arbitrary", "arbitrary")),
    )(page_table, *dense, *shared, *([cache] * pps))


def _as_page(rows, page):
    return jnp.pad(rows.transpose(0, 2, 1), ((0, 0), (0, 0), (0, page - rows.shape[1])))


def _sb_sample_body(pt_ref, q_ref, new_ref, *rest, n_dense, pps, nsteps, dt):
    pages, (o_ref, acc_ref, r_ref) = rest[:pps], rest[pps:]
    c = pl.program_id(1)
    nk = SB_KV_HEADS * SB_HEAD_DIM
    q2 = q_ref[0]
    m, tk = q2.shape[0], new_ref.shape[2]
    tri = _tri(tk)

    @pl.when(c == 0)
    def _():
        row_t = (lax.broadcasted_iota(jnp.int32, (m, tk), 0) // (SB_HEADS // SB_KV_HEADS)) % dt
        key = lax.broadcasted_iota(jnp.int32, (m, tk), 1)
        mask = (key < row_t) & (key < dt)
        nw = new_ref[0]
        pv, r_new = _sb_tile(q2, nw[:nk].astype(BF16), nw[nk:].astype(BF16), mask, tri, jnp.zeros((m, 1), F32), True)
        acc_ref[...] = pv
        r_ref[...] = r_new

    pgs = [pages[s][0, 0] for s in range(pps)]
    zs = [_dot(q2, pg[:nk].astype(BF16)) for pg in pgs]
    sps = [_softplus(z) for z in zs]
    cums = _dot(jnp.concatenate(sps, axis=0).astype(BF16), tri)
    r = r_ref[...]
    pv = jnp.zeros(acc_ref.shape, F32)
    for s in range(pps):
        a = jnp.exp(zs[s] - sps[s] - (cums[s * m:(s + 1) * m] + r))
        pv = pv + _dot_nt(a.astype(BF16), pgs[s][nk:].astype(BF16))
        r = r + jnp.sum(sps[s], axis=-1, keepdims=True)
    acc_ref[...] += pv
    r_ref[...] = r

    @pl.when(c == nsteps - 1)
    def _():
        o_ref[0] = acc_ref[...]


def _group_block_diag(q, groups):
    b, t, f = q.shape
    d = SB_HEAD_DIM
    n = f // (groups * d)
    q5 = q.reshape(b, t, groups, n, d)
    eye = jnp.eye(groups, dtype=q.dtype)
    out = q5.transpose(0, 2, 1, 3, 4)[:, :, :, :, None, :] * eye[None, :, None, None, :, None]
    return out.reshape(b, groups * t * n, groups * d)


def _group_diag_extract(acc, groups, t):
    b, r, f = acc.shape
    w = f // groups
    n = r // (groups * t)
    a5 = acc.reshape(b, groups, t * n, groups, w)
    picked = jnp.stack([a5[:, g, :, g, :] for g in range(groups)], axis=1)
    return picked.reshape(b, groups, t, n, w).transpose(0, 2, 1, 3, 4).reshape(b, t, groups * n * w)


def _sb_sample_attn(q, new_kv, cache, layer, page_table):
    b, dt, _ = q.shape
    page = cache.shape[3]
    qbd = _group_block_diag(q, SB_KV_HEADS)
    m = qbd.shape[1]
    nk = SB_KV_HEADS * SB_HEAD_DIM
    (acc,) = _paged_call(
        functools.partial(_sb_sample_body, dt=dt), page_table, cache, layer, (2 * nk, 0),
        [qbd, _as_page(new_kv, page)], [(m, nk, F32)],
        [pltpu.VMEM((m, nk), F32), pltpu.VMEM((m, 1), F32)], descending=True)
    return _group_diag_extract(acc, SB_KV_HEADS, dt)


def _page_view(cache):
    nl, pool, page = cache.shape[:3]
    feat = math.prod(cache.shape[3:])
    return jnp.moveaxis(cache.reshape(nl, pool, page, feat), 2, 3)


def _split_mod(mod, per_row_t=None):
    d = mod.shape[-1] // 3
    parts = [mod[:, k * d:(k + 1) * d] for k in range(3)]
    if per_row_t is None:
        return [p[:, None, :] for p in parts]
    return [jnp.repeat(p, per_row_t, axis=0)[None] for p in parts]


def _sb_layer(xp, xs, mod_p, mod_s, w_in, w_out, ln_g, ln_b, cache, layer, page_table):
    db, dt, d = xs.shape
    sh_p, sc_p, gt_p = _split_mod(mod_p)
    sh_s, sc_s, gt_s = _split_mod(mod_s, dt)
    xs_flat = xs.reshape(1, db * dt, d)
    qp, kvp, zp, vtp = _sb_proj(xp, sh_p, sc_p, w_in)
    qs, kvs, zs, _ = _sb_proj(xs_flat, sh_s, sc_s, w_in)
    op = _sb_prompt_attn(qp, kvp, vtp)
    os_ = _sb_sample_attn(qs.reshape(db, dt, -1), kvs.reshape(db, dt, -1), cache, layer, page_table)
    yp = _out_proj(xp, gt_p, op, zp, w_out, ln_g, ln_b)
    ys = _out_proj(xs_flat, gt_s, os_.reshape(1, db * dt, -1), zs, w_out, ln_g, ln_b)
    rows_shape = (2, SB_KV_HEADS, SB_HEAD_DIM)
    return (yp, ys.reshape(db, dt, d), kvp.reshape(kvp.shape[:2] + rows_shape), kvs.reshape((db, dt) + rows_shape))


def _softmax_tile(s, mask, v, m_ref, l_ref, acc_ref, keys_on_lanes=False):
    if mask is not None:
        s = jnp.where(mask, s, NEG_INF)
    m_prev = m_ref[...]
    m_new = jnp.maximum(m_prev, jnp.max(s, axis=-1, keepdims=True))
    p = jnp.exp(s - m_new)
    if mask is not None:
        p = jnp.where(mask, p, 0.0)
    scale = jnp.exp(m_prev - m_new)
    pb = p.astype(BF16)
    pv = _dot_nt(pb, v) if keys_on_lanes else _dot(pb, v)
    l_ref[...] = scale * l_ref[...] + jnp.sum(p, axis=-1, keepdims=True)
    acc_ref[...] = scale * acc_ref[...] + pv
    m_ref[...] = m_new


def _softmax_init(m_ref, l_ref, acc_ref):
    m_ref[...] = jnp.full_like(m_ref, NEG_INF)
    l_ref[...] = jnp.zeros_like(l_ref)
    acc_ref[...] = jnp.zeros_like(acc_ref)


def _softmax_scratch(m, f):
    return [pltpu.VMEM((m, 1), F32), pltpu.VMEM((m, 1), F32), pltpu.VMEM((m, f), F32)]


def _softmax_pages(q, kts, vts, mask, m_ref, l_ref, acc_ref):
    s = jnp.concatenate([_dot(q, kt) for kt in kts], axis=1)
    p, scale = _softmax_stage(s, mask, m_ref, l_ref)
    tk = kts[0].shape[1]
    pv = _dot_nt(p[:, :tk], vts[0])
    for n in range(1, len(vts)):
        pv = pv + _dot_nt(p[:, n * tk:(n + 1) * tk], vts[n])
    acc_ref[...] = scale * acc_ref[...] + pv


def _rms(x, g):
    return x * lax.rsqrt(jnp.mean(x * x, axis=-1, keepdims=True) + NORM_EPS) * g


def _mla_proj_body(x_ref, sh_ref, sc_ref, c_ref, s1_ref, s2_ref, wcq_ref, wckv_ref, wpe_ref, wz_ref, g_ref,
                   cq_ref, rows_ref, kk_ref, z_ref, latt_ref):
    h = _modulated(x_ref, sh_ref, sc_ref)
    cq_ref[0] = _dot(h, wcq_ref[...])
    lat = _rms(_dot(h, wckv_ref[...]), g_ref[...])
    pe = _rope128(_dot(h, wpe_ref[...]), c_ref[0], s1_ref[0], s2_ref[0], MLA_ROPE_DIM // 2)
    rows_ref[0, :, :MLA_KV_RANK] = lat
    rows_ref[0, :, MLA_KV_RANK:] = pe[:, :MLA_ROPE_DIM]
    kk_ref[0] = jnp.concatenate([lat, pe], axis=1).astype(BF16)
    _store_transposed(latt_ref, lat)
    z_ref[0] = _dot(h, wz_ref[...])


def _mla_proj(x, sh, sc, tables, w_in, kv_norm):
    r, pe = MLA_KV_RANK, MLA_ROPE_DIM
    w = w_in.astype(BF16)
    n0, n1, n2 = MLA_Q_RANK, MLA_Q_RANK + r, MLA_Q_RANK + r + pe
    wpe = jnp.pad(w[:, n1:n2], ((0, 0), (0, LANES - pe)))
    nz = w.shape[1] - n2
    return _row_call(_mla_proj_body, x, [sh, sc], list(tables),
                     [w[:, :n0], w[:, n0:n1], wpe, w[:, n2:], kv_norm.reshape(1, r)],
                     [(n0, F32), (r + pe, F32), (2 * LANES, BF16), (nz, F32), ("t", r // LANES)], ROW_TILE)


def _mla_q_body(cq_ref, c_ref, s1_ref, s2_ref, g_ref, wn_ref, wuk_ref, wp_ref, q_ref):
    nb = _rms(cq_ref[0], g_ref[...]).astype(BF16)
    scale = (MLA_NOPE_DIM + MLA_ROPE_DIM) ** -0.5
    c, s1, s2 = c_ref[0], s1_ref[0], s2_ref[0]
    for h in range(MLA_HEADS):
        nope = _dot(nb, wn_ref[h]).astype(BF16)
        q_ref[0, :, 2 * h * LANES:(2 * h + 1) * LANES] = (_dot(nope, wuk_ref[h]) * scale).astype(BF16)
        pe = _rope128(_dot(nb, wp_ref[h]), c, s1, s2, MLA_ROPE_DIM // 2)
        q_ref[0, :, (2 * h + 1) * LANES:(2 * h + 2) * LANES] = (pe * scale).astype(BF16)


def _mla_q(cq, tables, q_norm, w_q_up, w_uk):
    hds, dn, dr = MLA_HEADS, MLA_NOPE_DIM, MLA_ROPE_DIM
    wq = w_q_up.astype(BF16).reshape(MLA_Q_RANK, hds, dn + dr)
    wn = wq[:, :, :dn].transpose(1, 0, 2)
    wp = jnp.pad(wq[:, :, dn:].transpose(1, 0, 2), ((0, 0), (0, 0), (0, LANES - dr)))
    wuk = w_uk.astype(BF16).transpose(1, 2, 0)
    (q,) = _row_call(_mla_q_body, cq, [], list(tables), [q_norm.reshape(1, MLA_Q_RANK), wn, wuk, wp],
                     [(hds * 2 * LANES, BF16)], ROW_TILE)
    return q


def _causal_mask(i, j, tq, rows):
    qpos = i * tq + lax.broadcasted_iota(jnp.int32, (tq, tq), 0)
    kpos = j * tq + lax.broadcasted_iota(jnp.int32, (tq, tq), 1)
    mask = kpos <= qpos
    return mask if rows == 1 else jnp.concatenate([mask] * rows, axis=0)


def _softmax_stage(s, mask, m_ref, l_ref, axis=-1):
    if mask is not None:
        s = jnp.where(mask, s, NEG_INF)
    m_prev = m_ref[...]
    m_new = jnp.maximum(m_prev, jnp.max(s, axis=axis, keepdims=True))
    p = jnp.exp(s - m_new)
    if mask is not None:
        p = jnp.where(mask, p, 0.0)
    scale = jnp.exp(m_prev - m_new)
    l_ref[...] = scale * l_ref[...] + jnp.sum(p, axis=axis, keepdims=True)
    m_ref[...] = m_new
    return p.astype(BF16), scale


def _flash_loop(q, load_k, load_v, mask_fn, j_lo, i, m_ref, l_ref, acc_ref, transposed=False):
    _softmax_init(m_ref, l_ref, acc_ref)
    axis = 0 if transposed else -1

    def scores(j):
        return _dot_nt(load_k(j), q) if transposed else _dot_nt(q, load_k(j))

    def pv_update(p_prev, scale_prev, j_prev):
        pv = _dot(load_v(j_prev), p_prev) if transposed else _dot(p_prev, load_v(j_prev))
        acc_ref[...] = scale_prev * acc_ref[...] + pv

    def step(j, carry):
        s, p_prev, scale_prev = carry
        pv_update(p_prev, scale_prev, jnp.maximum(j - 1, j_lo))
        s_next = scores(j + 1)
        p, scale = _softmax_stage(s, mask_fn(j, False), m_ref, l_ref, axis)
        return s_next, p, scale

    s0 = scores(j_lo)
    init = (s0, jnp.zeros(s0.shape, BF16), jnp.ones(m_ref.shape, F32))
    s, p_prev, scale_prev = lax.fori_loop(j_lo, i, step, init)
    pv_update(p_prev, scale_prev, jnp.maximum(i - 1, j_lo))
    p, scale = _softmax_stage(s, mask_fn(i, True), m_ref, l_ref, axis)
    pv_update(p, scale, i)


def _softmax_scratch_t(m, f):
    return [pltpu.VMEM((1, m), F32), pltpu.VMEM((1, m), F32), pltpu.VMEM((f, m), F32)]


def _causal_mask_t(i, j, tq, cols):
    kpos = j * tq + lax.broadcasted_iota(jnp.int32, (tq, tq), 0)
    qpos = i * tq + lax.broadcasted_iota(jnp.int32, (tq, tq), 1)
    mask = kpos <= qpos
    return mask if cols == 1 else jnp.concatenate([mask] * cols, axis=1)


def _mla_prompt_body(q_ref, kk_ref, vt_ref, o_ref, m_ref, l_ref, acc_ref, *, tq):
    i = pl.program_id(2)

    def load_k(j):
        return kk_ref[0, pl.ds(pl.multiple_of(j * tq, tq), tq), :]

    def load_v(j):
        return vt_ref[0, 0, j]

    def mask_fn2(j, diag):
        return _causal_mask_t(i, j, tq, 2) if diag else None

    q = q_ref[0]
    q2 = jnp.concatenate([q[:, :2 * LANES], q[:, 2 * LANES:]], axis=0)
    _flash_loop(q2, load_k, load_v, mask_fn2, 0, i, m_ref, l_ref, acc_ref, transposed=True)
    o_t = acc_ref[...] / l_ref[...]
    o_ref[0, :, :MLA_KV_RANK] = o_t[:, :tq].T.astype(BF16)
    o_ref[0, :, MLA_KV_RANK:] = o_t[:, tq:].T.astype(BF16)


def _mla_prompt_attn(q, kk, latt):
    b, t, _ = kk.shape
    tq = min(TQ, t)
    assert latt.shape[-1] == tq
    return pl.pallas_call(
        functools.partial(_mla_prompt_body, tq=tq),
        out_shape=jax.ShapeDtypeStruct((b, t, MLA_HEADS * MLA_KV_RANK), BF16),
        grid=(b, MLA_HEADS // 2, t // tq),
        in_specs=[pl.BlockSpec((1, tq, 4 * LANES), lambda b_, h, i: (b_, i, h)),
                  pl.BlockSpec((1, t, 2 * LANES), lambda b_, h, i: (b_, 0, 0)),
                  _vt_spec(latt, lambda h: 0)],
        out_specs=pl.BlockSpec((1, tq, 2 * MLA_KV_RANK), lambda b_, h, i: (b_, i, h)),
        scratch_shapes=_softmax_scratch_t(2 * tq, MLA_KV_RANK),
        compiler_params=_cparams(("arbitrary", "arbitrary", "arbitrary")),
    )(q, kk, latt)


def _new_tile_mask(m, tk, rows_per_token, dt, inclusive):
    row_t = (lax.broadcasted_iota(jnp.int32, (m, tk), 0) // rows_per_token) % dt
    key = lax.broadcasted_iota(jnp.int32, (m, tk), 1)
    return ((key <= row_t) if inclusive else (key < row_t)) & (key < dt)


def _mla_sample_body(pt_ref, q_ref, new_ref, *rest, n_dense, pps, nsteps, dt):
    pages, (o_ref, m_ref, l_ref, acc_ref) = rest[:pps], rest[pps:]
    c = pl.program_id(1)
    q = q_ref[0]
    m, tk = q.shape[0], new_ref.shape[2]

    @pl.when(c == 0)
    def _():
        _softmax_init(m_ref, l_ref, acc_ref)
        nw = new_ref[0].astype(BF16)
        _softmax_tile(_dot(q, nw), _new_tile_mask(m, tk, MLA_HEADS, dt, True), nw[:MLA_KV_RANK], m_ref, l_ref, acc_ref, True)

    pgs = [pages[s][0, 0].astype(BF16) for s in range(pps)]
    _softmax_pages(q, pgs, [pg[:MLA_KV_RANK] for pg in pgs], None, m_ref, l_ref, acc_ref)

    @pl.when(c == nsteps - 1)
    def _():
        o_ref[0] = (acc_ref[...] / l_ref[...]).astype(BF16)


def _mla_sample_attn(q, new_rows, cache, layer, page_table):
    b, dt, _ = q.shape
    feat = MLA_KV_RANK + MLA_ROPE_DIM
    q160 = q.reshape(b, dt * MLA_HEADS, 2 * LANES)[:, :, :feat]
    m = dt * MLA_HEADS
    (o,) = _paged_call(
        functools.partial(_mla_sample_body, dt=dt), page_table, cache, layer, (feat, 0),
        [q160, _as_page(new_rows, cache.shape[3])], [(m, MLA_KV_RANK, BF16)], _softmax_scratch(m, MLA_KV_RANK))
    return o.reshape(b, dt, MLA_HEADS * MLA_KV_RANK)


def _out_mla_body(x_ref, g_ref, o_ref, z_ref, wuv_ref, w_ref, lng_ref, lnb_ref, y_ref):
    _finish(_dot(o_ref[0], wuv_ref[...]), z_ref, x_ref, g_ref, w_ref, lng_ref, lnb_ref, y_ref)


def _mla_out_proj(x, gate, o_lat, z, w_uv, w_out, ln_g, ln_b):
    d = x.shape[-1]
    eye = jnp.eye(MLA_HEADS, dtype=BF16)
    wuv_bd = (w_uv.astype(BF16).transpose(1, 0, 2)[:, :, None, :] * eye[:, None, :, None]).reshape(
        MLA_HEADS * MLA_KV_RANK, MLA_HEADS * MLA_V_DIM)
    (y,) = _row_call(_out_mla_body, x, [gate], [o_lat, z],
                     [wuv_bd, w_out.astype(BF16), ln_g.reshape(1, d), ln_b.reshape(1, d)], [(d, F32)], ROW_TILE)
    return y


def _mla_layer(xp, xs, mod_p, mod_s, pos_p, pos_s, w_in, q_norm, w_q_up, kv_norm, w_uk, w_uv, w_out, ln_g, ln_b,
               cache, layer, page_table):
    db, dt, d = xs.shape
    sh_p, sc_p, gt_p = _split_mod(mod_p)
    sh_s, sc_s, gt_s = _split_mod(mod_s, dt)
    xs_flat = xs.reshape(1, db * dt, d)
    tab_p = _rope_tables(pos_p, MLA_ROPE_DIM, MLA_ROPE_THETA, MLA_ROPE_DIM)
    tab_s = _rope_tables(jnp.tile(pos_s, db), MLA_ROPE_DIM, MLA_ROPE_THETA, MLA_ROPE_DIM)
    cq_p, rows_p, kk_p, z_p, latt_p = _mla_proj(xp, sh_p, sc_p, tab_p, w_in, kv_norm)
    cq_s, rows_s, _, z_s, _ = _mla_proj(xs_flat, sh_s, sc_s, tab_s, w_in, kv_norm)
    q_p = _mla_q(cq_p, tab_p, q_norm, w_q_up, w_uk)
    q_s = _mla_q(cq_s, tab_s, q_norm, w_q_up, w_uk)
    o_p = _mla_prompt_attn(q_p, kk_p, latt_p)
    rows_s = rows_s.reshape(db, dt, -1)
    o_s = _mla_sample_attn(q_s.reshape(db, dt, -1), rows_s, cache, layer, page_table)
    yp = _mla_out_proj(xp, gt_p, o_p, z_p, w_uv, w_out, ln_g, ln_b)
    ys = _mla_out_proj(xs_flat, gt_s, o_s.reshape(1, db * dt, -1), z_s, w_uv, w_out, ln_g, ln_b)
    return yp, ys.reshape(db, dt, d), rows_p, rows_s


def _diff_proj_body(x_ref, sh_ref, sc_ref, c_ref, s1_ref, s2_ref, wq_ref, wk_ref, wv_ref, wz_ref,
                    q_ref, kv_ref, z_ref, vt_ref):
    h = _modulated(x_ref, sh_ref, sc_ref)
    c, s1, s2 = c_ref[0], s1_ref[0], s2_ref[0]
    half = DIFF_HEAD_DIM // 8
    q = _dot(h, wq_ref[...])
    for ch in range(q.shape[1] // LANES):
        qc = _rope128(q[:, ch * LANES:(ch + 1) * LANES], c, s1, s2, half)
        q_ref[0, :, ch * LANES:(ch + 1) * LANES] = (qc * (DIFF_HEAD_DIM ** -0.5)).astype(BF16)
    k = _dot(h, wk_ref[...])
    nk = k.shape[1]
    for ch in range(nk // LANES):
        kv_ref[0, :, ch * LANES:(ch + 1) * LANES] = _rope128(k[:, ch * LANES:(ch + 1) * LANES], c, s1, s2, half)
    v = _dot(h, wv_ref[...])
    kv_ref[0, :, nk:] = v
    _store_transposed(vt_ref, v)
    z_ref[0] = _dot(h, wz_ref[...])


def _diff_proj(x, sh, sc, tables, w_in):
    hd, g, d = DIFF_HEADS, DIFF_KV_HEADS, DIFF_HEAD_DIM
    n1 = hd * 2 * d
    n2 = n1 + g * 2 * d
    n3 = n2 + g * 2 * d
    w = w_in.astype(BF16)
    return _row_call(_diff_proj_body, x, [sh, sc], list(tables), [w[:, :n1], w[:, n1:n2], w[:, n2:n3], w[:, n3:]],
                     [(n1, BF16), (n3 - n1, F32), (w.shape[1] - n3, F32), ("t", (n3 - n2) // LANES)], ROW_TILE)


def _diff_lambda(lam, lam_init):
    a = jnp.sum(lam[0:1] * lam[1:2], axis=-1, keepdims=True)
    b = jnp.sum(lam[2:3] * lam[3:4], axis=-1, keepdims=True)
    return jnp.exp(a) - jnp.exp(b) + lam_init


def _diff_combine(acc, l, lam, sub, lam_init, n):
    o = acc[:n] / l[:n] - lam * (acc[n:] / l[n:])
    return _rms(o, sub) * (1.0 - lam_init)


def _diff_prompt_body(q_ref, k_ref, vt_ref, lam_ref, sub_ref, o_ref, m_ref, l_ref, acc_ref, *, tq, lam_init):
    i = pl.program_id(2)
    lane = lax.broadcasted_iota(jnp.int32, (tq, LANES), 1)
    q = q_ref[0]
    zero = jnp.zeros_like(q)
    q2 = jnp.concatenate([jnp.where(lane < DIFF_HEAD_DIM, q, zero), jnp.where(lane >= DIFF_HEAD_DIM, q, zero)], axis=0)

    def load_k(j):
        return k_ref[0, pl.ds(pl.multiple_of(j * tq, tq), tq), :].astype(BF16)

    def load_v(j):
        return vt_ref[0, 0, j]

    def mask_fn(j, diag):
        return _causal_mask_t(i, j, tq, 2) if diag else None

    _flash_loop(q2, load_k, load_v, mask_fn, 0, i, m_ref, l_ref, acc_ref, transposed=True)
    acc, l = acc_ref[...], l_ref[...]
    o_t = acc[:, :tq] / l[:, :tq] - _diff_lambda(lam_ref[...], lam_init) * (acc[:, tq:] / l[:, tq:])
    o_ref[0] = _rms(o_t.T, sub_ref[...]) * (1.0 - lam_init)


def _diff_prompt_attn(q, kv, vt, lam, subln, lam_init):
    b, t, _ = q.shape
    tq = min(TQ, t)
    assert vt.shape[-1] == tq
    hpg = DIFF_HEADS // DIFF_KV_HEADS
    return pl.pallas_call(
        functools.partial(_diff_prompt_body, tq=tq, lam_init=lam_init),
        out_shape=jax.ShapeDtypeStruct((b, t, DIFF_HEADS * LANES), F32),
        grid=(b, DIFF_HEADS, t // tq),
        in_specs=[pl.BlockSpec((1, tq, LANES), lambda b_, n, i: (b_, i, n)),
                  pl.BlockSpec((1, t, LANES), lambda b_, n, i: (b_, 0, n // hpg)),
                  _vt_spec(vt, lambda n: n // hpg),
                  pl.BlockSpec(lam.shape, lambda b_, n, i: (0, 0)),
                  pl.BlockSpec((1, LANES), lambda b_, n, i: (0, 0))],
        out_specs=pl.BlockSpec((1, tq, LANES), lambda b_, n, i: (b_, i, n)),
        scratch_shapes=_softmax_scratch_t(2 * tq, LANES),
        compiler_params=_cparams(("arbitrary", "arbitrary", "arbitrary")),
    )(q, kv, vt, lam, subln.reshape(1, LANES))


def _diff_sample_body(pt_ref, q_ref, new_ref, lam_ref, sub_ref, *rest, n_dense, pps, nsteps, dt, lam_init):
    pages, (o_ref, m_ref, l_ref, acc_ref) = rest[:pps], rest[pps:]
    c = pl.program_id(1)
    g = DIFF_KV_HEADS
    stride = 2 * g
    nrow = new_ref.shape[1]
    q = q_ref[0]
    m = q.shape[0]
    mg = m // g
    rows = [slice(gi * mg, (gi + 1) * mg) for gi in range(g)]

    def tiles(pgs, pos_ok):
        n = len(pgs)
        pb = [pg.astype(BF16) for pg in pgs]
        s = jnp.concatenate([_dot_nt(q, p) for p in pb], axis=1)
        col = lax.broadcasted_iota(jnp.int32, (m, n * nrow), 1)
        row_g = lax.broadcasted_iota(jnp.int32, (m, n * nrow), 0) // mg
        mask = (col % stride) == row_g
        if pos_ok is not None:
            mask = mask & pos_ok
        s = jnp.where(mask, s, NEG_INF)
        m_prev = m_ref[...]
        m_new = jnp.maximum(m_prev, jnp.max(s, axis=-1, keepdims=True))
        p = jnp.where(mask, jnp.exp(s - m_new), 0.0)
        scale = jnp.exp(m_prev - m_new)
        l_ref[...] = scale * l_ref[...] + jnp.sum(p, axis=-1, keepdims=True)
        m_ref[...] = m_new
        pv = jnp.zeros(acc_ref.shape, F32)
        for k in range(n):
            pk = pltpu.roll(p[:, k * nrow:(k + 1) * nrow], g, 1).astype(BF16)
            pv = pv + _dot(pk, pb[k])
        acc_ref[...] = scale * acc_ref[...] + pv

    @pl.when(c == 0)
    def _():
        _softmax_init(m_ref, l_ref, acc_ref)
        pos = lax.broadcasted_iota(jnp.int32, (m, nrow), 1) // stride
        tok = _row_token(m, nrow, DIFF_HEADS // g, dt)
        tiles([new_ref[0]], (pos <= tok) & (pos < dt))

    tiles([pages[s][0, 0] for s in range(pps)], None)

    @pl.when(c == nsteps - 1)
    def _():
        lam = _diff_lambda(lam_ref[0], lam_init)
        n = mg // 2
        for gi in range(g):
            r = rows[gi]
            o_ref[0, gi * n:(gi + 1) * n, :] = _diff_combine(acc_ref[r, :], l_ref[r, :], lam, sub_ref[0], lam_init, n)


def _diff_sample_attn(q, new_kv, cache, layer, page_table, lam, subln, lam_init):
    b, dt, _ = q.shape
    g, d = DIFF_KV_HEADS, DIFF_HEAD_DIM
    n = DIFF_HEADS // g
    page = cache.shape[2] // (2 * g)
    q6 = q.reshape(b, dt, g, n, 2, d).transpose(0, 2, 4, 1, 3, 5)
    eye = jnp.eye(2, dtype=q.dtype)
    qrows = (q6[..., None, :] * eye[None, None, :, None, None, :, None]).reshape(b, g * 2 * dt * n, 2 * d)
    new_page = jnp.pad(new_kv.reshape(b, dt, 2 * g, LANES), ((0, 0), (0, page - dt), (0, 0), (0, 0)))
    new_page = new_page.reshape(b, page * 2 * g, LANES)
    m = qrows.shape[1]
    (o,) = _paged_call(
        functools.partial(_diff_sample_body, dt=dt, lam_init=lam_init), page_table, cache, layer,
        (cache.shape[2], 0), [qrows, new_page, jnp.broadcast_to(lam, (b,) + lam.shape),
                              jnp.broadcast_to(subln.reshape(1, 1, LANES), (b, 1, LANES))],
        [(m // 2, LANES, F32)], _softmax_scratch(m, LANES))
    return o.reshape(b, g, dt, n, LANES).transpose(0, 2, 1, 3, 4).reshape(b, dt, DIFF_HEADS * LANES)


def _diff_layer(xp, xs, mod_p, mod_s, pos_p, pos_s, layer_idx, w_in, lam, subln, w_out, ln_g, ln_b,
                cache, layer, page_table):
    db, dt, d = xs.shape
    lam_init = 0.8 - 0.6 * math.exp(-0.3 * layer_idx)
    sh_p, sc_p, gt_p = _split_mod(mod_p)
    sh_s, sc_s, gt_s = _split_mod(mod_s, dt)
    xs_flat = xs.reshape(1, db * dt, d)
    tab_p = _rope_tables(pos_p, DIFF_HEAD_DIM // 4, ROPE_THETA, DIFF_HEAD_DIM)
    tab_s = _rope_tables(jnp.tile(pos_s, db), DIFF_HEAD_DIM // 4, ROPE_THETA, DIFF_HEAD_DIM)
    q_p, kv_p, z_p, vt_p = _diff_proj(xp, sh_p, sc_p, tab_p, w_in)
    q_s, kv_s, z_s, _ = _diff_proj(xs_flat, sh_s, sc_s, tab_s, w_in)
    o_p = _diff_prompt_attn(q_p, kv_p, vt_p, lam, subln, lam_init)
    kv_s = kv_s.reshape(db, dt, -1)
    nl, pool, page = cache.shape[:3]
    cache_rows = cache.reshape(nl, pool, page * 2 * DIFF_KV_HEADS, LANES)
    o_s = _diff_sample_attn(q_s.reshape(db, dt, -1), kv_s, cache_rows, layer, page_table, lam, subln, lam_init)
    yp = _out_proj(xp, gt_p, o_p, z_p, w_out, ln_g, ln_b)
    ys = _out_proj(xs_flat, gt_s, o_s.reshape(1, db * dt, -1), z_s, w_out, ln_g, ln_b)
    rows_shape = (2, DIFF_KV_HEADS, 2 * DIFF_HEAD_DIM)
    return yp, ys.reshape(db, dt, d), kv_p.reshape(kv_p.shape[:2] + rows_shape), kv_s.reshape((db, dt) + rows_shape)


NSA_HPG = NSA_HEADS // NSA_KV_GROUPS
GL_PAD = LANES


def _nsa_proj_body(x_ref, sh_ref, sc_ref, c_ref, s1_ref, s2_ref, wq_ref, wkv_ref, ww_ref, wg_ref, wz_ref,
                   q_ref, kv_ref, w_ref, gl_ref, z_ref, vst_ref, vwt_ref):
    h = _modulated(x_ref, sh_ref, sc_ref)
    c, s1, s2 = c_ref[0], s1_ref[0], s2_ref[0]
    half = NSA_HEAD_DIM // 8
    q = _dot(h, wq_ref[...])
    for ch in range(q.shape[1] // LANES):
        qc = _rope128(q[:, ch * LANES:(ch + 1) * LANES], c, s1, s2, half)
        q_ref[0, :, ch * LANES:(ch + 1) * LANES] = (qc * (NSA_HEAD_DIM ** -0.5)).astype(BF16)
    kv = _dot(h, wkv_ref[...])
    kv_ref[0, :, :2 * LANES] = kv[:, :2 * LANES]
    kv_ref[0, :, 2 * LANES:3 * LANES] = _rope128(kv[:, 2 * LANES:3 * LANES], c, s1, s2, half)
    kv_ref[0, :, 3 * LANES:] = kv[:, 3 * LANES:]
    _store_transposed(vst_ref, kv[:, 3 * LANES:])
    w = _dot(h, ww_ref[...])
    w_ref[0, :, :LANES] = _rope128(w[:, :LANES], c, s1, s2, half)
    w_ref[0, :, LANES:] = w[:, LANES:]
    _store_transposed(vwt_ref, w[:, LANES:])
    gl_ref[0] = _dot(h, wg_ref[...])
    z_ref[0] = _dot(h, wz_ref[...])


def _nsa_proj(x, sh, sc, tables, w_in):
    hd, g, d = NSA_HEADS, NSA_KV_GROUPS, NSA_HEAD_DIM
    n1 = hd * d
    n2 = n1 + 4 * g * d
    n3 = n2 + 2 * g * d
    n4 = n3 + 3 * hd
    w = w_in.astype(BF16)
    wg = jnp.pad(w[:, n3:n4], ((0, 0), (0, GL_PAD - 3 * hd)))
    return _row_call(_nsa_proj_body, x, [sh, sc], list(tables), [w[:, :n1], w[:, n1:n2], w[:, n2:n3], wg, w[:, n4:]],
                     [(n1, BF16), (n2 - n1, F32), (n3 - n2, F32), (GL_PAD, F32), (w.shape[1] - n4, F32),
                      ("t", 1), ("t", 1)], ROW_TILE)


def _cmp_weights(pe, w1, b1, w2, b2):
    g = NSA_KV_GROUPS
    eye = jnp.eye(g, dtype=BF16)
    w1b = w1.astype(BF16)
    bd = (w1b[:, :, None, :, None, :] * eye[None, None, :, None, :, None]).reshape(
        2, CMP_BLOCK, g * NSA_HEAD_DIM, g * CMP_HIDDEN)
    wcat = jnp.concatenate([bd[:, :CMP_STRIDE], bd[:, CMP_STRIDE:]], axis=3)
    wcat = wcat.reshape(2, CMP_STRIDE // 2, 2 * g * NSA_HEAD_DIM, 2 * g * CMP_HIDDEN)
    w1f = w1b.reshape(2, CMP_BLOCK * NSA_HEAD_DIM, CMP_HIDDEN)
    pef = jnp.pad(pe.astype(BF16).reshape(2, 1, CMP_BLOCK * NSA_HEAD_DIM), ((0, 0), (0, 7), (0, 0)))
    w2b = w2.astype(BF16)
    w2bd = (w2b[:, None, :, None, :] * eye[None, :, None, :, None]).reshape(2, g * CMP_HIDDEN, g * NSA_HEAD_DIM)
    b2t = jnp.tile(b2, (1, g)).reshape(2, 1, g * NSA_HEAD_DIM)
    return [wcat, w1f, pef, b1.reshape(2, 1, CMP_HIDDEN), w2bd, b2t]


def _compress(get_rows, nblk, wcat_ref, w1f_ref, pe_ref, b1_ref, w2_ref, b2_ref):
    outs = []
    nh = NSA_KV_GROUPS * CMP_HIDDEN
    for comp in range(2):
        acc = jnp.zeros((nblk, 2 * nh), F32)
        for pp in range(CMP_STRIDE // 2):
            x = jnp.concatenate([get_rows(comp, 2 * pp), get_rows(comp, 2 * pp + 1)], axis=1).astype(BF16)
            acc = acc + _dot(x, wcat_ref[comp, pp])
        bias = _dot(pe_ref[comp], w1f_ref[comp])[0:1] + b1_ref[comp]
        bias = jnp.concatenate([bias] * NSA_KV_GROUPS, axis=1)
        hid = _silu(acc[:, :nh] + pltpu.roll(acc[:, nh:], nblk - 1, 0) + bias).astype(BF16)
        outs.append(_dot(hid, w2_ref[comp]) + b2_ref[comp])
    return outs


def _nsa_compress_body(kc_ref, vc_ref, wcat_ref, w1f_ref, pe_ref, b1_ref, w2_ref, b2_ref, k_ref, v_ref, *, nblk):
    def get_rows(comp, p):
        return (kc_ref, vc_ref)[comp][0, pl.ds(p, nblk, stride=CMP_STRIDE), :]
    k, v = _compress(get_rows, nblk, wcat_ref, w1f_ref, pe_ref, b1_ref, w2_ref, b2_ref)
    k_ref[0] = k
    v_ref[0] = v


def _nsa_compress(kv, cw):
    b, t, f = kv.shape
    nblk = t // CMP_STRIDE
    full = lambda a: pl.BlockSpec(a.shape, lambda b_: (0,) * a.ndim)
    return pl.pallas_call(
        functools.partial(_nsa_compress_body, nblk=nblk),
        out_shape=[jax.ShapeDtypeStruct((b, nblk, LANES), F32)] * 2,
        grid=(b,),
        in_specs=[pl.BlockSpec((1, t, LANES), lambda b_: (b_, 0, 0)),
                  pl.BlockSpec((1, t, LANES), lambda b_: (b_, 0, 1))] + [full(a) for a in cw],
        out_specs=[pl.BlockSpec((1, nblk, LANES), lambda b_: (b_, 0, 0))] * 2,
        compiler_params=_cparams(("arbitrary",)),
    )(kv, kv, *cw)


def _place_in_half(x, src_half, dst_half):
    return jnp.where(dst_half == src_half, x, pltpu.roll(x, LANES // 2, 1))


def _pair_queries(q, dst_half):
    q = q.astype(F32)
    lane = lax.broadcasted_iota(jnp.int32, q.shape, 1)
    keep = (lane // (LANES // 2)) == dst_half
    qa = jnp.where(keep, _place_in_half(q, 0, dst_half), 0.0)
    qb = jnp.where(keep, _place_in_half(q, 1, dst_half), 0.0)
    return jnp.concatenate([qa, qb], axis=0).astype(BF16)


def _pair_outputs(o2, src_half):
    tq = o2.shape[0] // 2
    lane = lax.broadcasted_iota(jnp.int32, (tq, LANES), 1)
    return jnp.where(lane < LANES // 2, _place_in_half(o2[:tq], src_half, 0), _place_in_half(o2[tq:], src_half, 1))


def _select_blocks(imp, cur, n_sel, k_eff):
    sp, n = imp.shape
    blk = lax.broadcasted_iota(jnp.int32, (sp, n), 0)
    blkf = blk.astype(F32)
    forced = (blk == 0) | (blk == cur) | (blk == cur - 1)
    allowed = (blk <= cur) & (blk < n_sel)
    score = jnp.where(blk > cur, NEG_INF, jnp.where(forced, FORCE_SCORE, imp))
    lowest = -3.0e38
    score = jnp.where(blk < n_sel, score, lowest)
    sel = jnp.zeros((sp, n), F32)
    for _ in range(k_eff):
        mx = jnp.max(score, axis=0, keepdims=True)
        first = jnp.min(jnp.where(score == mx, blkf, float(sp)), axis=0, keepdims=True)
        hit = blkf == first
        sel = jnp.where(hit, 1.0, sel)
        score = jnp.where(hit, lowest, score)
    return jnp.where(allowed, sel, 0.0)


def _overlap_t(n_sel_pad, n_cmp_pad, n_sel, n_cmp):
    ci = np.arange(n_cmp_pad)[None, :] * CMP_STRIDE
    sj = np.arange(n_sel_pad)[:, None] * SEL_BLOCK
    ov = (ci < sj + SEL_BLOCK) & (ci + CMP_BLOCK > sj)
    ov &= (np.arange(n_cmp_pad)[None, :] < n_cmp) & (np.arange(n_sel_pad)[:, None] < n_sel)
    return jnp.asarray(ov, BF16)


def _cmp_softmax(s, mask):
    s = jnp.where(mask, s, NEG_INF)
    p = jnp.where(mask, jnp.exp(s - jnp.max(s, axis=-1, keepdims=True)), 0.0)
    l = jnp.sum(p, axis=-1, keepdims=True)
    return p / jnp.where(l > 0.0, l, 1.0)


def _nsa_cmpsel_body(q_ref, k_ref, v_ref, ov_ref, o_ref, sel_ref, *, tq, n_sel, k_eff):
    g = pl.program_id(1)
    i = pl.program_id(2)
    nblk = k_ref.shape[1]
    nch = NSA_HPG // 2
    q2s = [_pair_queries(q_ref[0, :, ch * LANES:(ch + 1) * LANES], g) for ch in range(nch)]
    qall = jnp.concatenate(q2s, axis=0)
    s = _dot_nt(qall, k_ref[0].astype(BF16))
    m = qall.shape[0]
    qpos = i * tq + lax.broadcasted_iota(jnp.int32, (m, nblk), 0) % tq
    cmp_end = lax.broadcasted_iota(jnp.int32, (m, nblk), 1) * CMP_STRIDE + (CMP_BLOCK - 1)
    p = _cmp_softmax(s, cmp_end <= qpos)
    o = _dot(p.astype(BF16), v_ref[0].astype(BF16))
    for ch in range(nch):
        o_ref[0, :, ch * LANES:(ch + 1) * LANES] = _pair_outputs(o[2 * ch * tq:(2 * ch + 2) * tq], g)
    psum = p[:tq]
    for n in range(1, NSA_HPG):
        psum = psum + p[n * tq:(n + 1) * tq]
    hi = psum.astype(BF16)
    lo = (psum - hi.astype(F32)).astype(BF16)
    imp_t = _dot_nt(ov_ref[...], hi) + _dot_nt(ov_ref[...], lo)
    sp = imp_t.shape[0]
    cur = (i * tq + lax.broadcasted_iota(jnp.int32, (1, tq), 1)) // SEL_BLOCK
    sel_t = _select_blocks(imp_t, cur, n_sel, k_eff)
    if sp < LANES:
        sel_t = jnp.concatenate([sel_t, jnp.zeros((LANES - sp, tq), F32)], axis=0)
    sel_ref[0, 0] = sel_t.astype(BF16)


def _nsa_cmpsel(q, kcmp, vcmp):
    b, t, f = q.shape
    tq = min(TQ, t)
    nblk = kcmp.shape[1]
    n_sel = -(-t // SEL_BLOCK)
    assert n_sel <= LANES
    sp = -(-n_sel // 8) * 8
    ov = _overlap_t(sp, nblk, n_sel, nblk - 1)
    gw = f // NSA_KV_GROUPS
    return pl.pallas_call(
        functools.partial(_nsa_cmpsel_body, tq=tq, n_sel=n_sel, k_eff=min(N_SELECT, n_sel)),
        out_shape=[jax.ShapeDtypeStruct((b, t, f), F32), jax.ShapeDtypeStruct((b, NSA_KV_GROUPS, LANES, t), BF16)],
        grid=(b, NSA_KV_GROUPS, t // tq),
        in_specs=[pl.BlockSpec((1, tq, gw), lambda b_, g, i: (b_, i, g)),
                  pl.BlockSpec((1, nblk, LANES), lambda b_, g, i: (b_, 0, 0)),
                  pl.BlockSpec((1, nblk, LANES), lambda b_, g, i: (b_, 0, 0)),
                  pl.BlockSpec(ov.shape, lambda b_, g, i: (0, 0))],
        out_specs=[pl.BlockSpec((1, tq, gw), lambda b_, g, i: (b_, i, g)),
                   pl.BlockSpec((1, 1, LANES, tq), lambda b_, g, i: (b_, g, 0, i))],
        compiler_params=_cparams(("arbitrary", "arbitrary", "arbitrary")),
    )(q, kcmp, vcmp, ov)


def _block_expand(base_pos, n_blocks_pad, tk):
    blk = lax.broadcasted_iota(jnp.int32, (n_blocks_pad, tk), 0)
    pos = base_pos + lax.broadcasted_iota(jnp.int32, (n_blocks_pad, tk), 1)
    return (blk == pos // SEL_BLOCK).astype(BF16)


def _nsa_prompt_body(*refs, tq, mode):
    if mode == "sel":
        q_ref, k_ref, vt_ref, sel_ref, o_ref, m_ref, l_ref, acc_ref = refs
    else:
        q_ref, k_ref, vt_ref, o_ref, m_ref, l_ref, acc_ref = refs
    c = pl.program_id(1)
    i = pl.program_id(2)
    g = c // (NSA_HPG // 2)
    q2 = _pair_queries(q_ref[0], g)
    koff = lax.broadcasted_iota(jnp.int32, (tq, tq), 0)
    qpos = i * tq + lax.broadcasted_iota(jnp.int32, (tq, tq), 1)
    if mode == "sel":
        sel_t = sel_ref[0, 0]

    def load_k(j):
        return k_ref[0, pl.ds(pl.multiple_of(j * tq, tq), tq), :].astype(BF16)

    def load_v(j):
        return vt_ref[0, 0, j]

    def mask_fn(j, diag):
        start = j * tq
        if mode == "sel":
            blk = lax.broadcasted_iota(jnp.int32, (tq, LANES), 1)
            expand_t = (blk == (start + lax.broadcasted_iota(jnp.int32, (tq, LANES), 0)) // SEL_BLOCK).astype(BF16)
            mask = _dot(expand_t, sel_t) > 0.5
            if diag:
                mask = mask & (start + koff <= qpos)
        else:
            mask = (start + koff <= qpos) & (qpos - (start + koff) < WINDOW)
        return jnp.concatenate([mask, mask], axis=1)

    j_lo = 0 if mode == "sel" else jnp.maximum(i - (WINDOW + tq - 1) // tq, 0)
    _flash_loop(q2, load_k, load_v, mask_fn, j_lo, i, m_ref, l_ref, acc_ref, transposed=True)
    o_t = acc_ref[...] / l_ref[...]
    o_ref[0] = _pair_outputs(jnp.concatenate([o_t[:, :tq].T, o_t[:, tq:].T], axis=0), g)


def _nsa_prompt_attn(q, kv, k_chunk, vt, sel=None):
    b, t, f = q.shape
    tq = min(TQ, t)
    assert vt.shape[-1] == tq
    cpg = NSA_HPG // 2
    in_specs = [pl.BlockSpec((1, tq, LANES), lambda b_, c, i: (b_, i, c)),
                pl.BlockSpec((1, t, LANES), lambda b_, c, i: (b_, 0, k_chunk)),
                _vt_spec(vt, lambda c: 0)]
    args = [q, kv, vt]
    if sel is not None:
        in_specs.append(pl.BlockSpec((1, 1, LANES, tq), lambda b_, c, i: (b_, c // cpg, 0, i)))
        args.append(sel)
    return pl.pallas_call(
        functools.partial(_nsa_prompt_body, tq=tq, mode="sel" if sel is not None else "win"),
        out_shape=jax.ShapeDtypeStruct((b, t, f), F32),
        grid=(b, f // LANES, t // tq),
        in_specs=in_specs,
        out_specs=pl.BlockSpec((1, tq, LANES), lambda b_, c, i: (b_, i, c)),
        scratch_shapes=_softmax_scratch_t(2 * tq, LANES),
        compiler_params=_cparams(("arbitrary", "arbitrary", "arbitrary")),
    )(*args)


def _out_nsa_body(x_ref, g_ref, oc_ref, os_ref, ow_ref, gl_ref, z_ref, e_ref, w_ref, lng_ref, lnb_ref, y_ref):
    sig = _sigmoid(gl_ref[0])
    o = (_split_dot(sig, e_ref[0]) * oc_ref[0] + _split_dot(sig, e_ref[1]) * os_ref[0]
         + _split_dot(sig, e_ref[2]) * ow_ref[0])
    _finish(o, z_ref, x_ref, g_ref, w_ref, lng_ref, lnb_ref, y_ref)


def _nsa_out_proj(x, gate, o_c, o_s, o_w, gl, z, w_out, ln_g, ln_b):
    d = x.shape[-1]
    e = np.zeros((3, GL_PAD, NSA_HEADS * NSA_HEAD_DIM), np.float32)
    for br in range(3):
        for h in range(NSA_HEADS):
            e[br, 3 * h + br, h * NSA_HEAD_DIM:(h + 1) * NSA_HEAD_DIM] = 1.0
    (y,) = _row_call(_out_nsa_body, x, [gate], [o_c, o_s, o_w, gl, z],
                     [jnp.asarray(e, BF16), w_out.astype(BF16), ln_g.reshape(1, d), ln_b.reshape(1, d)],
                     [(d, F32)], ROW_TILE)
    return y


def _row_token(m, n, rows_per_token, dt):
    return (lax.broadcasted_iota(jnp.int32, (m, n), 0) // rows_per_token) % dt


def _nsa_sample_cmp_body(pt_ref, q_ref, new_ref, ov_ref, wcat_ref, w1f_ref, pe_ref, b1_ref, w2_ref, b2_ref, *rest,
                         n_dense, pps, nsteps, dt, past, nblk):
    pages, (oc_ref, imp_ref, xk_ref, xv_ref) = rest[:pps], rest[pps:]
    c = pl.program_id(1)
    page = pages[0].shape[3]

    @pl.when(c == 0)
    def _():
        tail = xk_ref.shape[0] - past
        nw = new_ref[0]
        for x_ref, lo in ((xk_ref, 0), (xv_ref, LANES)):
            x_ref[pl.ds(past, tail), :] = jnp.zeros((tail, LANES), F32)
            x_ref[pl.ds(past, nw.shape[0]), :] = nw[:, lo:lo + LANES]

    for s in range(pps):
        pg = pages[s][0, 0]
        base = pl.multiple_of((c * pps + s) * page, page)
        xk_ref[pl.ds(base, page), :] = pg[:LANES].T
        xv_ref[pl.ds(base, page), :] = pg[LANES:].T

    @pl.when(c == nsteps - 1)
    def _():
        def get_rows(comp, p):
            return (xk_ref, xv_ref)[comp][pl.ds(p, nblk, stride=CMP_STRIDE), :]
        k, v = _compress(get_rows, nblk, wcat_ref, w1f_ref, pe_ref, b1_ref, w2_ref, b2_ref)
        q = q_ref[0]
        m = q.shape[0]
        s = _dot_nt(q, k.astype(BF16))
        qpos = past + _row_token(m, nblk, NSA_HPG, dt)
        cmp_end = lax.broadcasted_iota(jnp.int32, (m, nblk), 1) * CMP_STRIDE + (CMP_BLOCK - 1)
        p = _cmp_softmax(s, cmp_end <= qpos)
        oc_ref[0] = _dot(p.astype(BF16), v.astype(BF16))
        psum = jnp.sum(p.reshape(m // NSA_HPG, NSA_HPG, nblk), axis=1)
        imp_ref[0] = _split_dot(psum, ov_ref[...])


def _select_body(imp_ref, cur_ref, sel_ref, *, n_sel, k_eff):
    sel_ref[...] = _select_blocks(imp_ref[...], cur_ref[...], n_sel, k_eff)


def _nsa_sample_select(imp, past, dt, n_sel):
    b, r, spl = imp.shape
    sp = -(-n_sel // 8) * 8
    n = b * r
    npad = -(-n // LANES) * LANES
    imp_t = jnp.pad(imp.reshape(n, spl)[:, :sp].T, ((0, 0), (0, npad - n)))
    cur = jnp.tile((past + jnp.arange(dt, dtype=jnp.int32)) // SEL_BLOCK, n // dt)
    cur = jnp.pad(cur, (0, npad - n)).reshape(1, npad)
    sel_t = pl.pallas_call(
        functools.partial(_select_body, n_sel=n_sel, k_eff=min(N_SELECT, n_sel)),
        out_shape=jax.ShapeDtypeStruct((sp, npad), F32),
        compiler_params=pltpu.CompilerParams(vmem_limit_bytes=VMEM_LIMIT),
    )(imp_t, cur)
    return jnp.pad(sel_t[:, :n].T, ((0, 0), (0, spl - sp))).reshape(b, r, spl)


def _nsa_sample_sel_body(pt_ref, q_ref, sel_ref, new_ref, st_ref, neww_ref, *rest, n_dense, pps, nsteps, dt, past):
    pages, (os_ref, ow_ref, m_ref, l_ref, acc_ref, m2_ref, l2_ref, acc2_ref) = rest[:pps], rest[pps:]
    c = pl.program_id(1)
    q = q_ref[0]
    selb = sel_ref[0]
    m, spl = selb.shape
    page = pages[0].shape[3]

    def sel_tile(kv_t, base_pos, extra):
        tk = kv_t.shape[1]
        mask = _dot(selb, _block_expand(base_pos, spl, tk)) > 0.5
        if extra is not None:
            mask = mask & extra
        _softmax_tile(_dot(q, kv_t[:LANES].astype(BF16)), mask, kv_t[LANES:].astype(BF16), m_ref, l_ref, acc_ref, True)

    @pl.when(c == 0)
    def _():
        _softmax_init(m_ref, l_ref, acc_ref)
        _softmax_init(m2_ref, l2_ref, acc2_ref)
        tk = new_ref.shape[2]
        new_mask = _new_tile_mask(m, tk, NSA_HPG, dt, True)
        sel_tile(new_ref[0], past, new_mask)
        st = st_ref[0]
        wlen = st.shape[1]
        diff = (past + _row_token(m, wlen, NSA_HPG, dt)) - (past - wlen + lax.broadcasted_iota(jnp.int32, (m, wlen), 1))
        kpos_ok = (past - wlen + lax.broadcasted_iota(jnp.int32, (m, wlen), 1)) >= 0
        _softmax_tile(_dot(q, st[:LANES].astype(BF16)), (diff >= 0) & (diff < WINDOW) & kpos_ok,
                      st[LANES:].astype(BF16), m2_ref, l2_ref, acc2_ref, True)
        nw = neww_ref[0]
        diff_n = _row_token(m, tk, NSA_HPG, dt) - lax.broadcasted_iota(jnp.int32, (m, tk), 1)
        _softmax_tile(_dot(q, nw[:LANES].astype(BF16)), new_mask & (diff_n < WINDOW),
                      nw[LANES:].astype(BF16), m2_ref, l2_ref, acc2_ref, True)

    pgs = [pages[s][0, 0] for s in range(pps)]
    mask = _dot(selb, _block_expand(c * (pps * page), spl, pps * page)) > 0.5
    _softmax_pages(q, [pg[:LANES].astype(BF16) for pg in pgs], [pg[LANES:].astype(BF16) for pg in pgs], mask,
                   m_ref, l_ref, acc_ref)

    @pl.when(c == nsteps - 1)
    def _():
        os_ref[0] = acc_ref[...] / l_ref[...]
        ow_ref[0] = acc2_ref[...] / l2_ref[...]


def _nsa_sample_attn(q, new_kv, new_w, cache, win_state, layer, page_table, cw):
    b, dt, _ = q.shape
    g = NSA_KV_GROUPS
    npg, page = page_table.shape[1], cache.shape[3]
    past = npg * page
    tk_all = past + dt
    n_chunks = -(-tk_all // CMP_STRIDE)
    nblk = -(-n_chunks // LANES) * LANES
    n_sel = -(-tk_all // SEL_BLOCK)
    spl = -(-n_sel // LANES) * LANES
    qg = _group_block_diag(q, g)
    m = qg.shape[1]
    ov = _overlap_t(spl, nblk, n_sel, n_chunks - 1).T
    new_cmp = jnp.pad(new_kv[:, :, :2 * LANES], ((0, 0), (0, CMP_STRIDE - dt), (0, 0)))
    o_c, imp = _paged_call(
        functools.partial(_nsa_sample_cmp_body, dt=dt, past=past, nblk=nblk), page_table, cache, layer, (2 * LANES, 0),
        [qg, new_cmp], [(m, LANES, F32), (m // NSA_HPG, spl, F32)],
        [pltpu.VMEM((nblk * CMP_STRIDE, LANES), F32)] * 2, shared=[ov] + list(cw))
    sel = _nsa_sample_select(imp, past, dt, n_sel)
    sel_rows = jnp.repeat(sel, NSA_HPG, axis=1).astype(BF16)
    o_s, o_w = _paged_call(
        functools.partial(_nsa_sample_sel_body, dt=dt, past=past), page_table, cache, layer, (2 * LANES, 1),
        [qg, sel_rows, _as_page(new_kv[:, :, 2 * LANES:], page), win_state, _as_page(new_w, page)],
        [(m, LANES, F32), (m, LANES, F32)], _softmax_scratch(m, LANES) + _softmax_scratch(m, LANES))
    return tuple(_group_diag_extract(o, g, dt) for o in (o_c, o_s, o_w))


def _nsa_layer(xp, xs, mod_p, mod_s, pos_p, pos_s, w_in, pe, w1, b1, w2, b2, w_out, ln_g, ln_b,
               cache, win_state, layer, page_table):
    db, dt, d = xs.shape
    g, hd = NSA_KV_GROUPS, NSA_HEAD_DIM
    sh_p, sc_p, gt_p = _split_mod(mod_p)
    sh_s, sc_s, gt_s = _split_mod(mod_s, dt)
    xs_flat = xs.reshape(1, db * dt, d)
    tab_p = _rope_tables(pos_p, hd // 4, ROPE_THETA, hd)
    tab_s = _rope_tables(jnp.tile(pos_s, db), hd // 4, ROPE_THETA, hd)
    cw = _cmp_weights(pe, w1, b1, w2, b2)
    q_p, kv_p, w_p, gl_p, z_p, vst_p, vwt_p = _nsa_proj(xp, sh_p, sc_p, tab_p, w_in)
    q_s, kv_s, w_s, gl_s, z_s, _, _ = _nsa_proj(xs_flat, sh_s, sc_s, tab_s, w_in)
    kcmp, vcmp = _nsa_compress(kv_p, cw)
    oc_p, sel = _nsa_cmpsel(q_p, kcmp, vcmp)
    os_p = _nsa_prompt_attn(q_p, kv_p, 2, vst_p, sel)
    ow_p = _nsa_prompt_attn(q_p, w_p, 0, vwt_p)
    yp = _nsa_out_proj(xp, gt_p, oc_p, os_p, ow_p, gl_p, z_p, w_out, ln_g, ln_b)
    kv_s, w_s = kv_s.reshape(db, dt, -1), w_s.reshape(db, dt, -1)
    state = _page_view(win_state)[layer]
    oc_s, os_s, ow_s = _nsa_sample_attn(q_s.reshape(db, dt, -1), kv_s, w_s, cache, state, layer, page_table, cw)
    flat = lambda a: a.reshape(1, db * dt, -1)
    ys = _nsa_out_proj(xs_flat, gt_s, flat(oc_s), flat(os_s), flat(ow_s), gl_s, z_s, w_out, ln_g, ln_b)
    t = xp.shape[1]
    rows_p = kv_p.reshape(kv_p.shape[:2] + (4, g, hd))
    rows_s = kv_s.reshape(db, dt, 4, g, hd)
    win_p = w_p[:, t - min(WINDOW, t):].reshape(xp.shape[0], min(WINDOW, t), 2, g, hd)
    wrows = jnp.concatenate([win_state[layer], w_s.reshape(db, dt, 2, g, hd)], axis=1)
    win_s = wrows[:, wrows.shape[1] - min(WINDOW, wrows.shape[1]):]
    return yp, ys.reshape(db, dt, d), rows_p, rows_s, win_p, win_s


def kernel(x_prompt, x_sample, cache_sb_kv, cache_mla_latent, cache_diff_kv, cache_nsa_kv, state_nsa_window,
           page_table, c_prompt, c_sample, ada_w, ada_b, ln_g, ln_b, sb_w_in, sb_w_out,
           mla_w_in, mla_q_norm, mla_w_q_up, mla_kv_norm, mla_w_uk, mla_w_uv, mla_w_out,
           diff_w_in, diff_lam, diff_subln, diff_w_out,
           nsa_w_in, nsa_cmp_pe, nsa_cmp_w1, nsa_cmp_b1, nsa_cmp_w2, nsa_cmp_b2, nsa_w_out):
    nb = x_prompt.shape[0]
    mods = _adaln(jnp.concatenate([c_prompt, c_sample], axis=0), ada_w, ada_b)
    past = page_table.shape[1] * cache_sb_kv.shape[2]
    pos_p = jnp.arange(x_prompt.shape[1], dtype=jnp.int32)
    pos_s = past + jnp.arange(x_sample.shape[1], dtype=jnp.int32)
    xp, xs, sb_p, sb_s = _sb_layer(x_prompt, x_sample, mods[0, :nb], mods[0, nb:], sb_w_in[0], sb_w_out[0],
                                   ln_g[0], ln_b[0], _page_view(cache_sb_kv), 0, page_table)
    xp, xs, mla_p, mla_s = _mla_layer(xp, xs, mods[1, :nb], mods[1, nb:], pos_p, pos_s, mla_w_in[0], mla_q_norm[0],
                                      mla_w_q_up[0], mla_kv_norm[0], mla_w_uk[0], mla_w_uv[0], mla_w_out[0],
                                      ln_g[1], ln_b[1], _page_view(cache_mla_latent), 0, page_table)
    xp, xs, diff_p, diff_s = _diff_layer(xp, xs, mods[2, :nb], mods[2, nb:], pos_p, pos_s, 2, diff_w_in[0],
                                         diff_lam[0], diff_subln[0], diff_w_out[0], ln_g[2], ln_b[2],
                                         cache_diff_kv, 0, page_table)
    xp, xs, nsa_p, nsa_s, win_p, win_s = _nsa_layer(
        xp, xs, mods[3, :nb], mods[3, nb:], pos_p, pos_s, nsa_w_in[0], nsa_cmp_pe[0], nsa_cmp_w1[0], nsa_cmp_b1[0],
        nsa_cmp_w2[0], nsa_cmp_b2[0], nsa_w_out[0], ln_g[3], ln_b[3], _page_view(cache_nsa_kv), state_nsa_window,
        0, page_table)
    return (xp, xs, sb_p[None], sb_s[None], mla_p[None], mla_s[None], diff_p[None], diff_s[None],
            nsa_p[None], nsa_s[None], win_p[None], win_s[None])
```

```python
import functools
import math

import numpy as np
import jax
import jax.numpy as jnp
from jax import lax
from jax.experimental import pallas as pl
from jax.experimental.pallas import tpu as pltpu

F32 = jnp.float32
BF16 = jnp.bfloat16

DEPTH = 4
ALPHA = (2 * DEPTH) ** 0.25
NORM_EPS = 1e-5
NEG_INF = -1e30
FORCE_SCORE = 1e9
ROPE_THETA = 500000.0

SB_HEADS, SB_KV_HEADS, SB_HEAD_DIM = 16, 4, 64
MLA_HEADS, MLA_Q_RANK, MLA_KV_RANK = 16, 256, 128
MLA_NOPE_DIM, MLA_ROPE_DIM, MLA_V_DIM = 64, 32, 64
MLA_ROPE_THETA = 10000.0
DIFF_HEADS, DIFF_KV_HEADS, DIFF_HEAD_DIM = 8, 2, 64
NSA_HEADS, NSA_KV_GROUPS, NSA_HEAD_DIM = 16, 2, 64
CMP_BLOCK, CMP_STRIDE, CMP_HIDDEN = 32, 16, 128
SEL_BLOCK, N_SELECT, WINDOW = 64, 16, 512

LANES = 128
VMEM_LIMIT = 48 * 1024 * 1024
PAGES_PER_STEP = 32
TQ = 256
ROW_TILE = 256


def _cparams(sem):
    return pltpu.CompilerParams(dimension_semantics=sem, vmem_limit_bytes=VMEM_LIMIT)


def _dot(a, b):
    return jnp.dot(a, b, preferred_element_type=F32)


def _dot_nt(a, b):
    return lax.dot_general(a, b, (((1,), (1,)), ((), ())), preferred_element_type=F32)


def _silu(x):
    return x / (1.0 + jnp.exp(-x))


def _sigmoid(x):
    return 1.0 / (1.0 + jnp.exp(-x))


def _split_dot(x, w_bf16):
    hi = x.astype(BF16)
    lo = (x - hi.astype(F32)).astype(BF16)
    return _dot(hi, w_bf16) + _dot(lo, w_bf16)


def _rope128(x, c, s1, s2, half):
    return x * c + pltpu.roll(x, LANES - half, 1) * s1 + pltpu.roll(x, half, 1) * s2


def _rope_tables(pos, rot_dim, theta, period):
    half = rot_dim // 2
    inv = theta ** (-jnp.arange(half, dtype=F32) / half)
    ang = pos.astype(F32)[:, None] * inv
    cos, sin = jnp.cos(ang), jnp.sin(ang)
    n = pos.shape[0]
    ones = jnp.ones((n, period - rot_dim), F32)
    zeros_h = jnp.zeros((n, half), F32)
    zeros_r = jnp.zeros((n, period - rot_dim), F32)
    c = jnp.concatenate([cos, cos, ones], axis=1)
    s1 = jnp.concatenate([-sin, zeros_h, zeros_r], axis=1)
    s2 = jnp.concatenate([zeros_h, sin, zeros_r], axis=1)
    rep = LANES // period
    return tuple(jnp.tile(t, (1, rep))[None] for t in (c, s1, s2))


def _adaln_body(c_ref, w_ref, b_ref, o_ref):
    s = _silu(c_ref[...]).astype(BF16)
    o_ref[0] = _dot(s, w_ref[0].astype(BF16)) + b_ref[0]


def _adaln(c_all, ada_w, ada_b):
    depth, d, d3 = ada_w.shape
    n = c_all.shape[0]
    nt = d3 // d
    return pl.pallas_call(
        _adaln_body,
        out_shape=jax.ShapeDtypeStruct((depth, n, d3), F32),
        grid=(depth, nt),
        in_specs=[pl.BlockSpec((n, d), lambda l, j: (0, 0)),
                  pl.BlockSpec((1, d, d), lambda l, j: (l, 0, j)),
                  pl.BlockSpec((1, 1, d), lambda l, j: (l, 0, j))],
        out_specs=pl.BlockSpec((1, n, d), lambda l, j: (l, 0, j)),
        compiler_params=_cparams(("arbitrary", "arbitrary")),
    )(c_all, ada_w, ada_b.reshape(depth, 1, d3))


def _row_call(body, x, mods, row_inputs, full_inputs, out_defs, tm):
    bt, r, _ = x.shape
    tm = min(tm, r)
    grid = (bt, r // tm)

    def tiled(a):
        lead = a.shape[0]
        if a.shape[1] == 1:
            return pl.BlockSpec((1, 1, a.shape[2]), lambda b, i: (b, 0, 0))
        if lead == 1 and bt > 1:
            return pl.BlockSpec((1, tm, a.shape[2]), lambda b, i: (0, i, 0))
        return pl.BlockSpec((1, tm, a.shape[2]), lambda b, i: (b, i, 0))

    def full(a):
        nd = a.ndim
        return pl.BlockSpec(a.shape, lambda b, i: (0,) * nd)

    in_specs = [tiled(x)] + [tiled(m) for m in mods] + [tiled(t) for t in row_inputs] + [full(w) for w in full_inputs]
    out_shape, out_specs = [], []
    for w, dt in out_defs:
        if w == "t":
            out_shape.append(jax.ShapeDtypeStruct((bt, dt, r // tm, LANES, tm), BF16))
            out_specs.append(pl.BlockSpec((1, dt, 1, LANES, tm), lambda b, i: (b, 0, i, 0, 0)))
        else:
            out_shape.append(jax.ShapeDtypeStruct((bt, r, w), dt))
            out_specs.append(pl.BlockSpec((1, tm, w), lambda b, i: (b, i, 0)))
    return pl.pallas_call(
        body, out_shape=out_shape, grid=grid, in_specs=in_specs, out_specs=out_specs,
        compiler_params=_cparams(("arbitrary", "arbitrary")),
    )(x, *mods, *row_inputs, *full_inputs)


def _modulated(x_ref, sh_ref, sc_ref):
    return (x_ref[0] * (1.0 + sc_ref[0]) + sh_ref[0]).astype(BF16)


def _store_transposed(t_ref, x):
    for ch in range(x.shape[1] // LANES):
        t_ref[0, ch, 0] = x[:, ch * LANES:(ch + 1) * LANES].T.astype(BF16)


def _finish(o, z_ref, x_ref, g_ref, w_ref, lng_ref, lnb_ref, y_ref):
    gated = (o * _silu(z_ref[0])).astype(BF16)
    y = _dot(gated, w_ref[...])
    r = ALPHA * x_ref[0] + (1.0 + g_ref[0]) * y
    mu = jnp.mean(r, axis=-1, keepdims=True)
    d = r - mu
    var = jnp.mean(d * d, axis=-1, keepdims=True)
    y_ref[0] = d * lax.rsqrt(var + NORM_EPS) * lng_ref[...] + lnb_ref[...]


def _out_plain_body(x_ref, g_ref, o_ref, z_ref, w_ref, lng_ref, lnb_ref, y_ref):
    _finish(o_ref[0].astype(F32), z_ref, x_ref, g_ref, w_ref, lng_ref, lnb_ref, y_ref)


def _out_proj(x, gate, o, z, w_out, ln_g, ln_b):
    d = x.shape[-1]
    (y,) = _row_call(_out_plain_body, x, [gate], [o, z], [w_out.astype(BF16), ln_g.reshape(1, d), ln_b.reshape(1, d)],
                     [(d, F32)], ROW_TILE)
    return y


def _sb_proj_body(x_ref, sh_ref, sc_ref, wq_ref, wkv_ref, wz_ref, q_ref, kv_ref, z_ref, vt_ref):
    h = _modulated(x_ref, sh_ref, sc_ref)
    q_ref[0] = (_dot(h, wq_ref[...]) * (SB_HEAD_DIM ** -0.5 * LOG2E)).astype(BF16)
    kv = _dot(h, wkv_ref[...])
    kv_ref[0] = kv
    _store_transposed(vt_ref, kv[:, kv.shape[1] // 2:])
    z_ref[0] = _dot(h, wz_ref[...])


def _sb_proj(x, sh, sc, w_in):
    nq = SB_HEADS * SB_HEAD_DIM
    nkv = 2 * SB_KV_HEADS * SB_HEAD_DIM
    w = w_in.astype(BF16)
    return _row_call(_sb_proj_body, x, [sh, sc], [], [w[:, :nq], w[:, nq:nq + nkv], w[:, nq + nkv:]],
                     [(nq, BF16), (nkv, F32), (nq, F32), ("t", nkv // (2 * LANES))], ROW_TILE)


LOG2E = math.log2(math.e)


def _softplus2(t):
    return jnp.maximum(t, 0.0) + jnp.log2(1.0 + jnp.exp2(-jnp.abs(t)))


def _sb_tile(q2, k, v, mask, tri, r_prev, keys_on_lanes=False):
    z = _dot(q2, k) if keys_on_lanes else _dot_nt(q2, k)
    sp = _softplus2(z)
    u = sp if mask is None else jnp.where(mask, sp, 0.0)
    log_rem = _split_dot(u, tri) + r_prev
    a = jnp.exp2(z - sp - log_rem)
    if mask is not None:
        a = jnp.where(mask, a, 0.0)
    a = a.astype(BF16)
    pv = _dot_nt(a, v) if keys_on_lanes else _dot(a, v)
    return pv, r_prev + jnp.sum(u, axis=-1, keepdims=True)


def _tri(tk):
    return (lax.broadcasted_iota(jnp.int32, (tk, tk), 0) > lax.broadcasted_iota(jnp.int32, (tk, tk), 1)).astype(BF16)


def _sb_prompt_body(q_ref, k_ref, vt_ref, o_ref, acc_ref, *, tq):
    c = pl.program_id(1)
    i = pl.program_id(2)
    half = (c // 2) % 2
    lane = lax.broadcasted_iota(jnp.int32, (tq, LANES), 1)
    in_half = (lane // SB_HEAD_DIM) == half
    q = q_ref[0].astype(F32)
    qr = pltpu.roll(q, SB_HEAD_DIM, 1)
    first = half == 0
    qa = jnp.where(in_half, jnp.where(first, q, qr), 0.0)
    qb = jnp.where(in_half, jnp.where(first, qr, q), 0.0)
    q2 = jnp.concatenate([qa, qb], axis=0).astype(BF16)
    row = lax.broadcasted_iota(jnp.int32, (tq, tq), 0)
    col = lax.broadcasted_iota(jnp.int32, (tq, tq), 1)
    tri_t = (col > row).astype(BF16)
    diag_mask = jnp.concatenate([row < col] * 2, axis=1)
    acc_ref[...] = jnp.zeros_like(acc_ref)

    def scores(j):
        k = k_ref[0, pl.ds(pl.multiple_of(j * tq, tq), tq), :].astype(BF16)
        return _dot_nt(k, q2)

    def weights(z, mask, r_prev):
        sp = _softplus2(z)
        u = sp if mask is None else jnp.where(mask, sp, 0.0)
        cum = _dot(tri_t, u.astype(BF16))
        a = jnp.exp2(z - sp - (cum + r_prev))
        if mask is not None:
            a = jnp.where(mask, a, 0.0)
        return a.astype(BF16), r_prev + jnp.sum(u, axis=0, keepdims=True)

    a_prev, r = weights(scores(i), diag_mask, jnp.zeros((1, 2 * tq), F32))
    z = scores(jnp.maximum(i - 1, 0))

    def step(jj, carry):
        z, a_prev, r = carry
        j = i - jj
        acc_ref[...] += _dot(vt_ref[0, 0, j + 1], a_prev)
        z_next = scores(jnp.maximum(j - 1, 0))
        a, r = weights(z, None, r)
        return z_next, a, r

    _, a_prev, _ = lax.fori_loop(1, i + 1, step, (z, a_prev, r))
    acc_t = acc_ref[...] + _dot(vt_ref[0, 0, 0], a_prev)
    oa, ob = acc_t[:, :tq].T, acc_t[:, tq:].T
    oar, obr = pltpu.roll(oa, SB_HEAD_DIM, 1), pltpu.roll(ob, SB_HEAD_DIM, 1)
    low = lane < SB_HEAD_DIM
    o_ref[0] = jnp.where(low, jnp.where(first, oa, oar), jnp.where(first, obr, ob))


def _vt_spec(vt, chunk_of):
    return pl.BlockSpec((1, 1) + vt.shape[2:], lambda b_, c, i: (b_, chunk_of(c), 0, 0, 0))


def _sb_prompt_attn(q, kv, vt):
    b, t, _ = q.shape
    tq = min(TQ, t)
    assert vt.shape[-1] == tq
    nchunk = SB_HEADS * SB_HEAD_DIM // LANES
    return pl.pallas_call(
        functools.partial(_sb_prompt_body, tq=tq),
        out_shape=jax.ShapeDtypeStruct((b, t, nchunk * LANES), F32),
        grid=(b, nchunk, t // tq),
        in_specs=[pl.BlockSpec((1, tq, LANES), lambda b_, c, i: (b_, i, c)),
                  pl.BlockSpec((1, t, LANES), lambda b_, c, i: (b_, 0, c // 4)),
                  _vt_spec(vt, lambda c: c // 4)],
        out_specs=pl.BlockSpec((1, tq, LANES), lambda b_, c, i: (b_, i, c)),
        scratch_shapes=[pltpu.VMEM((LANES, 2 * tq), F32)],
        compiler_params=_cparams(("arbitrary", "arbitrary", "arbitrary")),
    )(q, kv, vt)


def _paged_call(body, page_table, cache, layer, row_block, dense, out_defs, scratch, descending=False, shared=()):
    nb, npg = page_table.shape
    pps = min(PAGES_PER_STEP, npg)
    nsteps = npg // pps
    page = cache.shape[3]
    rb, ridx = row_block

    def page_spec(s):
        def imap(b, c, pt):
            p = c * pps + s
            if descending:
                p = npg - 1 - p
            return (layer, pt[b, p], ridx, 0)
        return pl.BlockSpec((1, 1, rb, page), imap)

    def shared_spec(a):
        nd = a.ndim
        return pl.BlockSpec(a.shape, lambda b, c, pt: (0,) * nd)

    in_specs = [pl.BlockSpec((1,) + a.shape[1:], lambda b, c, pt: (b, 0, 0)) for a in dense]
    in_specs += [shared_spec(a) for a in shared]
    in_specs += [page_spec(s) for s in range(pps)]
    out_shape = [jax.ShapeDtypeStruct((nb, r, w), dt) for r, w, dt in out_defs]
    out_specs = [pl.BlockSpec((1, r, w), lambda b, c, pt: (b, 0, 0)) for r, w, _ in out_defs]
    grid_spec = pltpu.PrefetchScalarGridSpec(num_scalar_prefetch=1, grid=(nb, nsteps), in_specs=in_specs,
                                             out_specs=out_specs, scratch_shapes=scratch)
    return pl.pallas_call(
        functools.partial(body, n_dense=len(dense) + len(shared), pps=pps, nsteps=nsteps),
        out_shape=out_shape, grid_spec=grid_spec,
        compiler_params=_cparams(("arbitrary", "arbitrary")),
    )(page_table, *dense, *shared, *([cache] * pps))


def _as_page(rows, page):
    return jnp.pad(rows.transpose(0, 2, 1), ((0, 0), (0, 0), (0, page - rows.shape[1])))


def _sb_sample_body(pt_ref, q_ref, new_ref, *rest, n_dense, pps, nsteps, dt):
    pages, (o_ref, acc_ref, r_ref) = rest[:pps], rest[pps:]
    c = pl.program_id(1)
    nk = SB_KV_HEADS * SB_HEAD_DIM
    q2 = q_ref[0]
    m, tk = q2.shape[0], new_ref.shape[2]
    tri = _tri(tk)

    @pl.when(c == 0)
    def _():
        row_t = (lax.broadcasted_iota(jnp.int32, (m, tk), 0) // (SB_HEADS // SB_KV_HEADS)) % dt
        key = lax.broadcasted_iota(jnp.int32, (m, tk), 1)
        mask = (key < row_t) & (key < dt)
        nw = new_ref[0]
        pv, r_new = _sb_tile(q2, nw[:nk].astype(BF16), nw[nk:].astype(BF16), mask, tri, jnp.zeros((m, 1), F32), True)
        acc_ref[...] = pv
        r_ref[...] = r_new

    pgs = [pages[s][0, 0] for s in range(pps)]
    zs = [_dot(q2, pg[:nk].astype(BF16)) for pg in pgs]
    sps = [_softplus2(z) for z in zs]
    cums = _dot(jnp.concatenate(sps, axis=0).astype(BF16), tri)
    r = r_ref[...]
    pv = jnp.zeros(acc_ref.shape, F32)
    for s in range(pps):
        a = jnp.exp2(zs[s] - sps[s] - (cums[s * m:(s + 1) * m] + r))
        pv = pv + _dot_nt(a.astype(BF16), pgs[s][nk:].astype(BF16))
        r = r + jnp.sum(sps[s], axis=-1, keepdims=True)
    acc_ref[...] += pv
    r_ref[...] = r

    @pl.when(c == nsteps - 1)
    def _():
        o_ref[0] = acc_ref[...]


def _group_block_diag(q, groups):
    b, t, f = q.shape
    d = SB_HEAD_DIM
    n = f // (groups * d)
    q5 = q.reshape(b, t, groups, n, d)
    eye = jnp.eye(groups, dtype=q.dtype)
    out = q5.transpose(0, 2, 1, 3, 4)[:, :, :, :, None, :] * eye[None, :, None, None, :, None]
    return out.reshape(b, groups * t * n, groups * d)


def _group_diag_extract(acc, groups, t):
    b, r, f = acc.shape
    w = f // groups
    n = r // (groups * t)
    a5 = acc.reshape(b, groups, t * n, groups, w)
    picked = jnp.stack([a5[:, g, :, g, :] for g in range(groups)], axis=1)
    return picked.reshape(b, groups, t, n, w).transpose(0, 2, 1, 3, 4).reshape(b, t, groups * n * w)


def _sb_sample_attn(q, new_kv, cache, layer, page_table):
    b, dt, _ = q.shape
    page = cache.shape[3]
    qbd = _group_block_diag(q, SB_KV_HEADS)
    m = qbd.shape[1]
    nk = SB_KV_HEADS * SB_HEAD_DIM
    (acc,) = _paged_call(
        functools.partial(_sb_sample_body, dt=dt), page_table, cache, layer, (2 * nk, 0),
        [qbd, _as_page(new_kv, page)], [(m, nk, F32)],
        [pltpu.VMEM((m, nk), F32), pltpu.VMEM((m, 1), F32)], descending=True)
    return _group_diag_extract(acc, SB_KV_HEADS, dt)


def _page_view(cache):
    nl, pool, page = cache.shape[:3]
    feat = math.prod(cache.shape[3:])
    return jnp.moveaxis(cache.reshape(nl, pool, page, feat), 2, 3)


def _split_mod(mod, per_row_t=None):
    d = mod.shape[-1] // 3
    parts = [mod[:, k * d:(k + 1) * d] for k in range(3)]
    if per_row_t is None:
        return [p[:, None, :] for p in parts]
    return [jnp.repeat(p, per_row_t, axis=0)[None] for p in parts]


def _sb_layer(xp, xs, mod_p, mod_s, w_in, w_out, ln_g, ln_b, cache, layer, page_table):
    db, dt, d = xs.shape
    sh_p, sc_p, gt_p = _split_mod(mod_p)
    sh_s, sc_s, gt_s = _split_mod(mod_s, dt)
    xs_flat = xs.reshape(1, db * dt, d)
    qp, kvp, zp, vtp = _sb_proj(xp, sh_p, sc_p, w_in)
    qs, kvs, zs, _ = _sb_proj(xs_flat, sh_s, sc_s, w_in)
    op = _sb_prompt_attn(qp, kvp, vtp)
    os_ = _sb_sample_attn(qs.reshape(db, dt, -1), kvs.reshape(db, dt, -1), cache, layer, page_table)
    yp = _out_proj(xp, gt_p, op, zp, w_out, ln_g, ln_b)
    ys = _out_proj(xs_flat, gt_s, os_.reshape(1, db * dt, -1), zs, w_out, ln_g, ln_b)
    rows_shape = (2, SB_KV_HEADS, SB_HEAD_DIM)
    return (yp, ys.reshape(db, dt, d), kvp.reshape(kvp.shape[:2] + rows_shape), kvs.reshape((db, dt) + rows_shape))


def _softmax_tile(s, mask, v, m_ref, l_ref, acc_ref, keys_on_lanes=False):
    if mask is not None:
        s = jnp.where(mask, s, NEG_INF)
    m_prev = m_ref[...]
    m_new = jnp.maximum(m_prev, jnp.max(s, axis=-1, keepdims=True))
    p = jnp.exp(s - m_new)
    if mask is not None:
        p = jnp.where(mask, p, 0.0)
    scale = jnp.exp(m_prev - m_new)
    pb = p.astype(BF16)
    pv = _dot_nt(pb, v) if keys_on_lanes else _dot(pb, v)
    l_ref[...] = scale * l_ref[...] + jnp.sum(p, axis=-1, keepdims=True)
    acc_ref[...] = scale * acc_ref[...] + pv
    m_ref[...] = m_new


def _softmax_init(m_ref, l_ref, acc_ref):
    m_ref[...] = jnp.full_like(m_ref, NEG_INF)
    l_ref[...] = jnp.zeros_like(l_ref)
    acc_ref[...] = jnp.zeros_like(acc_ref)


def _softmax_scratch(m, f):
    return [pltpu.VMEM((m, 1), F32), pltpu.VMEM((m, 1), F32), pltpu.VMEM((m, f), F32)]


def _softmax_pages(q, kts, vts, mask, m_ref, l_ref, acc_ref):
    s = _dot(q, jnp.concatenate(kts, axis=1))
    p, scale = _softmax_stage(s, mask, m_ref, l_ref)
    acc_ref[...] = scale * acc_ref[...] + _dot_nt(p, jnp.concatenate(vts, axis=1))


def _rms(x, g):
    return x * lax.rsqrt(jnp.mean(x * x, axis=-1, keepdims=True) + NORM_EPS) * g


def _mla_proj_body(x_ref, sh_ref, sc_ref, c_ref, s1_ref, s2_ref, wcq_ref, wckv_ref, wpe_ref, wz_ref, g_ref,
                   cq_ref, rows_ref, kk_ref, z_ref, latt_ref):
    h = _modulated(x_ref, sh_ref, sc_ref)
    cq_ref[0] = _dot(h, wcq_ref[...])
    lat = _rms(_dot(h, wckv_ref[...]), g_ref[...])
    pe = _rope128(_dot(h, wpe_ref[...]), c_ref[0], s1_ref[0], s2_ref[0], MLA_ROPE_DIM // 2)
    rows_ref[0, :, :MLA_KV_RANK] = lat
    rows_ref[0, :, MLA_KV_RANK:] = pe[:, :MLA_ROPE_DIM]
    kk_ref[0] = jnp.concatenate([lat, pe], axis=1).astype(BF16)
    _store_transposed(latt_ref, lat)
    z_ref[0] = _dot(h, wz_ref[...])


def _mla_proj(x, sh, sc, tables, w_in, kv_norm):
    r, pe = MLA_KV_RANK, MLA_ROPE_DIM
    w = w_in.astype(BF16)
    n0, n1, n2 = MLA_Q_RANK, MLA_Q_RANK + r, MLA_Q_RANK + r + pe
    wpe = jnp.pad(w[:, n1:n2], ((0, 0), (0, LANES - pe)))
    nz = w.shape[1] - n2
    return _row_call(_mla_proj_body, x, [sh, sc], list(tables),
                     [w[:, :n0], w[:, n0:n1], wpe, w[:, n2:], kv_norm.reshape(1, r)],
                     [(n0, F32), (r + pe, F32), (2 * LANES, BF16), (nz, F32), ("t", r // LANES)], ROW_TILE)


def _mla_q_body(cq_ref, c_ref, s1_ref, s2_ref, g_ref, wn_ref, wuk_ref, wp_ref, q_ref):
    nb = _rms(cq_ref[0], g_ref[...]).astype(BF16)
    scale = (MLA_NOPE_DIM + MLA_ROPE_DIM) ** -0.5
    c, s1, s2 = c_ref[0], s1_ref[0], s2_ref[0]
    for h in range(MLA_HEADS):
        nope = _dot(nb, wn_ref[h]).astype(BF16)
        q_ref[0, :, 2 * h * LANES:(2 * h + 1) * LANES] = (_dot(nope, wuk_ref[h]) * scale).astype(BF16)
        pe = _rope128(_dot(nb, wp_ref[h]), c, s1, s2, MLA_ROPE_DIM // 2)
        q_ref[0, :, (2 * h + 1) * LANES:(2 * h + 2) * LANES] = (pe * scale).astype(BF16)


def _mla_q(cq, tables, q_norm, w_q_up, w_uk):
    hds, dn, dr = MLA_HEADS, MLA_NOPE_DIM, MLA_ROPE_DIM
    wq = w_q_up.astype(BF16).reshape(MLA_Q_RANK, hds, dn + dr)
    wn = wq[:, :, :dn].transpose(1, 0, 2)
    wp = jnp.pad(wq[:, :, dn:].transpose(1, 0, 2), ((0, 0), (0, 0), (0, LANES - dr)))
    wuk = w_uk.astype(BF16).transpose(1, 2, 0)
    (q,) = _row_call(_mla_q_body, cq, [], list(tables), [q_norm.reshape(1, MLA_Q_RANK), wn, wuk, wp],
                     [(hds * 2 * LANES, BF16)], ROW_TILE)
    return q


def _causal_mask(i, j, tq, rows):
    qpos = i * tq + lax.broadcasted_iota(jnp.int32, (tq, tq), 0)
    kpos = j * tq + lax.broadcasted_iota(jnp.int32, (tq, tq), 1)
    mask = kpos <= qpos
    return mask if rows == 1 else jnp.concatenate([mask] * rows, axis=0)


def _softmax_stage(s, mask, m_ref, l_ref, axis=-1):
    if mask is not None:
        s = jnp.where(mask, s, NEG_INF)
    m_prev = m_ref[...]
    m_new = jnp.maximum(m_prev, jnp.max(s, axis=axis, keepdims=True))
    p = jnp.exp(s - m_new)
    if mask is not None:
        p = jnp.where(mask, p, 0.0)
    scale = jnp.exp(m_prev - m_new)
    l_ref[...] = scale * l_ref[...] + jnp.sum(p, axis=axis, keepdims=True)
    m_ref[...] = m_new
    return p.astype(BF16), scale


def _flash_loop(q, load_k, load_v, mask_fn, j_lo, i, m_ref, l_ref, acc_ref, transposed=False):
    _softmax_init(m_ref, l_ref, acc_ref)
    axis = 0 if transposed else -1

    def scores(j):
        return _dot_nt(load_k(j), q) if transposed else _dot_nt(q, load_k(j))

    def pv_update(p_prev, scale_prev, j_prev):
        pv = _dot(load_v(j_prev), p_prev) if transposed else _dot(p_prev, load_v(j_prev))
        acc_ref[...] = scale_prev * acc_ref[...] + pv

    def step(j, carry):
        s, p_prev, scale_prev = carry
        pv_update(p_prev, scale_prev, jnp.maximum(j - 1, j_lo))
        s_next = scores(j + 1)
        p, scale = _softmax_stage(s, mask_fn(j, False), m_ref, l_ref, axis)
        return s_next, p, scale

    s0 = scores(j_lo)
    init = (s0, jnp.zeros(s0.shape, BF16), jnp.ones(m_ref.shape, F32))
    s, p_prev, scale_prev = lax.fori_loop(j_lo, i, step, init)
    pv_update(p_prev, scale_prev, jnp.maximum(i - 1, j_lo))
    p, scale = _softmax_stage(s, mask_fn(i, True), m_ref, l_ref, axis)
    pv_update(p, scale, i)


def _softmax_scratch_t(m, f):
    return [pltpu.VMEM((1, m), F32), pltpu.VMEM((1, m), F32), pltpu.VMEM((f, m), F32)]


def _causal_mask_t(i, j, tq, cols):
    kpos = j * tq + lax.broadcasted_iota(jnp.int32, (tq, tq), 0)
    qpos = i * tq + lax.broadcasted_iota(jnp.int32, (tq, tq), 1)
    mask = kpos <= qpos
    return mask if cols == 1 else jnp.concatenate([mask] * cols, axis=1)


def _mla_prompt_body(q_ref, kk_ref, vt_ref, o_ref, m_ref, l_ref, acc_ref, *, tq):
    i = pl.program_id(2)

    def load_k(j):
        return kk_ref[0, pl.ds(pl.multiple_of(j * tq, tq), tq), :]

    def load_v(j):
        return vt_ref[0, 0, j]

    def mask_fn2(j, diag):
        return _causal_mask_t(i, j, tq, 2) if diag else None

    q = q_ref[0]
    q2 = jnp.concatenate([q[:, :2 * LANES], q[:, 2 * LANES:]], axis=0)
    _flash_loop(q2, load_k, load_v, mask_fn2, 0, i, m_ref, l_ref, acc_ref, transposed=True)
    o_t = acc_ref[...] / l_ref[...]
    o_ref[0, :, :MLA_KV_RANK] = o_t[:, :tq].T.astype(BF16)
    o_ref[0, :, MLA_KV_RANK:] = o_t[:, tq:].T.astype(BF16)


def _mla_prompt_attn(q, kk, latt):
    b, t, _ = kk.shape
    tq = min(TQ, t)
    assert latt.shape[-1] == tq
    return pl.pallas_call(
        functools.partial(_mla_prompt_body, tq=tq),
        out_shape=jax.ShapeDtypeStruct((b, t, MLA_HEADS * MLA_KV_RANK), BF16),
        grid=(b, MLA_HEADS // 2, t // tq),
        in_specs=[pl.BlockSpec((1, tq, 4 * LANES), lambda b_, h, i: (b_, i, h)),
                  pl.BlockSpec((1, t, 2 * LANES), lambda b_, h, i: (b_, 0, 0)),
                  _vt_spec(latt, lambda h: 0)],
        out_specs=pl.BlockSpec((1, tq, 2 * MLA_KV_RANK), lambda b_, h, i: (b_, i, h)),
        scratch_shapes=_softmax_scratch_t(2 * tq, MLA_KV_RANK),
        compiler_params=_cparams(("arbitrary", "arbitrary", "arbitrary")),
    )(q, kk, latt)


def _new_tile_mask(m, tk, rows_per_token, dt, inclusive):
    row_t = (lax.broadcasted_iota(jnp.int32, (m, tk), 0) // rows_per_token) % dt
    key = lax.broadcasted_iota(jnp.int32, (m, tk), 1)
    return ((key <= row_t) if inclusive else (key < row_t)) & (key < dt)


def _mla_sample_body(pt_ref, q_ref, new_ref, *rest, n_dense, pps, nsteps, dt):
    pages, (o_ref, m_ref, l_ref, acc_ref) = rest[:pps], rest[pps:]
    c = pl.program_id(1)
    q = q_ref[0]
    m, tk = q.shape[0], new_ref.shape[2]

    @pl.when(c == 0)
    def _():
        _softmax_init(m_ref, l_ref, acc_ref)
        nw = new_ref[0].astype(BF16)
        _softmax_tile(_dot(q, nw), _new_tile_mask(m, tk, MLA_HEADS, dt, True), nw[:MLA_KV_RANK], m_ref, l_ref, acc_ref, True)

    pgs = [pages[s][0, 0].astype(BF16) for s in range(pps)]
    _softmax_pages(q, pgs, [pg[:MLA_KV_RANK] for pg in pgs], None, m_ref, l_ref, acc_ref)

    @pl.when(c == nsteps - 1)
    def _():
        o_ref[0] = (acc_ref[...] / l_ref[...]).astype(BF16)


def _mla_sample_attn(q, new_rows, cache, layer, page_table):
    b, dt, _ = q.shape
    feat = MLA_KV_RANK + MLA_ROPE_DIM
    q160 = q.reshape(b, dt * MLA_HEADS, 2 * LANES)[:, :, :feat]
    m = dt * MLA_HEADS
    (o,) = _paged_call(
        functools.partial(_mla_sample_body, dt=dt), page_table, cache, layer, (feat, 0),
        [q160, _as_page(new_rows, cache.shape[3])], [(m, MLA_KV_RANK, BF16)], _softmax_scratch(m, MLA_KV_RANK))
    return o.reshape(b, dt, MLA_HEADS * MLA_KV_RANK)


def _out_mla_body(x_ref, g_ref, o_ref, z_ref, wuv_ref, w_ref, lng_ref, lnb_ref, y_ref):
    _finish(_dot(o_ref[0], wuv_ref[...]), z_ref, x_ref, g_ref, w_ref, lng_ref, lnb_ref, y_ref)


def _mla_out_proj(x, gate, o_lat, z, w_uv, w_out, ln_g, ln_b):
    d = x.shape[-1]
    eye = jnp.eye(MLA_HEADS, dtype=BF16)
    wuv_bd = (w_uv.astype(BF16).transpose(1, 0, 2)[:, :, None, :] * eye[:, None, :, None]).reshape(
        MLA_HEADS * MLA_KV_RANK, MLA_HEADS * MLA_V_DIM)
    (y,) = _row_call(_out_mla_body, x, [gate], [o_lat, z],
                     [wuv_bd, w_out.astype(BF16), ln_g.reshape(1, d), ln_b.reshape(1, d)], [(d, F32)], ROW_TILE)
    return y


def _mla_layer(xp, xs, mod_p, mod_s, pos_p, pos_s, w_in, q_norm, w_q_up, kv_norm, w_uk, w_uv, w_out, ln_g, ln_b,
               cache, layer, page_table):
    db, dt, d = xs.shape
    sh_p, sc_p, gt_p = _split_mod(mod_p)
    sh_s, sc_s, gt_s = _split_mod(mod_s, dt)
    xs_flat = xs.reshape(1, db * dt, d)
    tab_p = _rope_tables(pos_p, MLA_ROPE_DIM, MLA_ROPE_THETA, MLA_ROPE_DIM)
    tab_s = _rope_tables(jnp.tile(pos_s, db), MLA_ROPE_DIM, MLA_ROPE_THETA, MLA_ROPE_DIM)
    cq_p, rows_p, kk_p, z_p, latt_p = _mla_proj(xp, sh_p, sc_p, tab_p, w_in, kv_norm)
    cq_s, rows_s, _, z_s, _ = _mla_proj(xs_flat, sh_s, sc_s, tab_s, w_in, kv_norm)
    q_p = _mla_q(cq_p, tab_p, q_norm, w_q_up, w_uk)
    q_s = _mla_q(cq_s, tab_s, q_norm, w_q_up, w_uk)
    o_p = _mla_prompt_attn(q_p, kk_p, latt_p)
    rows_s = rows_s.reshape(db, dt, -1)
    o_s = _mla_sample_attn(q_s.reshape(db, dt, -1), rows_s, cache, layer, page_table)
    yp = _mla_out_proj(xp, gt_p, o_p, z_p, w_uv, w_out, ln_g, ln_b)
    ys = _mla_out_proj(xs_flat, gt_s, o_s.reshape(1, db * dt, -1), z_s, w_uv, w_out, ln_g, ln_b)
    return yp, ys.reshape(db, dt, d), rows_p, rows_s


def _diff_proj_body(x_ref, sh_ref, sc_ref, c_ref, s1_ref, s2_ref, wq_ref, wk_ref, wv_ref, wz_ref,
                    q_ref, kv_ref, z_ref, vt_ref):
    h = _modulated(x_ref, sh_ref, sc_ref)
    c, s1, s2 = c_ref[0], s1_ref[0], s2_ref[0]
    half = DIFF_HEAD_DIM // 8
    q = _dot(h, wq_ref[...])
    for ch in range(q.shape[1] // LANES):
        qc = _rope128(q[:, ch * LANES:(ch + 1) * LANES], c, s1, s2, half)
        q_ref[0, :, ch * LANES:(ch + 1) * LANES] = (qc * (DIFF_HEAD_DIM ** -0.5)).astype(BF16)
    k = _dot(h, wk_ref[...])
    nk = k.shape[1]
    for ch in range(nk // LANES):
        kv_ref[0, :, ch * LANES:(ch + 1) * LANES] = _rope128(k[:, ch * LANES:(ch + 1) * LANES], c, s1, s2, half)
    v = _dot(h, wv_ref[...])
    kv_ref[0, :, nk:] = v
    _store_transposed(vt_ref, v)
    z_ref[0] = _dot(h, wz_ref[...])


def _diff_proj(x, sh, sc, tables, w_in):
    hd, g, d = DIFF_HEADS, DIFF_KV_HEADS, DIFF_HEAD_DIM
    n1 = hd * 2 * d
    n2 = n1 + g * 2 * d
    n3 = n2 + g * 2 * d
    w = w_in.astype(BF16)
    return _row_call(_diff_proj_body, x, [sh, sc], list(tables), [w[:, :n1], w[:, n1:n2], w[:, n2:n3], w[:, n3:]],
                     [(n1, BF16), (n3 - n1, F32), (w.shape[1] - n3, F32), ("t", (n3 - n2) // LANES)], ROW_TILE)


def _diff_lambda(lam, lam_init):
    a = jnp.sum(lam[0:1] * lam[1:2], axis=-1, keepdims=True)
    b = jnp.sum(lam[2:3] * lam[3:4], axis=-1, keepdims=True)
    return jnp.exp(a) - jnp.exp(b) + lam_init


def _diff_combine(acc, l, lam, sub, lam_init, n):
    o = acc[:n] / l[:n] - lam * (acc[n:] / l[n:])
    return _rms(o, sub) * (1.0 - lam_init)


def _diff_prompt_body(q_ref, k_ref, vt_ref, lam_ref, sub_ref, o_ref, m_ref, l_ref, acc_ref, *, tq, lam_init):
    i = pl.program_id(2)
    lane = lax.broadcasted_iota(jnp.int32, (tq, LANES), 1)
    q = q_ref[0]
    zero = jnp.zeros_like(q)
    q2 = jnp.concatenate([jnp.where(lane < DIFF_HEAD_DIM, q, zero), jnp.where(lane >= DIFF_HEAD_DIM, q, zero)], axis=0)

    def load_k(j):
        return k_ref[0, pl.ds(pl.multiple_of(j * tq, tq), tq), :].astype(BF16)

    def load_v(j):
        return vt_ref[0, 0, j]

    def mask_fn(j, diag):
        return _causal_mask_t(i, j, tq, 2) if diag else None

    _flash_loop(q2, load_k, load_v, mask_fn, 0, i, m_ref, l_ref, acc_ref, transposed=True)
    acc, l = acc_ref[...], l_ref[...]
    o_t = acc[:, :tq] / l[:, :tq] - _diff_lambda(lam_ref[...], lam_init) * (acc[:, tq:] / l[:, tq:])
    o_ref[0] = _rms(o_t.T, sub_ref[...]) * (1.0 - lam_init)


def _diff_prompt_attn(q, kv, vt, lam, subln, lam_init):
    b, t, _ = q.shape
    tq = min(TQ, t)
    assert vt.shape[-1] == tq
    hpg = DIFF_HEADS // DIFF_KV_HEADS
    return pl.pallas_call(
        functools.partial(_diff_prompt_body, tq=tq, lam_init=lam_init),
        out_shape=jax.ShapeDtypeStruct((b, t, DIFF_HEADS * LANES), F32),
        grid=(b, DIFF_HEADS, t // tq),
        in_specs=[pl.BlockSpec((1, tq, LANES), lambda b_, n, i: (b_, i, n)),
                  pl.BlockSpec((1, t, LANES), lambda b_, n, i: (b_, 0, n // hpg)),
                  _vt_spec(vt, lambda n: n // hpg),
                  pl.BlockSpec(lam.shape, lambda b_, n, i: (0, 0)),
                  pl.BlockSpec((1, LANES), lambda b_, n, i: (0, 0))],
        out_specs=pl.BlockSpec((1, tq, LANES), lambda b_, n, i: (b_, i, n)),
        scratch_shapes=_softmax_scratch_t(2 * tq, LANES),
        compiler_params=_cparams(("arbitrary", "arbitrary", "arbitrary")),
    )(q, kv, vt, lam, subln.reshape(1, LANES))


def _diff_sample_body(pt_ref, q_ref, new_ref, lam_ref, sub_ref, *rest, n_dense, pps, nsteps, dt, lam_init):
    pages, (o_ref, m_ref, l_ref, acc_ref) = rest[:pps], rest[pps:]
    c = pl.program_id(1)
    g = DIFF_KV_HEADS
    stride = 2 * g
    nrow = new_ref.shape[1]
    q = q_ref[0]
    m = q.shape[0]
    mg = m // g
    rows = [slice(gi * mg, (gi + 1) * mg) for gi in range(g)]

    def tiles(pgs, pos_ok):
        n = len(pgs)
        pb = [pg.astype(BF16) for pg in pgs]
        s = jnp.concatenate([_dot_nt(q, p) for p in pb], axis=1)
        col = lax.broadcasted_iota(jnp.int32, (m, n * nrow), 1)
        row_g = lax.broadcasted_iota(jnp.int32, (m, n * nrow), 0) // mg
        mask = (col % stride) == row_g
        if pos_ok is not None:
            mask = mask & pos_ok
        s = jnp.where(mask, s, NEG_INF)
        m_prev = m_ref[...]
        m_new = jnp.maximum(m_prev, jnp.max(s, axis=-1, keepdims=True))
        p = jnp.where(mask, jnp.exp(s - m_new), 0.0)
        scale = jnp.exp(m_prev - m_new)
        l_ref[...] = scale * l_ref[...] + jnp.sum(p, axis=-1, keepdims=True)
        m_ref[...] = m_new
        pv = jnp.zeros(acc_ref.shape, F32)
        for k in range(n):
            pk = pltpu.roll(p[:, k * nrow:(k + 1) * nrow], g, 1).astype(BF16)
            pv = pv + _dot(pk, pb[k])
        acc_ref[...] = scale * acc_ref[...] + pv

    @pl.when(c == 0)
    def _():
        _softmax_init(m_ref, l_ref, acc_ref)
        pos = lax.broadcasted_iota(jnp.int32, (m, nrow), 1) // stride
        tok = _row_token(m, nrow, DIFF_HEADS // g, dt)
        tiles([new_ref[0]], (pos <= tok) & (pos < dt))

    tiles([pages[s][0, 0] for s in range(pps)], None)

    @pl.when(c == nsteps - 1)
    def _():
        lam = _diff_lambda(lam_ref[0], lam_init)
        n = mg // 2
        for gi in range(g):
            r = rows[gi]
            o_ref[0, gi * n:(gi + 1) * n, :] = _diff_combine(acc_ref[r, :], l_ref[r, :], lam, sub_ref[0], lam_init, n)


def _diff_sample_attn(q, new_kv, cache, layer, page_table, lam, subln, lam_init):
    b, dt, _ = q.shape
    g, d = DIFF_KV_HEADS, DIFF_HEAD_DIM
    n = DIFF_HEADS // g
    page = cache.shape[2] // (2 * g)
    q6 = q.reshape(b, dt, g, n, 2, d).transpose(0, 2, 4, 1, 3, 5)
    eye = jnp.eye(2, dtype=q.dtype)
    qrows = (q6[..., None, :] * eye[None, None, :, None, None, :, None]).reshape(b, g * 2 * dt * n, 2 * d)
    new_page = jnp.pad(new_kv.reshape(b, dt, 2 * g, LANES), ((0, 0), (0, page - dt), (0, 0), (0, 0)))
    new_page = new_page.reshape(b, page * 2 * g, LANES)
    m = qrows.shape[1]
    (o,) = _paged_call(
        functools.partial(_diff_sample_body, dt=dt, lam_init=lam_init), page_table, cache, layer,
        (cache.shape[2], 0), [qrows, new_page, jnp.broadcast_to(lam, (b,) + lam.shape),
                              jnp.broadcast_to(subln.reshape(1, 1, LANES), (b, 1, LANES))],
        [(m // 2, LANES, F32)], _softmax_scratch(m, LANES))
    return o.reshape(b, g, dt, n, LANES).transpose(0, 2, 1, 3, 4).reshape(b, dt, DIFF_HEADS * LANES)


def _diff_layer(xp, xs, mod_p, mod_s, pos_p, pos_s, layer_idx, w_in, lam, subln, w_out, ln_g, ln_b,
                cache, layer, page_table):
    db, dt, d = xs.shape
    lam_init = 0.8 - 0.6 * math.exp(-0.3 * layer_idx)
    sh_p, sc_p, gt_p = _split_mod(mod_p)
    sh_s, sc_s, gt_s = _split_mod(mod_s, dt)
    xs_flat = xs.reshape(1, db * dt, d)
    tab_p = _rope_tables(pos_p, DIFF_HEAD_DIM // 4, ROPE_THETA, DIFF_HEAD_DIM)
    tab_s = _rope_tables(jnp.tile(pos_s, db), DIFF_HEAD_DIM // 4, ROPE_THETA, DIFF_HEAD_DIM)
    q_p, kv_p, z_p, vt_p = _diff_proj(xp, sh_p, sc_p, tab_p, w_in)
    q_s, kv_s, z_s, _ = _diff_proj(xs_flat, sh_s, sc_s, tab_s, w_in)
    o_p = _diff_prompt_attn(q_p, kv_p, vt_p, lam, subln, lam_init)
    kv_s = kv_s.reshape(db, dt, -1)
    nl, pool, page = cache.shape[:3]
    cache_rows = cache.reshape(nl, pool, page * 2 * DIFF_KV_HEADS, LANES)
    o_s = _diff_sample_attn(q_s.reshape(db, dt, -1), kv_s, cache_rows, layer, page_table, lam, subln, lam_init)
    yp = _out_proj(xp, gt_p, o_p, z_p, w_out, ln_g, ln_b)
    ys = _out_proj(xs_flat, gt_s, o_s.reshape(1, db * dt, -1), z_s, w_out, ln_g, ln_b)
    rows_shape = (2, DIFF_KV_HEADS, 2 * DIFF_HEAD_DIM)
    return yp, ys.reshape(db, dt, d), kv_p.reshape(kv_p.shape[:2] + rows_shape), kv_s.reshape((db, dt) + rows_shape)


NSA_HPG = NSA_HEADS // NSA_KV_GROUPS
GL_PAD = LANES


def _nsa_proj_body(x_ref, sh_ref, sc_ref, c_ref, s1_ref, s2_ref, wq_ref, wkv_ref, ww_ref, wg_ref, wz_ref,
                   q_ref, kv_ref, w_ref, gl_ref, z_ref, vst_ref, vwt_ref):
    h = _modulated(x_ref, sh_ref, sc_ref)
    c, s1, s2 = c_ref[0], s1_ref[0], s2_ref[0]
    half = NSA_HEAD_DIM // 8
    q = _dot(h, wq_ref[...])
    for ch in range(q.shape[1] // LANES):
        qc = _rope128(q[:, ch * LANES:(ch + 1) * LANES], c, s1, s2, half)
        q_ref[0, :, ch * LANES:(ch + 1) * LANES] = (qc * (NSA_HEAD_DIM ** -0.5)).astype(BF16)
    kv = _dot(h, wkv_ref[...])
    kv_ref[0, :, :2 * LANES] = kv[:, :2 * LANES]
    kv_ref[0, :, 2 * LANES:3 * LANES] = _rope128(kv[:, 2 * LANES:3 * LANES], c, s1, s2, half)
    kv_ref[0, :, 3 * LANES:] = kv[:, 3 * LANES:]
    _store_transposed(vst_ref, kv[:, 3 * LANES:])
    w = _dot(h, ww_ref[...])
    w_ref[0, :, :LANES] = _rope128(w[:, :LANES], c, s1, s2, half)
    w_ref[0, :, LANES:] = w[:, LANES:]
    _store_transposed(vwt_ref, w[:, LANES:])
    gl_ref[0] = _dot(h, wg_ref[...])
    z_ref[0] = _dot(h, wz_ref[...])


def _nsa_proj(x, sh, sc, tables, w_in):
    hd, g, d = NSA_HEADS, NSA_KV_GROUPS, NSA_HEAD_DIM
    n1 = hd * d
    n2 = n1 + 4 * g * d
    n3 = n2 + 2 * g * d
    n4 = n3 + 3 * hd
    w = w_in.astype(BF16)
    wg = jnp.pad(w[:, n3:n4], ((0, 0), (0, GL_PAD - 3 * hd)))
    return _row_call(_nsa_proj_body, x, [sh, sc], list(tables), [w[:, :n1], w[:, n1:n2], w[:, n2:n3], wg, w[:, n4:]],
                     [(n1, BF16), (n2 - n1, F32), (n3 - n2, F32), (GL_PAD, F32), (w.shape[1] - n4, F32),
                      ("t", 1), ("t", 1)], ROW_TILE)


def _cmp_weights(pe, w1, b1, w2, b2):
    g = NSA_KV_GROUPS
    eye = jnp.eye(g, dtype=BF16)
    w1b = w1.astype(BF16)
    bd = (w1b[:, :, None, :, None, :] * eye[None, None, :, None, :, None]).reshape(
        2, CMP_BLOCK, g * NSA_HEAD_DIM, g * CMP_HIDDEN)
    wcat = jnp.concatenate([bd[:, :CMP_STRIDE], bd[:, CMP_STRIDE:]], axis=3)
    wcat = wcat.reshape(2, CMP_STRIDE // 2, 2 * g * NSA_HEAD_DIM, 2 * g * CMP_HIDDEN)
    w1f = w1b.reshape(2, CMP_BLOCK * NSA_HEAD_DIM, CMP_HIDDEN)
    pef = jnp.pad(pe.astype(BF16).reshape(2, 1, CMP_BLOCK * NSA_HEAD_DIM), ((0, 0), (0, 7), (0, 0)))
    w2b = w2.astype(BF16)
    w2bd = (w2b[:, None, :, None, :] * eye[None, :, None, :, None]).reshape(2, g * CMP_HIDDEN, g * NSA_HEAD_DIM)
    b2t = jnp.tile(b2, (1, g)).reshape(2, 1, g * NSA_HEAD_DIM)
    return [wcat, w1f, pef, b1.reshape(2, 1, CMP_HIDDEN), w2bd, b2t]


def _compress(get_rows, nblk, wcat_ref, w1f_ref, pe_ref, b1_ref, w2_ref, b2_ref):
    outs = []
    nh = NSA_KV_GROUPS * CMP_HIDDEN
    for comp in range(2):
        acc = jnp.zeros((nblk, 2 * nh), F32)
        for pp in range(CMP_STRIDE // 2):
            x = jnp.concatenate([get_rows(comp, 2 * pp), get_rows(comp, 2 * pp + 1)], axis=1).astype(BF16)
            acc = acc + _dot(x, wcat_ref[comp, pp])
        bias = _dot(pe_ref[comp], w1f_ref[comp])[0:1] + b1_ref[comp]
        bias = jnp.concatenate([bias] * NSA_KV_GROUPS, axis=1)
        hid = _silu(acc[:, :nh] + pltpu.roll(acc[:, nh:], nblk - 1, 0) + bias).astype(BF16)
        outs.append(_dot(hid, w2_ref[comp]) + b2_ref[comp])
    return outs


def _nsa_compress_body(kc_ref, vc_ref, wcat_ref, w1f_ref, pe_ref, b1_ref, w2_ref, b2_ref, k_ref, v_ref, *, nblk):
    def get_rows(comp, p):
        return (kc_ref, vc_ref)[comp][0, pl.ds(p, nblk, stride=CMP_STRIDE), :]
    k, v = _compress(get_rows, nblk, wcat_ref, w1f_ref, pe_ref, b1_ref, w2_ref, b2_ref)
    k_ref[0] = k
    v_ref[0] = v


def _nsa_compress(kv, cw):
    b, t, f = kv.shape
    nblk = t // CMP_STRIDE
    full = lambda a: pl.BlockSpec(a.shape, lambda b_: (0,) * a.ndim)
    return pl.pallas_call(
        functools.partial(_nsa_compress_body, nblk=nblk),
        out_shape=[jax.ShapeDtypeStruct((b, nblk, LANES), F32)] * 2,
        grid=(b,),
        in_specs=[pl.BlockSpec((1, t, LANES), lambda b_: (b_, 0, 0)),
                  pl.BlockSpec((1, t, LANES), lambda b_: (b_, 0, 1))] + [full(a) for a in cw],
        out_specs=[pl.BlockSpec((1, nblk, LANES), lambda b_: (b_, 0, 0))] * 2,
        compiler_params=_cparams(("arbitrary",)),
    )(kv, kv, *cw)


def _place_in_half(x, src_half, dst_half):
    return jnp.where(dst_half == src_half, x, pltpu.roll(x, LANES // 2, 1))


def _pair_queries(q, dst_half):
    q = q.astype(F32)
    lane = lax.broadcasted_iota(jnp.int32, q.shape, 1)
    keep = (lane // (LANES // 2)) == dst_half
    qa = jnp.where(keep, _place_in_half(q, 0, dst_half), 0.0)
    qb = jnp.where(keep, _place_in_half(q, 1, dst_half), 0.0)
    return jnp.concatenate([qa, qb], axis=0).astype(BF16)


def _pair_outputs(o2, src_half):
    tq = o2.shape[0] // 2
    lane = lax.broadcasted_iota(jnp.int32, (tq, LANES), 1)
    return jnp.where(lane < LANES // 2, _place_in_half(o2[:tq], src_half, 0), _place_in_half(o2[tq:], src_half, 1))


def _select_blocks(imp, cur, n_sel, k_eff):
    sp, n = imp.shape
    blk = lax.broadcasted_iota(jnp.int32, (sp, n), 0)
    blkf = blk.astype(F32)
    forced = (blk == 0) | (blk == cur) | (blk == cur - 1)
    allowed = (blk <= cur) & (blk < n_sel)
    score = jnp.where(blk > cur, NEG_INF, jnp.where(forced, FORCE_SCORE, imp))
    lowest = -3.0e38
    score = jnp.where(blk < n_sel, score, lowest)
    sel = jnp.zeros((sp, n), F32)
    for _ in range(k_eff):
        mx = jnp.max(score, axis=0, keepdims=True)
        first = jnp.min(jnp.where(score == mx, blkf, float(sp)), axis=0, keepdims=True)
        hit = blkf == first
        sel = jnp.where(hit, 1.0, sel)
        score = jnp.where(hit, lowest, score)
    return jnp.where(allowed, sel, 0.0)


def _overlap_t(n_sel_pad, n_cmp_pad, n_sel, n_cmp):
    ci = np.arange(n_cmp_pad)[None, :] * CMP_STRIDE
    sj = np.arange(n_sel_pad)[:, None] * SEL_BLOCK
    ov = (ci < sj + SEL_BLOCK) & (ci + CMP_BLOCK > sj)
    ov &= (np.arange(n_cmp_pad)[None, :] < n_cmp) & (np.arange(n_sel_pad)[:, None] < n_sel)
    return jnp.asarray(ov, BF16)


def _cmp_softmax(s, mask):
    s = jnp.where(mask, s, NEG_INF)
    p = jnp.where(mask, jnp.exp(s - jnp.max(s, axis=-1, keepdims=True)), 0.0)
    l = jnp.sum(p, axis=-1, keepdims=True)
    return p / jnp.where(l > 0.0, l, 1.0)


def _nsa_cmpsel_body(q_ref, k_ref, v_ref, ov_ref, o_ref, sel_ref, *, tq, n_sel, k_eff):
    g = pl.program_id(1)
    i = pl.program_id(2)
    nblk = k_ref.shape[1]
    nch = NSA_HPG // 2
    q2s = [_pair_queries(q_ref[0, :, ch * LANES:(ch + 1) * LANES], g) for ch in range(nch)]
    qall = jnp.concatenate(q2s, axis=0)
    s = _dot_nt(qall, k_ref[0].astype(BF16))
    m = qall.shape[0]
    qpos = i * tq + lax.broadcasted_iota(jnp.int32, (m, nblk), 0) % tq
    cmp_end = lax.broadcasted_iota(jnp.int32, (m, nblk), 1) * CMP_STRIDE + (CMP_BLOCK - 1)
    p = _cmp_softmax(s, cmp_end <= qpos)
    o = _dot(p.astype(BF16), v_ref[0].astype(BF16))
    for ch in range(nch):
        o_ref[0, :, ch * LANES:(ch + 1) * LANES] = _pair_outputs(o[2 * ch * tq:(2 * ch + 2) * tq], g)
    psum = p[:tq]
    for n in range(1, NSA_HPG):
        psum = psum + p[n * tq:(n + 1) * tq]
    hi = psum.astype(BF16)
    lo = (psum - hi.astype(F32)).astype(BF16)
    imp_t = _dot_nt(ov_ref[...], hi) + _dot_nt(ov_ref[...], lo)
    sp = imp_t.shape[0]
    cur = (i * tq + lax.broadcasted_iota(jnp.int32, (1, tq), 1)) // SEL_BLOCK
    sel_t = _select_blocks(imp_t, cur, n_sel, k_eff)
    if sp < LANES:
        sel_t = jnp.concatenate([sel_t, jnp.zeros((LANES - sp, tq), F32)], axis=0)
    sel_ref[0, 0] = sel_t.astype(BF16)


def _nsa_cmpsel(q, kcmp, vcmp):
    b, t, f = q.shape
    tq = min(TQ, t)
    nblk = kcmp.shape[1]
    n_sel = -(-t // SEL_BLOCK)
    assert n_sel <= LANES
    sp = -(-n_sel // 8) * 8
    ov = _overlap_t(sp, nblk, n_sel, nblk - 1)
    gw = f // NSA_KV_GROUPS
    return pl.pallas_call(
        functools.partial(_nsa_cmpsel_body, tq=tq, n_sel=n_sel, k_eff=min(N_SELECT, n_sel)),
        out_shape=[jax.ShapeDtypeStruct((b, t, f), F32), jax.ShapeDtypeStruct((b, NSA_KV_GROUPS, LANES, t), BF16)],
        grid=(b, NSA_KV_GROUPS, t // tq),
        in_specs=[pl.BlockSpec((1, tq, gw), lambda b_, g, i: (b_, i, g)),
                  pl.BlockSpec((1, nblk, LANES), lambda b_, g, i: (b_, 0, 0)),
                  pl.BlockSpec((1, nblk, LANES), lambda b_, g, i: (b_, 0, 0)),
                  pl.BlockSpec(ov.shape, lambda b_, g, i: (0, 0))],
        out_specs=[pl.BlockSpec((1, tq, gw), lambda b_, g, i: (b_, i, g)),
                   pl.BlockSpec((1, 1, LANES, tq), lambda b_, g, i: (b_, g, 0, i))],
        compiler_params=_cparams(("arbitrary", "arbitrary", "arbitrary")),
    )(q, kcmp, vcmp, ov)


def _block_expand(base_pos, n_blocks_pad, tk):
    blk = lax.broadcasted_iota(jnp.int32, (n_blocks_pad, tk), 0)
    pos = base_pos + lax.broadcasted_iota(jnp.int32, (n_blocks_pad, tk), 1)
    return (blk == pos // SEL_BLOCK).astype(BF16)


def _nsa_prompt_body(*refs, tq, mode):
    if mode == "sel":
        q_ref, k_ref, vt_ref, sel_ref, o_ref, m_ref, l_ref, acc_ref = refs
    else:
        q_ref, k_ref, vt_ref, o_ref, m_ref, l_ref, acc_ref = refs
    c = pl.program_id(1)
    i = pl.program_id(2)
    g = c // (NSA_HPG // 2)
    q2 = _pair_queries(q_ref[0], g)
    koff = lax.broadcasted_iota(jnp.int32, (tq, tq), 0)
    qpos = i * tq + lax.broadcasted_iota(jnp.int32, (tq, tq), 1)
    if mode == "sel":
        sel_t = sel_ref[0, 0]

    def load_k(j):
        return k_ref[0, pl.ds(pl.multiple_of(j * tq, tq), tq), :].astype(BF16)

    def load_v(j):
        return vt_ref[0, 0, j]

    def mask_fn(j, diag):
        start = j * tq
        if mode == "sel":
            blk = lax.broadcasted_iota(jnp.int32, (tq, LANES), 1)
            expand_t = (blk == (start + lax.broadcasted_iota(jnp.int32, (tq, LANES), 0)) // SEL_BLOCK).astype(BF16)
            mask = _dot(expand_t, sel_t) > 0.5
            if diag:
                mask = mask & (start + koff <= qpos)
        else:
            mask = (start + koff <= qpos) & (qpos - (start + koff) < WINDOW)
        return jnp.concatenate([mask, mask], axis=1)

    j_lo = 0 if mode == "sel" else jnp.maximum(i - (WINDOW + tq - 1) // tq, 0)
    _flash_loop(q2, load_k, load_v, mask_fn, j_lo, i, m_ref, l_ref, acc_ref, transposed=True)
    o_t = acc_ref[...] / l_ref[...]
    o_ref[0] = _pair_outputs(jnp.concatenate([o_t[:, :tq].T, o_t[:, tq:].T], axis=0), g)


def _nsa_prompt_attn(q, kv, k_chunk, vt, sel=None):
    b, t, f = q.shape
    tq = min(TQ, t)
    assert vt.shape[-1] == tq
    cpg = NSA_HPG // 2
    in_specs = [pl.BlockSpec((1, tq, LANES), lambda b_, c, i: (b_, i, c)),
                pl.BlockSpec((1, t, LANES), lambda b_, c, i: (b_, 0, k_chunk)),
                _vt_spec(vt, lambda c: 0)]
    args = [q, kv, vt]
    if sel is not None:
        in_specs.append(pl.BlockSpec((1, 1, LANES, tq), lambda b_, c, i: (b_, c // cpg, 0, i)))
        args.append(sel)
    return pl.pallas_call(
        functools.partial(_nsa_prompt_body, tq=tq, mode="sel" if sel is not None else "win"),
        out_shape=jax.ShapeDtypeStruct((b, t, f), F32),
        grid=(b, f // LANES, t // tq),
        in_specs=in_specs,
        out_specs=pl.BlockSpec((1, tq, LANES), lambda b_, c, i: (b_, i, c)),
        scratch_shapes=_softmax_scratch_t(2 * tq, LANES),
        compiler_params=_cparams(("arbitrary", "arbitrary", "arbitrary")),
    )(*args)


def _out_nsa_body(x_ref, g_ref, oc_ref, os_ref, ow_ref, gl_ref, z_ref, e_ref, w_ref, lng_ref, lnb_ref, y_ref):
    sig = _sigmoid(gl_ref[0])
    o = (_split_dot(sig, e_ref[0]) * oc_ref[0] + _split_dot(sig, e_ref[1]) * os_ref[0]
         + _split_dot(sig, e_ref[2]) * ow_ref[0])
    _finish(o, z_ref, x_ref, g_ref, w_ref, lng_ref, lnb_ref, y_ref)


def _nsa_out_proj(x, gate, o_c, o_s, o_w, gl, z, w_out, ln_g, ln_b):
    d = x.shape[-1]
    e = np.zeros((3, GL_PAD, NSA_HEADS * NSA_HEAD_DIM), np.float32)
    for br in range(3):
        for h in range(NSA_HEADS):
            e[br, 3 * h + br, h * NSA_HEAD_DIM:(h + 1) * NSA_HEAD_DIM] = 1.0
    (y,) = _row_call(_out_nsa_body, x, [gate], [o_c, o_s, o_w, gl, z],
                     [jnp.asarray(e, BF16), w_out.astype(BF16), ln_g.reshape(1, d), ln_b.reshape(1, d)],
                     [(d, F32)], ROW_TILE)
    return y


def _row_token(m, n, rows_per_token, dt):
    return (lax.broadcasted_iota(jnp.int32, (m, n), 0) // rows_per_token) % dt


def _nsa_sample_cmp_body(pt_ref, q_ref, new_ref, ov_ref, wcat_ref, w1f_ref, pe_ref, b1_ref, w2_ref, b2_ref, *rest,
                         n_dense, pps, nsteps, dt, past, nblk):
    pages, (oc_ref, imp_ref, xk_ref, xv_ref) = rest[:pps], rest[pps:]
    c = pl.program_id(1)
    page = pages[0].shape[3]

    @pl.when(c == 0)
    def _():
        tail = xk_ref.shape[0] - past
        nw = new_ref[0]
        for x_ref, lo in ((xk_ref, 0), (xv_ref, LANES)):
            x_ref[pl.ds(past, tail), :] = jnp.zeros((tail, LANES), F32)
            x_ref[pl.ds(past, nw.shape[0]), :] = nw[:, lo:lo + LANES]

    for s in range(pps):
        pg = pages[s][0, 0]
        base = pl.multiple_of((c * pps + s) * page, page)
        xk_ref[pl.ds(base, page), :] = pg[:LANES].T
        xv_ref[pl.ds(base, page), :] = pg[LANES:].T

    @pl.when(c == nsteps - 1)
    def _():
        def get_rows(comp, p):
            return (xk_ref, xv_ref)[comp][pl.ds(p, nblk, stride=CMP_STRIDE), :]
        k, v = _compress(get_rows, nblk, wcat_ref, w1f_ref, pe_ref, b1_ref, w2_ref, b2_ref)
        q = q_ref[0]
        m = q.shape[0]
        s = _dot_nt(q, k.astype(BF16))
        qpos = past + _row_token(m, nblk, NSA_HPG, dt)
        cmp_end = lax.broadcasted_iota(jnp.int32, (m, nblk), 1) * CMP_STRIDE + (CMP_BLOCK - 1)
        p = _cmp_softmax(s, cmp_end <= qpos)
        oc_ref[0] = _dot(p.astype(BF16), v.astype(BF16))
        psum = jnp.sum(p.reshape(m // NSA_HPG, NSA_HPG, nblk), axis=1)
        imp_ref[0] = _split_dot(psum, ov_ref[...])


def _select_body(imp_ref, cur_ref, sel_ref, *, n_sel, k_eff):
    sel_ref[...] = _select_blocks(imp_ref[...], cur_ref[...], n_sel, k_eff)


def _nsa_sample_select(imp, past, dt, n_sel):
    b, r, spl = imp.shape
    sp = -(-n_sel // 8) * 8
    n = b * r
    npad = -(-n // LANES) * LANES
    imp_t = jnp.pad(imp.reshape(n, spl)[:, :sp].T, ((0, 0), (0, npad - n)))
    cur = jnp.tile((past + jnp.arange(dt, dtype=jnp.int32)) // SEL_BLOCK, n // dt)
    cur = jnp.pad(cur, (0, npad - n)).reshape(1, npad)
    sel_t = pl.pallas_call(
        functools.partial(_select_body, n_sel=n_sel, k_eff=min(N_SELECT, n_sel)),
        out_shape=jax.ShapeDtypeStruct((sp, npad), F32),
        compiler_params=pltpu.CompilerParams(vmem_limit_bytes=VMEM_LIMIT),
    )(imp_t, cur)
    return jnp.pad(sel_t[:, :n].T, ((0, 0), (0, spl - sp))).reshape(b, r, spl)


def _nsa_sample_sel_body(pt_ref, q_ref, sel_ref, new_ref, st_ref, neww_ref, *rest, n_dense, pps, nsteps, dt, past):
    pages, (os_ref, ow_ref, m_ref, l_ref, acc_ref, m2_ref, l2_ref, acc2_ref) = rest[:pps], rest[pps:]
    c = pl.program_id(1)
    q = q_ref[0]
    selb = sel_ref[0]
    m, spl = selb.shape
    page = pages[0].shape[3]

    def sel_tile(kv_t, base_pos, extra):
        tk = kv_t.shape[1]
        mask = _dot(selb, _block_expand(base_pos, spl, tk)) > 0.5
        if extra is not None:
            mask = mask & extra
        _softmax_tile(_dot(q, kv_t[:LANES].astype(BF16)), mask, kv_t[LANES:].astype(BF16), m_ref, l_ref, acc_ref, True)

    @pl.when(c == 0)
    def _():
        _softmax_init(m_ref, l_ref, acc_ref)
        _softmax_init(m2_ref, l2_ref, acc2_ref)
        tk = new_ref.shape[2]
        new_mask = _new_tile_mask(m, tk, NSA_HPG, dt, True)
        sel_tile(new_ref[0], past, new_mask)
        st = st_ref[0]
        wlen = st.shape[1]
        diff = (past + _row_token(m, wlen, NSA_HPG, dt)) - (past - wlen + lax.broadcasted_iota(jnp.int32, (m, wlen), 1))
        kpos_ok = (past - wlen + lax.broadcasted_iota(jnp.int32, (m, wlen), 1)) >= 0
        _softmax_tile(_dot(q, st[:LANES].astype(BF16)), (diff >= 0) & (diff < WINDOW) & kpos_ok,
                      st[LANES:].astype(BF16), m2_ref, l2_ref, acc2_ref, True)
        nw = neww_ref[0]
        diff_n = _row_token(m, tk, NSA_HPG, dt) - lax.broadcasted_iota(jnp.int32, (m, tk), 1)
        _softmax_tile(_dot(q, nw[:LANES].astype(BF16)), new_mask & (diff_n < WINDOW),
                      nw[LANES:].astype(BF16), m2_ref, l2_ref, acc2_ref, True)

    pgs = [pages[s][0, 0] for s in range(pps)]
    mask = _dot(selb, _block_expand(c * (pps * page), spl, pps * page)) > 0.5
    _softmax_pages(q, [pg[:LANES].astype(BF16) for pg in pgs], [pg[LANES:].astype(BF16) for pg in pgs], mask,
                   m_ref, l_ref, acc_ref)

    @pl.when(c == nsteps - 1)
    def _():
        os_ref[0] = acc_ref[...] / l_ref[...]
        ow_ref[0] = acc2_ref[...] / l2_ref[...]


def _nsa_sample_attn(q, new_kv, new_w, cache, win_state, layer, page_table, cw):
    b, dt, _ = q.shape
    g = NSA_KV_GROUPS
    npg, page = page_table.shape[1], cache.shape[3]
    past = npg * page
    tk_all = past + dt
    n_chunks = -(-tk_all // CMP_STRIDE)
    nblk = -(-n_chunks // LANES) * LANES
    n_sel = -(-tk_all // SEL_BLOCK)
    spl = -(-n_sel // LANES) * LANES
    qg = _group_block_diag(q, g)
    m = qg.shape[1]
    ov = _overlap_t(spl, nblk, n_sel, n_chunks - 1).T
    new_cmp = jnp.pad(new_kv[:, :, :2 * LANES], ((0, 0), (0, CMP_STRIDE - dt), (0, 0)))
    o_c, imp = _paged_call(
        functools.partial(_nsa_sample_cmp_body, dt=dt, past=past, nblk=nblk), page_table, cache, layer, (2 * LANES, 0),
        [qg, new_cmp], [(m, LANES, F32), (m // NSA_HPG, spl, F32)],
        [pltpu.VMEM((nblk * CMP_STRIDE, LANES), F32)] * 2, shared=[ov] + list(cw))
    sel = _nsa_sample_select(imp, past, dt, n_sel)
    sel_rows = jnp.repeat(sel, NSA_HPG, axis=1).astype(BF16)
    o_s, o_w = _paged_call(
        functools.partial(_nsa_sample_sel_body, dt=dt, past=past), page_table, cache, layer, (2 * LANES, 1),
        [qg, sel_rows, _as_page(new_kv[:, :, 2 * LANES:], page), win_state, _as_page(new_w, page)],
        [(m, LANES, F32), (m, LANES, F32)], _softmax_scratch(m, LANES) + _softmax_scratch(m, LANES))
    return tuple(_group_diag_extract(o, g, dt) for o in (o_c, o_s, o_w))


def _nsa_layer(xp, xs, mod_p, mod_s, pos_p, pos_s, w_in, pe, w1, b1, w2, b2, w_out, ln_g, ln_b,
               cache, win_state, layer, page_table):
    db, dt, d = xs.shape
    g, hd = NSA_KV_GROUPS, NSA_HEAD_DIM
    sh_p, sc_p, gt_p = _split_mod(mod_p)
    sh_s, sc_s, gt_s = _split_mod(mod_s, dt)
    xs_flat = xs.reshape(1, db * dt, d)
    tab_p = _rope_tables(pos_p, hd // 4, ROPE_THETA, hd)
    tab_s = _rope_tables(jnp.tile(pos_s, db), hd // 4, ROPE_THETA, hd)
    cw = _cmp_weights(pe, w1, b1, w2, b2)
    q_p, kv_p, w_p, gl_p, z_p, vst_p, vwt_p = _nsa_proj(xp, sh_p, sc_p, tab_p, w_in)
    q_s, kv_s, w_s, gl_s, z_s, _, _ = _nsa_proj(xs_flat, sh_s, sc_s, tab_s, w_in)
    kcmp, vcmp = _nsa_compress(kv_p, cw)
    oc_p, sel = _nsa_cmpsel(q_p, kcmp, vcmp)
    os_p = _nsa_prompt_attn(q_p, kv_p, 2, vst_p, sel)
    ow_p = _nsa_prompt_attn(q_p, w_p, 0, vwt_p)
    yp = _nsa_out_proj(xp, gt_p, oc_p, os_p, ow_p, gl_p, z_p, w_out, ln_g, ln_b)
    kv_s, w_s = kv_s.reshape(db, dt, -1), w_s.reshape(db, dt, -1)
    state = _page_view(win_state)[layer]
    oc_s, os_s, ow_s = _nsa_sample_attn(q_s.reshape(db, dt, -1), kv_s, w_s, cache, state, layer, page_table, cw)
    flat = lambda a: a.reshape(1, db * dt, -1)
    ys = _nsa_out_proj(xs_flat, gt_s, flat(oc_s), flat(os_s), flat(ow_s), gl_s, z_s, w_out, ln_g, ln_b)
    t = xp.shape[1]
    rows_p = kv_p.reshape(kv_p.shape[:2] + (4, g, hd))
    rows_s = kv_s.reshape(db, dt, 4, g, hd)
    win_p = w_p[:, t - min(WINDOW, t):].reshape(xp.shape[0], min(WINDOW, t), 2, g, hd)
    wrows = jnp.concatenate([win_state[layer], w_s.reshape(db, dt, 2, g, hd)], axis=1)
    win_s = wrows[:, wrows.shape[1] - min(WINDOW, wrows.shape[1]):]
    return yp, ys.reshape(db, dt, d), rows_p, rows_s, win_p, win_s


def kernel(x_prompt, x_sample, cache_sb_kv, cache_mla_latent, cache_diff_kv, cache_nsa_kv, state_nsa_window,
           page_table, c_prompt, c_sample, ada_w, ada_b, ln_g, ln_b, sb_w_in, sb_w_out,
           mla_w_in, mla_q_norm, mla_w_q_up, mla_kv_norm, mla_w_uk, mla_w_uv, mla_w_out,
           diff_w_in, diff_lam, diff_subln, diff_w_out,
           nsa_w_in, nsa_cmp_pe, nsa_cmp_w1, nsa_cmp_b1, nsa_cmp_w2, nsa_cmp_b2, nsa_w_out):
    nb = x_prompt.shape[0]
    mods = _adaln(jnp.concatenate([c_prompt, c_sample], axis=0), ada_w, ada_b)
    past = page_table.shape[1] * cache_sb_kv.shape[2]
    pos_p = jnp.arange(x_prompt.shape[1], dtype=jnp.int32)
    pos_s = past + jnp.arange(x_sample.shape[1], dtype=jnp.int32)
    xp, xs, sb_p, sb_s = _sb_layer(x_prompt, x_sample, mods[0, :nb], mods[0, nb:], sb_w_in[0], sb_w_out[0],
                                   ln_g[0], ln_b[0], _page_view(cache_sb_kv), 0, page_table)
    xp, xs, mla_p, mla_s = _mla_layer(xp, xs, mods[1, :nb], mods[1, nb:], pos_p, pos_s, mla_w_in[0], mla_q_norm[0],
                                      mla_w_q_up[0], mla_kv_norm[0], mla_w_uk[0], mla_w_uv[0], mla_w_out[0],
                                      ln_g[1], ln_b[1], _page_view(cache_mla_latent), 0, page_table)
    xp, xs, diff_p, diff_s = _diff_layer(xp, xs, mods[2, :nb], mods[2, nb:], pos_p, pos_s, 2, diff_w_in[0],
                                         diff_lam[0], diff_subln[0], diff_w_out[0], ln_g[2], ln_b[2],
                                         cache_diff_kv, 0, page_table)
    xp, xs, nsa_p, nsa_s, win_p, win_s = _nsa_layer(
        xp, xs, mods[3, :nb], mods[3, nb:], pos_p, pos_s, nsa_w_in[0], nsa_cmp_pe[0], nsa_cmp_w1[0], nsa_cmp_b1[0],
        nsa_cmp_w2[0], nsa_cmp_b2[0], nsa_w_out[0], ln_g[3], ln_b[3], _page_view(cache_nsa_kv), state_nsa_window,
        0, page_table)
    return (xp, xs, sb_p[None], sb_s[None], mla_p[None], mla_s[None], diff_p[None], diff_s[None],
            nsa_p[None], nsa_s[None], win_p[None], win_s[None])
```

```python
import functools
import math

import numpy as np
import jax
import jax.numpy as jnp
from jax import lax
from jax.experimental import pallas as pl
from jax.experimental.pallas import tpu as pltpu

F32 = jnp.float32
BF16 = jnp.bfloat16

DEPTH = 4
ALPHA = (2 * DEPTH) ** 0.25
NORM_EPS = 1e-5
NEG_INF = -1e30
FORCE_SCORE = 1e9
ROPE_THETA = 500000.0

SB_HEADS, SB_KV_HEADS, SB_HEAD_DIM = 16, 4, 64
MLA_HEADS, MLA_Q_RANK, MLA_KV_RANK = 16, 256, 128
MLA_NOPE_DIM, MLA_ROPE_DIM, MLA_V_DIM = 64, 32, 64
MLA_ROPE_THETA = 10000.0
DIFF_HEADS, DIFF_KV_HEADS, DIFF_HEAD_DIM = 8, 2, 64
NSA_HEADS, NSA_KV_GROUPS, NSA_HEAD_DIM = 16, 2, 64
CMP_BLOCK, CMP_STRIDE, CMP_HIDDEN = 32, 16, 128
SEL_BLOCK, N_SELECT, WINDOW = 64, 16, 512

LANES = 128
VMEM_LIMIT = 48 * 1024 * 1024
PAGES_PER_STEP = 32
TQ = 256
ROW_TILE = 256


def _cparams(sem):
    return pltpu.CompilerParams(dimension_semantics=sem, vmem_limit_bytes=VMEM_LIMIT)


def _dot(a, b):
    return jnp.dot(a, b, preferred_element_type=F32)


def _dot_nt(a, b):
    return lax.dot_general(a, b, (((1,), (1,)), ((), ())), preferred_element_type=F32)


def _silu(x):
    return x / (1.0 + jnp.exp(-x))


def _sigmoid(x):
    return 1.0 / (1.0 + jnp.exp(-x))


def _split_dot(x, w_bf16):
    hi = x.astype(BF16)
    lo = (x - hi.astype(F32)).astype(BF16)
    return _dot(hi, w_bf16) + _dot(lo, w_bf16)


def _rope128(x, c, s1, s2, half):
    return x * c + pltpu.roll(x, LANES - half, 1) * s1 + pltpu.roll(x, half, 1) * s2


def _rope_tables(pos, rot_dim, theta, period):
    half = rot_dim // 2
    inv = theta ** (-jnp.arange(half, dtype=F32) / half)
    ang = pos.astype(F32)[:, None] * inv
    cos, sin = jnp.cos(ang), jnp.sin(ang)
    n = pos.shape[0]
    ones = jnp.ones((n, period - rot_dim), F32)
    zeros_h = jnp.zeros((n, half), F32)
    zeros_r = jnp.zeros((n, period - rot_dim), F32)
    c = jnp.concatenate([cos, cos, ones], axis=1)
    s1 = jnp.concatenate([-sin, zeros_h, zeros_r], axis=1)
    s2 = jnp.concatenate([zeros_h, sin, zeros_r], axis=1)
    rep = LANES // period
    return tuple(jnp.tile(t, (1, rep))[None] for t in (c, s1, s2))


def _adaln_body(c_ref, w_ref, b_ref, o_ref):
    s = _silu(c_ref[...]).astype(BF16)
    o_ref[0] = _dot(s, w_ref[0].astype(BF16)) + b_ref[0]


def _adaln(c_all, ada_w, ada_b):
    depth, d, d3 = ada_w.shape
    n = c_all.shape[0]
    nt = d3 // d
    return pl.pallas_call(
        _adaln_body,
        out_shape=jax.ShapeDtypeStruct((depth, n, d3), F32),
        grid=(depth, nt),
        in_specs=[pl.BlockSpec((n, d), lambda l, j: (0, 0)),
                  pl.BlockSpec((1, d, d), lambda l, j: (l, 0, j)),
                  pl.BlockSpec((1, 1, d), lambda l, j: (l, 0, j))],
        out_specs=pl.BlockSpec((1, n, d), lambda l, j: (l, 0, j)),
        compiler_params=_cparams(("arbitrary", "arbitrary")),
    )(c_all, ada_w, ada_b.reshape(depth, 1, d3))


def _row_call(body, x, mods, row_inputs, full_inputs, out_defs, tm):
    bt, r, _ = x.shape
    tm = min(tm, r)
    grid = (bt, r // tm)

    def tiled(a):
        lead = a.shape[0]
        if a.shape[1] == 1:
            return pl.BlockSpec((1, 1, a.shape[2]), lambda b, i: (b, 0, 0))
        if lead == 1 and bt > 1:
            return pl.BlockSpec((1, tm, a.shape[2]), lambda b, i: (0, i, 0))
        return pl.BlockSpec((1, tm, a.shape[2]), lambda b, i: (b, i, 0))

    def full(a):
        nd = a.ndim
        return pl.BlockSpec(a.shape, lambda b, i: (0,) * nd)

    in_specs = [tiled(x)] + [tiled(m) for m in mods] + [tiled(t) for t in row_inputs] + [full(w) for w in full_inputs]
    out_shape, out_specs = [], []
    for w, dt in out_defs:
        if w == "t":
            out_shape.append(jax.ShapeDtypeStruct((bt, dt, r // tm, LANES, tm), BF16))
            out_specs.append(pl.BlockSpec((1, dt, 1, LANES, tm), lambda b, i: (b, 0, i, 0, 0)))
        else:
            out_shape.append(jax.ShapeDtypeStruct((bt, r, w), dt))
            out_specs.append(pl.BlockSpec((1, tm, w), lambda b, i: (b, i, 0)))
    return pl.pallas_call(
        body, out_shape=out_shape, grid=grid, in_specs=in_specs, out_specs=out_specs,
        compiler_params=_cparams(("arbitrary", "arbitrary")),
    )(x, *mods, *row_inputs, *full_inputs)


def _modulated(x_ref, sh_ref, sc_ref):
    return (x_ref[0] * (1.0 + sc_ref[0]) + sh_ref[0]).astype(BF16)


def _store_transposed(t_ref, x):
    for ch in range(x.shape[1] // LANES):
        t_ref[0, ch, 0] = x[:, ch * LANES:(ch + 1) * LANES].T.astype(BF16)


def _finish(o, z_ref, x_ref, g_ref, w_ref, lng_ref, lnb_ref, y_ref):
    gated = (o * _silu(z_ref[0])).astype(BF16)
    y = _dot(gated, w_ref[...])
    r = ALPHA * x_ref[0] + (1.0 + g_ref[0]) * y
    mu = jnp.mean(r, axis=-1, keepdims=True)
    d = r - mu
    var = jnp.mean(d * d, axis=-1, keepdims=True)
    y_ref[0] = d * lax.rsqrt(var + NORM_EPS) * lng_ref[...] + lnb_ref[...]


def _out_plain_body(x_ref, g_ref, o_ref, z_ref, w_ref, lng_ref, lnb_ref, y_ref):
    _finish(o_ref[0].astype(F32), z_ref, x_ref, g_ref, w_ref, lng_ref, lnb_ref, y_ref)


def _out_proj(x, gate, o, z, w_out, ln_g, ln_b):
    d = x.shape[-1]
    (y,) = _row_call(_out_plain_body, x, [gate], [o, z], [w_out.astype(BF16), ln_g.reshape(1, d), ln_b.reshape(1, d)],
                     [(d, F32)], ROW_TILE)
    return y


def _sb_proj_body(x_ref, sh_ref, sc_ref, wq_ref, wkv_ref, wz_ref, q_ref, kv_ref, z_ref, vt_ref):
    h = _modulated(x_ref, sh_ref, sc_ref)
    q_ref[0] = (_dot(h, wq_ref[...]) * (SB_HEAD_DIM ** -0.5 * LOG2E)).astype(BF16)
    kv = _dot(h, wkv_ref[...])
    kv_ref[0] = kv
    _store_transposed(vt_ref, kv[:, kv.shape[1] // 2:])
    z_ref[0] = _dot(h, wz_ref[...])


def _sb_proj(x, sh, sc, w_in):
    nq = SB_HEADS * SB_HEAD_DIM
    nkv = 2 * SB_KV_HEADS * SB_HEAD_DIM
    w = w_in.astype(BF16)
    return _row_call(_sb_proj_body, x, [sh, sc], [], [w[:, :nq], w[:, nq:nq + nkv], w[:, nq + nkv:]],
                     [(nq, BF16), (nkv, F32), (nq, F32), ("t", nkv // (2 * LANES))], ROW_TILE)


LOG2E = math.log2(math.e)


def _softplus2(t):
    return jnp.maximum(t, 0.0) + jnp.log2(1.0 + jnp.exp2(-jnp.abs(t)))


def _sb_tile(q2, k, v, mask, tri, r_prev, keys_on_lanes=False):
    z = _dot(q2, k) if keys_on_lanes else _dot_nt(q2, k)
    sp = _softplus2(z)
    u = sp if mask is None else jnp.where(mask, sp, 0.0)
    log_rem = _split_dot(u, tri) + r_prev
    a = jnp.exp2(z - sp - log_rem)
    if mask is not None:
        a = jnp.where(mask, a, 0.0)
    a = a.astype(BF16)
    pv = _dot_nt(a, v) if keys_on_lanes else _dot(a, v)
    return pv, r_prev + jnp.sum(u, axis=-1, keepdims=True)


def _tri(tk):
    return (lax.broadcasted_iota(jnp.int32, (tk, tk), 0) > lax.broadcasted_iota(jnp.int32, (tk, tk), 1)).astype(BF16)


def _sb_prompt_body(q_ref, k_ref, vt_ref, o_ref, acc_ref, *, tq):
    c = pl.program_id(1)
    i = pl.program_id(2)
    half = (c // 2) % 2
    lane = lax.broadcasted_iota(jnp.int32, (tq, LANES), 1)
    in_half = (lane // SB_HEAD_DIM) == half
    q = q_ref[0].astype(F32)
    qr = pltpu.roll(q, SB_HEAD_DIM, 1)
    first = half == 0
    qa = jnp.where(in_half, jnp.where(first, q, qr), 0.0)
    qb = jnp.where(in_half, jnp.where(first, qr, q), 0.0)
    q2 = jnp.concatenate([qa, qb], axis=0).astype(BF16)
    row = lax.broadcasted_iota(jnp.int32, (tq, tq), 0)
    col = lax.broadcasted_iota(jnp.int32, (tq, tq), 1)
    tri_t = (col > row).astype(BF16)
    diag_mask = jnp.concatenate([row < col] * 2, axis=1)
    acc_ref[...] = jnp.zeros_like(acc_ref)

    def scores(j):
        k = k_ref[0, pl.ds(pl.multiple_of(j * tq, tq), tq), :].astype(BF16)
        return _dot_nt(k, q2)

    def weights(z, mask, r_prev):
        sp = _softplus2(z)
        u = sp if mask is None else jnp.where(mask, sp, 0.0)
        cum = _dot(tri_t, u.astype(BF16))
        a = jnp.exp2(z - sp - (cum + r_prev))
        if mask is not None:
            a = jnp.where(mask, a, 0.0)
        return a.astype(BF16), r_prev + jnp.sum(u, axis=0, keepdims=True)

    a_prev, r = weights(scores(i), diag_mask, jnp.zeros((1, 2 * tq), F32))
    z = scores(jnp.maximum(i - 1, 0))

    def step(jj, carry):
        z, a_prev, r = carry
        j = i - jj
        acc_ref[...] += _dot(vt_ref[0, 0, j + 1], a_prev)
        z_next = scores(jnp.maximum(j - 1, 0))
        a, r = weights(z, None, r)
        return z_next, a, r

    _, a_prev, _ = lax.fori_loop(1, i + 1, step, (z, a_prev, r))
    acc_t = acc_ref[...] + _dot(vt_ref[0, 0, 0], a_prev)
    oa, ob = acc_t[:, :tq].T, acc_t[:, tq:].T
    oar, obr = pltpu.roll(oa, SB_HEAD_DIM, 1), pltpu.roll(ob, SB_HEAD_DIM, 1)
    low = lane < SB_HEAD_DIM
    o_ref[0] = jnp.where(low, jnp.where(first, oa, oar), jnp.where(first, obr, ob))


def _vt_spec(vt, chunk_of):
    return pl.BlockSpec((1, 1) + vt.shape[2:], lambda b_, c, i: (b_, chunk_of(c), 0, 0, 0))


def _sb_prompt_attn(q, kv, vt):
    b, t, _ = q.shape
    tq = min(TQ, t)
    assert vt.shape[-1] == tq
    nchunk = SB_HEADS * SB_HEAD_DIM // LANES
    return pl.pallas_call(
        functools.partial(_sb_prompt_body, tq=tq),
        out_shape=jax.ShapeDtypeStruct((b, t, nchunk * LANES), F32),
        grid=(b, nchunk, t // tq),
        in_specs=[pl.BlockSpec((1, tq, LANES), lambda b_, c, i: (b_, i, c)),
                  pl.BlockSpec((1, t, LANES), lambda b_, c, i: (b_, 0, c // 4)),
                  _vt_spec(vt, lambda c: c // 4)],
        out_specs=pl.BlockSpec((1, tq, LANES), lambda b_, c, i: (b_, i, c)),
        scratch_shapes=[pltpu.VMEM((LANES, 2 * tq), F32)],
        compiler_params=_cparams(("arbitrary", "arbitrary", "arbitrary")),
    )(q, kv, vt)


def _paged_call(body, page_table, cache, layer, row_block, dense, out_defs, scratch, descending=False, shared=()):
    nb, npg = page_table.shape
    pps = min(PAGES_PER_STEP, npg)
    nsteps = npg // pps
    page = cache.shape[3]
    rb, ridx = row_block

    def page_spec(s):
        def imap(b, c, pt):
            p = c * pps + s
            if descending:
                p = npg - 1 - p
            return (layer, pt[b, p], ridx, 0)
        return pl.BlockSpec((1, 1, rb, page), imap)

    def shared_spec(a):
        nd = a.ndim
        return pl.BlockSpec(a.shape, lambda b, c, pt: (0,) * nd)

    in_specs = [pl.BlockSpec((1,) + a.shape[1:], lambda b, c, pt: (b, 0, 0)) for a in dense]
    in_specs += [shared_spec(a) for a in shared]
    in_specs += [page_spec(s) for s in range(pps)]
    out_shape = [jax.ShapeDtypeStruct((nb, r, w), dt) for r, w, dt in out_defs]
    out_specs = [pl.BlockSpec((1, r, w), lambda b, c, pt: (b, 0, 0)) for r, w, _ in out_defs]
    grid_spec = pltpu.PrefetchScalarGridSpec(num_scalar_prefetch=1, grid=(nb, nsteps), in_specs=in_specs,
                                             out_specs=out_specs, scratch_shapes=scratch)
    return pl.pallas_call(
        functools.partial(body, n_dense=len(dense) + len(shared), pps=pps, nsteps=nsteps),
        out_shape=out_shape, grid_spec=grid_spec,
        compiler_params=_cparams(("arbitrary", "arbitrary")),
    )(page_table, *dense, *shared, *([cache] * pps))


def _as_page(rows, page):
    return jnp.pad(rows.transpose(0, 2, 1), ((0, 0), (0, 0), (0, page - rows.shape[1])))


def _sb_sample_body(pt_ref, q_ref, new_ref, *rest, n_dense, pps, nsteps, dt):
    pages, (o_ref, acc_ref, r_ref) = rest[:pps], rest[pps:]
    c = pl.program_id(1)
    nk = SB_KV_HEADS * SB_HEAD_DIM
    q2 = q_ref[0]
    m, tk = q2.shape[0], new_ref.shape[2]
    tri = _tri(tk)

    @pl.when(c == 0)
    def _():
        row_t = (lax.broadcasted_iota(jnp.int32, (m, tk), 0) // (SB_HEADS // SB_KV_HEADS)) % dt
        key = lax.broadcasted_iota(jnp.int32, (m, tk), 1)
        mask = (key < row_t) & (key < dt)
        nw = new_ref[0]
        pv, r_new = _sb_tile(q2, nw[:nk].astype(BF16), nw[nk:].astype(BF16), mask, tri, jnp.zeros((m, 1), F32), True)
        acc_ref[...] = pv
        r_ref[...] = r_new

    pgs = [pages[s][0, 0] for s in range(pps)]
    zs = [_dot(q2, pg[:nk].astype(BF16)) for pg in pgs]
    sps = [_softplus2(z) for z in zs]
    cums = _dot(jnp.concatenate(sps, axis=0).astype(BF16), tri)
    r = r_ref[...]
    pv = jnp.zeros(acc_ref.shape, F32)
    for s in range(pps):
        a = jnp.exp2(zs[s] - sps[s] - (cums[s * m:(s + 1) * m] + r))
        pv = pv + _dot_nt(a.astype(BF16), pgs[s][nk:].astype(BF16))
        r = r + jnp.sum(sps[s], axis=-1, keepdims=True)
    acc_ref[...] += pv
    r_ref[...] = r

    @pl.when(c == nsteps - 1)
    def _():
        o_ref[0] = acc_ref[...]


def _group_block_diag(q, groups):
    b, t, f = q.shape
    d = SB_HEAD_DIM
    n = f // (groups * d)
    q5 = q.reshape(b, t, groups, n, d)
    eye = jnp.eye(groups, dtype=q.dtype)
    out = q5.transpose(0, 2, 1, 3, 4)[:, :, :, :, None, :] * eye[None, :, None, None, :, None]
    return out.reshape(b, groups * t * n, groups * d)


def _group_diag_extract(acc, groups, t):
    b, r, f = acc.shape
    w = f // groups
    n = r // (groups * t)
    a5 = acc.reshape(b, groups, t * n, groups, w)
    picked = jnp.stack([a5[:, g, :, g, :] for g in range(groups)], axis=1)
    return picked.reshape(b, groups, t, n, w).transpose(0, 2, 1, 3, 4).reshape(b, t, groups * n * w)


def _sb_sample_attn(q, new_kv, cache, layer, page_table):
    b, dt, _ = q.shape
    page = cache.shape[3]
    qbd = _group_block_diag(q, SB_KV_HEADS)
    m = qbd.shape[1]
    nk = SB_KV_HEADS * SB_HEAD_DIM
    (acc,) = _paged_call(
        functools.partial(_sb_sample_body, dt=dt), page_table, cache, layer, (2 * nk, 0),
        [qbd, _as_page(new_kv, page)], [(m, nk, F32)],
        [pltpu.VMEM((m, nk), F32), pltpu.VMEM((m, 1), F32)], descending=True)
    return _group_diag_extract(acc, SB_KV_HEADS, dt)


def _page_view(cache):
    nl, pool, page = cache.shape[:3]
    feat = math.prod(cache.shape[3:])
    return jnp.moveaxis(cache.reshape(nl, pool, page, feat), 2, 3)


def _split_mod(mod, per_row_t=None):
    d = mod.shape[-1] // 3
    parts = [mod[:, k * d:(k + 1) * d] for k in range(3)]
    if per_row_t is None:
        return [p[:, None, :] for p in parts]
    return [jnp.repeat(p, per_row_t, axis=0)[None] for p in parts]


def _sb_layer(xp, xs, mod_p, mod_s, w_in, w_out, ln_g, ln_b, cache, layer, page_table):
    db, dt, d = xs.shape
    sh_p, sc_p, gt_p = _split_mod(mod_p)
    sh_s, sc_s, gt_s = _split_mod(mod_s, dt)
    xs_flat = xs.reshape(1, db * dt, d)
    qp, kvp, zp, vtp = _sb_proj(xp, sh_p, sc_p, w_in)
    qs, kvs, zs, _ = _sb_proj(xs_flat, sh_s, sc_s, w_in)
    op = _sb_prompt_attn(qp, kvp, vtp)
    os_ = _sb_sample_attn(qs.reshape(db, dt, -1), kvs.reshape(db, dt, -1), cache, layer, page_table)
    yp = _out_proj(xp, gt_p, op, zp, w_out, ln_g, ln_b)
    ys = _out_proj(xs_flat, gt_s, os_.reshape(1, db * dt, -1), zs, w_out, ln_g, ln_b)
    rows_shape = (2, SB_KV_HEADS, SB_HEAD_DIM)
    return (yp, ys.reshape(db, dt, d), kvp.reshape(kvp.shape[:2] + rows_shape), kvs.reshape((db, dt) + rows_shape))


def _softmax_tile(s, mask, v, m_ref, l_ref, acc_ref, keys_on_lanes=False):
    if mask is not None:
        s = jnp.where(mask, s, NEG_INF)
    m_prev = m_ref[...]
    m_new = jnp.maximum(m_prev, jnp.max(s, axis=-1, keepdims=True))
    p = jnp.exp(s - m_new)
    if mask is not None:
        p = jnp.where(mask, p, 0.0)
    scale = jnp.exp(m_prev - m_new)
    pb = p.astype(BF16)
    pv = _dot_nt(pb, v) if keys_on_lanes else _dot(pb, v)
    l_ref[...] = scale * l_ref[...] + jnp.sum(p, axis=-1, keepdims=True)
    acc_ref[...] = scale * acc_ref[...] + pv
    m_ref[...] = m_new


def _softmax_init(m_ref, l_ref, acc_ref):
    m_ref[...] = jnp.full_like(m_ref, NEG_INF)
    l_ref[...] = jnp.zeros_like(l_ref)
    acc_ref[...] = jnp.zeros_like(acc_ref)


def _softmax_scratch(m, f):
    return [pltpu.VMEM((m, 1), F32), pltpu.VMEM((m, 1), F32), pltpu.VMEM((m, f), F32)]


def _softmax_pages(q, kts, vts, mask, m_ref, l_ref, acc_ref):
    s = _dot(q, jnp.concatenate(kts, axis=1))
    p, scale = _softmax_stage(s, mask, m_ref, l_ref)
    acc_ref[...] = scale * acc_ref[...] + _dot_nt(p, jnp.concatenate(vts, axis=1))


def _rms(x, g):
    return x * lax.rsqrt(jnp.mean(x * x, axis=-1, keepdims=True) + NORM_EPS) * g


def _mla_proj_body(x_ref, sh_ref, sc_ref, c_ref, s1_ref, s2_ref, wcq_ref, wckv_ref, wpe_ref, wz_ref, g_ref,
                   cq_ref, rows_ref, kk_ref, z_ref, latt_ref):
    h = _modulated(x_ref, sh_ref, sc_ref)
    cq_ref[0] = _dot(h, wcq_ref[...])
    lat = _rms(_dot(h, wckv_ref[...]), g_ref[...])
    pe = _rope128(_dot(h, wpe_ref[...]), c_ref[0], s1_ref[0], s2_ref[0], MLA_ROPE_DIM // 2)
    rows_ref[0, :, :MLA_KV_RANK] = lat
    rows_ref[0, :, MLA_KV_RANK:] = pe[:, :MLA_ROPE_DIM]
    kk_ref[0] = jnp.concatenate([lat, pe], axis=1).astype(BF16)
    _store_transposed(latt_ref, lat)
    z_ref[0] = _dot(h, wz_ref[...])


def _mla_proj(x, sh, sc, tables, w_in, kv_norm):
    r, pe = MLA_KV_RANK, MLA_ROPE_DIM
    w = w_in.astype(BF16)
    n0, n1, n2 = MLA_Q_RANK, MLA_Q_RANK + r, MLA_Q_RANK + r + pe
    wpe = jnp.pad(w[:, n1:n2], ((0, 0), (0, LANES - pe)))
    nz = w.shape[1] - n2
    return _row_call(_mla_proj_body, x, [sh, sc], list(tables),
                     [w[:, :n0], w[:, n0:n1], wpe, w[:, n2:], kv_norm.reshape(1, r)],
                     [(n0, F32), (r + pe, F32), (2 * LANES, BF16), (nz, F32), ("t", r // LANES)], ROW_TILE)


def _mla_q_body(cq_ref, c_ref, s1_ref, s2_ref, g_ref, wn_ref, wuk_ref, wp_ref, q_ref):
    nb = _rms(cq_ref[0], g_ref[...]).astype(BF16)
    scale = (MLA_NOPE_DIM + MLA_ROPE_DIM) ** -0.5
    c, s1, s2 = c_ref[0], s1_ref[0], s2_ref[0]
    for h in range(MLA_HEADS):
        nope = _dot(nb, wn_ref[h]).astype(BF16)
        q_ref[0, :, 2 * h * LANES:(2 * h + 1) * LANES] = (_dot(nope, wuk_ref[h]) * scale).astype(BF16)
        pe = _rope128(_dot(nb, wp_ref[h]), c, s1, s2, MLA_ROPE_DIM // 2)
        q_ref[0, :, (2 * h + 1) * LANES:(2 * h + 2) * LANES] = (pe * scale).astype(BF16)


def _mla_q(cq, tables, q_norm, w_q_up, w_uk):
    hds, dn, dr = MLA_HEADS, MLA_NOPE_DIM, MLA_ROPE_DIM
    wq = w_q_up.astype(BF16).reshape(MLA_Q_RANK, hds, dn + dr)
    wn = wq[:, :, :dn].transpose(1, 0, 2)
    wp = jnp.pad(wq[:, :, dn:].transpose(1, 0, 2), ((0, 0), (0, 0), (0, LANES - dr)))
    wuk = w_uk.astype(BF16).transpose(1, 2, 0)
    (q,) = _row_call(_mla_q_body, cq, [], list(tables), [q_norm.reshape(1, MLA_Q_RANK), wn, wuk, wp],
                     [(hds * 2 * LANES, BF16)], ROW_TILE)
    return q


def _causal_mask(i, j, tq, rows):
    qpos = i * tq + lax.broadcasted_iota(jnp.int32, (tq, tq), 0)
    kpos = j * tq + lax.broadcasted_iota(jnp.int32, (tq, tq), 1)
    mask = kpos <= qpos
    return mask if rows == 1 else jnp.concatenate([mask] * rows, axis=0)


def _softmax_stage(s, mask, m_ref, l_ref, axis=-1):
    additive = mask is not None and mask.dtype == F32
    if additive:
        s = s + mask
    elif mask is not None:
        s = jnp.where(mask, s, NEG_INF)
    m_prev = m_ref[...]
    m_new = jnp.maximum(m_prev, jnp.max(s, axis=axis, keepdims=True))
    p = jnp.exp(s - m_new)
    if mask is not None and not additive:
        p = jnp.where(mask, p, 0.0)
    scale = jnp.exp(m_prev - m_new)
    l_ref[...] = scale * l_ref[...] + jnp.sum(p, axis=axis, keepdims=True)
    m_ref[...] = m_new
    return p.astype(BF16), scale


def _flash_loop(q, load_k, load_v, mask_fn, j_lo, i, m_ref, l_ref, acc_ref, transposed=False):
    _softmax_init(m_ref, l_ref, acc_ref)
    axis = 0 if transposed else -1

    def scores(j):
        return _dot_nt(load_k(j), q) if transposed else _dot_nt(q, load_k(j))

    def pv_update(p_prev, scale_prev, j_prev):
        pv = _dot(load_v(j_prev), p_prev) if transposed else _dot(p_prev, load_v(j_prev))
        acc_ref[...] = scale_prev * acc_ref[...] + pv

    def step(j, carry):
        s, p_prev, scale_prev = carry
        pv_update(p_prev, scale_prev, jnp.maximum(j - 1, j_lo))
        s_next = scores(j + 1)
        p, scale = _softmax_stage(s, mask_fn(j, False), m_ref, l_ref, axis)
        return s_next, p, scale

    s0 = scores(j_lo)
    init = (s0, jnp.zeros(s0.shape, BF16), jnp.ones(m_ref.shape, F32))
    s, p_prev, scale_prev = lax.fori_loop(j_lo, i, step, init)
    pv_update(p_prev, scale_prev, jnp.maximum(i - 1, j_lo))
    p, scale = _softmax_stage(s, mask_fn(i, True), m_ref, l_ref, axis)
    pv_update(p, scale, i)


def _softmax_scratch_t(m, f):
    return [pltpu.VMEM((1, m), F32), pltpu.VMEM((1, m), F32), pltpu.VMEM((f, m), F32)]


def _causal_mask_t(i, j, tq, cols):
    kpos = j * tq + lax.broadcasted_iota(jnp.int32, (tq, tq), 0)
    qpos = i * tq + lax.broadcasted_iota(jnp.int32, (tq, tq), 1)
    mask = kpos <= qpos
    return mask if cols == 1 else jnp.concatenate([mask] * cols, axis=1)


def _mla_prompt_body(q_ref, kk_ref, vt_ref, o_ref, m_ref, l_ref, acc_ref, *, tq):
    i = pl.program_id(2)

    def load_k(j):
        return kk_ref[0, pl.ds(pl.multiple_of(j * tq, tq), tq), :]

    def load_v(j):
        return vt_ref[0, 0, j]

    def mask_fn2(j, diag):
        return _causal_mask_t(i, j, tq, 2) if diag else None

    q = q_ref[0]
    q2 = jnp.concatenate([q[:, :2 * LANES], q[:, 2 * LANES:]], axis=0)
    _flash_loop(q2, load_k, load_v, mask_fn2, 0, i, m_ref, l_ref, acc_ref, transposed=True)
    o_t = acc_ref[...] / l_ref[...]
    o_ref[0, :, :MLA_KV_RANK] = o_t[:, :tq].T.astype(BF16)
    o_ref[0, :, MLA_KV_RANK:] = o_t[:, tq:].T.astype(BF16)


def _mla_prompt_attn(q, kk, latt):
    b, t, _ = kk.shape
    tq = min(TQ, t)
    assert latt.shape[-1] == tq
    return pl.pallas_call(
        functools.partial(_mla_prompt_body, tq=tq),
        out_shape=jax.ShapeDtypeStruct((b, t, MLA_HEADS * MLA_KV_RANK), BF16),
        grid=(b, MLA_HEADS // 2, t // tq),
        in_specs=[pl.BlockSpec((1, tq, 4 * LANES), lambda b_, h, i: (b_, i, h)),
                  pl.BlockSpec((1, t, 2 * LANES), lambda b_, h, i: (b_, 0, 0)),
                  _vt_spec(latt, lambda h: 0)],
        out_specs=pl.BlockSpec((1, tq, 2 * MLA_KV_RANK), lambda b_, h, i: (b_, i, h)),
        scratch_shapes=_softmax_scratch_t(2 * tq, MLA_KV_RANK),
        compiler_params=_cparams(("arbitrary", "arbitrary", "arbitrary")),
    )(q, kk, latt)


def _new_tile_mask(m, tk, rows_per_token, dt, inclusive):
    row_t = (lax.broadcasted_iota(jnp.int32, (m, tk), 0) // rows_per_token) % dt
    key = lax.broadcasted_iota(jnp.int32, (m, tk), 1)
    return ((key <= row_t) if inclusive else (key < row_t)) & (key < dt)


def _mla_sample_body(pt_ref, q_ref, new_ref, *rest, n_dense, pps, nsteps, dt):
    pages, (o_ref, m_ref, l_ref, acc_ref) = rest[:pps], rest[pps:]
    c = pl.program_id(1)
    q = q_ref[0]
    m, tk = q.shape[0], new_ref.shape[2]

    @pl.when(c == 0)
    def _():
        _softmax_init(m_ref, l_ref, acc_ref)
        nw = new_ref[0].astype(BF16)
        _softmax_tile(_dot(q, nw), _new_tile_mask(m, tk, MLA_HEADS, dt, True), nw[:MLA_KV_RANK], m_ref, l_ref, acc_ref, True)

    pgs = [pages[s][0, 0].astype(BF16) for s in range(pps)]
    _softmax_pages(q, pgs, [pg[:MLA_KV_RANK] for pg in pgs], None, m_ref, l_ref, acc_ref)

    @pl.when(c == nsteps - 1)
    def _():
        o_ref[0] = (acc_ref[...] / l_ref[...]).astype(BF16)


def _mla_sample_attn(q, new_rows, cache, layer, page_table):
    b, dt, _ = q.shape
    feat = MLA_KV_RANK + MLA_ROPE_DIM
    q160 = q.reshape(b, dt * MLA_HEADS, 2 * LANES)[:, :, :feat]
    m = dt * MLA_HEADS
    (o,) = _paged_call(
        functools.partial(_mla_sample_body, dt=dt), page_table, cache, layer, (feat, 0),
        [q160, _as_page(new_rows, cache.shape[3])], [(m, MLA_KV_RANK, BF16)], _softmax_scratch(m, MLA_KV_RANK))
    return o.reshape(b, dt, MLA_HEADS * MLA_KV_RANK)


def _out_mla_body(x_ref, g_ref, o_ref, z_ref, wuv_ref, w_ref, lng_ref, lnb_ref, y_ref):
    _finish(_dot(o_ref[0], wuv_ref[...]), z_ref, x_ref, g_ref, w_ref, lng_ref, lnb_ref, y_ref)


def _mla_out_proj(x, gate, o_lat, z, w_uv, w_out, ln_g, ln_b):
    d = x.shape[-1]
    eye = jnp.eye(MLA_HEADS, dtype=BF16)
    wuv_bd = (w_uv.astype(BF16).transpose(1, 0, 2)[:, :, None, :] * eye[:, None, :, None]).reshape(
        MLA_HEADS * MLA_KV_RANK, MLA_HEADS * MLA_V_DIM)
    (y,) = _row_call(_out_mla_body, x, [gate], [o_lat, z],
                     [wuv_bd, w_out.astype(BF16), ln_g.reshape(1, d), ln_b.reshape(1, d)], [(d, F32)], ROW_TILE)
    return y


def _mla_layer(xp, xs, mod_p, mod_s, pos_p, pos_s, w_in, q_norm, w_q_up, kv_norm, w_uk, w_uv, w_out, ln_g, ln_b,
               cache, layer, page_table):
    db, dt, d = xs.shape
    sh_p, sc_p, gt_p = _split_mod(mod_p)
    sh_s, sc_s, gt_s = _split_mod(mod_s, dt)
    xs_flat = xs.reshape(1, db * dt, d)
    tab_p = _rope_tables(pos_p, MLA_ROPE_DIM, MLA_ROPE_THETA, MLA_ROPE_DIM)
    tab_s = _rope_tables(jnp.tile(pos_s, db), MLA_ROPE_DIM, MLA_ROPE_THETA, MLA_ROPE_DIM)
    cq_p, rows_p, kk_p, z_p, latt_p = _mla_proj(xp, sh_p, sc_p, tab_p, w_in, kv_norm)
    cq_s, rows_s, _, z_s, _ = _mla_proj(xs_flat, sh_s, sc_s, tab_s, w_in, kv_norm)
    q_p = _mla_q(cq_p, tab_p, q_norm, w_q_up, w_uk)
    q_s = _mla_q(cq_s, tab_s, q_norm, w_q_up, w_uk)
    o_p = _mla_prompt_attn(q_p, kk_p, latt_p)
    rows_s = rows_s.reshape(db, dt, -1)
    o_s = _mla_sample_attn(q_s.reshape(db, dt, -1), rows_s, cache, layer, page_table)
    yp = _mla_out_proj(xp, gt_p, o_p, z_p, w_uv, w_out, ln_g, ln_b)
    ys = _mla_out_proj(xs_flat, gt_s, o_s.reshape(1, db * dt, -1), z_s, w_uv, w_out, ln_g, ln_b)
    return yp, ys.reshape(db, dt, d), rows_p, rows_s


def _diff_proj_body(x_ref, sh_ref, sc_ref, c_ref, s1_ref, s2_ref, wq_ref, wk_ref, wv_ref, wz_ref,
                    q_ref, kv_ref, z_ref, vt_ref):
    h = _modulated(x_ref, sh_ref, sc_ref)
    c, s1, s2 = c_ref[0], s1_ref[0], s2_ref[0]
    half = DIFF_HEAD_DIM // 8
    q = _dot(h, wq_ref[...])
    for ch in range(q.shape[1] // LANES):
        qc = _rope128(q[:, ch * LANES:(ch + 1) * LANES], c, s1, s2, half)
        q_ref[0, :, ch * LANES:(ch + 1) * LANES] = (qc * (DIFF_HEAD_DIM ** -0.5)).astype(BF16)
    k = _dot(h, wk_ref[...])
    nk = k.shape[1]
    for ch in range(nk // LANES):
        kv_ref[0, :, ch * LANES:(ch + 1) * LANES] = _rope128(k[:, ch * LANES:(ch + 1) * LANES], c, s1, s2, half)
    v = _dot(h, wv_ref[...])
    kv_ref[0, :, nk:] = v
    _store_transposed(vt_ref, v)
    z_ref[0] = _dot(h, wz_ref[...])


def _diff_proj(x, sh, sc, tables, w_in):
    hd, g, d = DIFF_HEADS, DIFF_KV_HEADS, DIFF_HEAD_DIM
    n1 = hd * 2 * d
    n2 = n1 + g * 2 * d
    n3 = n2 + g * 2 * d
    w = w_in.astype(BF16)
    return _row_call(_diff_proj_body, x, [sh, sc], list(tables), [w[:, :n1], w[:, n1:n2], w[:, n2:n3], w[:, n3:]],
                     [(n1, BF16), (n3 - n1, F32), (w.shape[1] - n3, F32), ("t", (n3 - n2) // LANES)], ROW_TILE)


def _diff_lambda(lam, lam_init):
    a = jnp.sum(lam[0:1] * lam[1:2], axis=-1, keepdims=True)
    b = jnp.sum(lam[2:3] * lam[3:4], axis=-1, keepdims=True)
    return jnp.exp(a) - jnp.exp(b) + lam_init


def _diff_combine(acc, l, lam, sub, lam_init, n):
    o = acc[:n] / l[:n] - lam * (acc[n:] / l[n:])
    return _rms(o, sub) * (1.0 - lam_init)


def _diff_prompt_body(q_ref, k_ref, vt_ref, lam_ref, sub_ref, o_ref, m_ref, l_ref, acc_ref, *, tq, lam_init):
    i = pl.program_id(2)
    lane = lax.broadcasted_iota(jnp.int32, (tq, LANES), 1)
    q = q_ref[0]
    zero = jnp.zeros_like(q)
    q2 = jnp.concatenate([jnp.where(lane < DIFF_HEAD_DIM, q, zero), jnp.where(lane >= DIFF_HEAD_DIM, q, zero)], axis=0)

    def load_k(j):
        return k_ref[0, pl.ds(pl.multiple_of(j * tq, tq), tq), :].astype(BF16)

    def load_v(j):
        return vt_ref[0, 0, j]

    def mask_fn(j, diag):
        return _causal_mask_t(i, j, tq, 2) if diag else None

    _flash_loop(q2, load_k, load_v, mask_fn, 0, i, m_ref, l_ref, acc_ref, transposed=True)
    acc, l = acc_ref[...], l_ref[...]
    o_t = acc[:, :tq] / l[:, :tq] - _diff_lambda(lam_ref[...], lam_init) * (acc[:, tq:] / l[:, tq:])
    o_ref[0] = _rms(o_t.T, sub_ref[...]) * (1.0 - lam_init)


def _diff_prompt_attn(q, kv, vt, lam, subln, lam_init):
    b, t, _ = q.shape
    tq = min(TQ, t)
    assert vt.shape[-1] == tq
    hpg = DIFF_HEADS // DIFF_KV_HEADS
    return pl.pallas_call(
        functools.partial(_diff_prompt_body, tq=tq, lam_init=lam_init),
        out_shape=jax.ShapeDtypeStruct((b, t, DIFF_HEADS * LANES), F32),
        grid=(b, DIFF_HEADS, t // tq),
        in_specs=[pl.BlockSpec((1, tq, LANES), lambda b_, n, i: (b_, i, n)),
                  pl.BlockSpec((1, t, LANES), lambda b_, n, i: (b_, 0, n // hpg)),
                  _vt_spec(vt, lambda n: n // hpg),
                  pl.BlockSpec(lam.shape, lambda b_, n, i: (0, 0)),
                  pl.BlockSpec((1, LANES), lambda b_, n, i: (0, 0))],
        out_specs=pl.BlockSpec((1, tq, LANES), lambda b_, n, i: (b_, i, n)),
        scratch_shapes=_softmax_scratch_t(2 * tq, LANES),
        compiler_params=_cparams(("arbitrary", "arbitrary", "arbitrary")),
    )(q, kv, vt, lam, subln.reshape(1, LANES))


def _diff_sample_body(pt_ref, q_ref, new_ref, lam_ref, sub_ref, *rest, n_dense, pps, nsteps, dt, lam_init):
    pages, (o_ref, m_ref, l_ref, acc_ref) = rest[:pps], rest[pps:]
    c = pl.program_id(1)
    g = DIFF_KV_HEADS
    stride = 2 * g
    nrow = new_ref.shape[1]
    q = q_ref[0]
    m = q.shape[0]
    mg = m // g
    rows = [slice(gi * mg, (gi + 1) * mg) for gi in range(g)]

    def tiles(pgs, pos_ok):
        n = len(pgs)
        pb = [pg.astype(BF16) for pg in pgs]
        s = jnp.concatenate([_dot_nt(q, p) for p in pb], axis=1)
        col = lax.broadcasted_iota(jnp.int32, (m, n * nrow), 1)
        row_g = lax.broadcasted_iota(jnp.int32, (m, n * nrow), 0) // mg
        mask = (col % stride) == row_g
        if pos_ok is not None:
            mask = mask & pos_ok
        s = jnp.where(mask, s, NEG_INF)
        m_prev = m_ref[...]
        m_new = jnp.maximum(m_prev, jnp.max(s, axis=-1, keepdims=True))
        p = jnp.where(mask, jnp.exp(s - m_new), 0.0)
        scale = jnp.exp(m_prev - m_new)
        l_ref[...] = scale * l_ref[...] + jnp.sum(p, axis=-1, keepdims=True)
        m_ref[...] = m_new
        pv = jnp.zeros(acc_ref.shape, F32)
        for k in range(n):
            pk = pltpu.roll(p[:, k * nrow:(k + 1) * nrow], g, 1).astype(BF16)
            pv = pv + _dot(pk, pb[k])
        acc_ref[...] = scale * acc_ref[...] + pv

    @pl.when(c == 0)
    def _():
        _softmax_init(m_ref, l_ref, acc_ref)
        pos = lax.broadcasted_iota(jnp.int32, (m, nrow), 1) // stride
        tok = _row_token(m, nrow, DIFF_HEADS // g, dt)
        tiles([new_ref[0]], (pos <= tok) & (pos < dt))

    tiles([pages[s][0, 0] for s in range(pps)], None)

    @pl.when(c == nsteps - 1)
    def _():
        lam = _diff_lambda(lam_ref[0], lam_init)
        n = mg // 2
        for gi in range(g):
            r = rows[gi]
            o_ref[0, gi * n:(gi + 1) * n, :] = _diff_combine(acc_ref[r, :], l_ref[r, :], lam, sub_ref[0], lam_init, n)


def _diff_sample_attn(q, new_kv, cache, layer, page_table, lam, subln, lam_init):
    b, dt, _ = q.shape
    g, d = DIFF_KV_HEADS, DIFF_HEAD_DIM
    n = DIFF_HEADS // g
    page = cache.shape[2] // (2 * g)
    q6 = q.reshape(b, dt, g, n, 2, d).transpose(0, 2, 4, 1, 3, 5)
    eye = jnp.eye(2, dtype=q.dtype)
    qrows = (q6[..., None, :] * eye[None, None, :, None, None, :, None]).reshape(b, g * 2 * dt * n, 2 * d)
    new_page = jnp.pad(new_kv.reshape(b, dt, 2 * g, LANES), ((0, 0), (0, page - dt), (0, 0), (0, 0)))
    new_page = new_page.reshape(b, page * 2 * g, LANES)
    m = qrows.shape[1]
    (o,) = _paged_call(
        functools.partial(_diff_sample_body, dt=dt, lam_init=lam_init), page_table, cache, layer,
        (cache.shape[2], 0), [qrows, new_page, jnp.broadcast_to(lam, (b,) + lam.shape),
                              jnp.broadcast_to(subln.reshape(1, 1, LANES), (b, 1, LANES))],
        [(m // 2, LANES, F32)], _softmax_scratch(m, LANES))
    return o.reshape(b, g, dt, n, LANES).transpose(0, 2, 1, 3, 4).reshape(b, dt, DIFF_HEADS * LANES)


def _diff_layer(xp, xs, mod_p, mod_s, pos_p, pos_s, layer_idx, w_in, lam, subln, w_out, ln_g, ln_b,
                cache, layer, page_table):
    db, dt, d = xs.shape
    lam_init = 0.8 - 0.6 * math.exp(-0.3 * layer_idx)
    sh_p, sc_p, gt_p = _split_mod(mod_p)
    sh_s, sc_s, gt_s = _split_mod(mod_s, dt)
    xs_flat = xs.reshape(1, db * dt, d)
    tab_p = _rope_tables(pos_p, DIFF_HEAD_DIM // 4, ROPE_THETA, DIFF_HEAD_DIM)
    tab_s = _rope_tables(jnp.tile(pos_s, db), DIFF_HEAD_DIM // 4, ROPE_THETA, DIFF_HEAD_DIM)
    q_p, kv_p, z_p, vt_p = _diff_proj(xp, sh_p, sc_p, tab_p, w_in)
    q_s, kv_s, z_s, _ = _diff_proj(xs_flat, sh_s, sc_s, tab_s, w_in)
    o_p = _diff_prompt_attn(q_p, kv_p, vt_p, lam, subln, lam_init)
    kv_s = kv_s.reshape(db, dt, -1)
    nl, pool, page = cache.shape[:3]
    cache_rows = cache.reshape(nl, pool, page * 2 * DIFF_KV_HEADS, LANES)
    o_s = _diff_sample_attn(q_s.reshape(db, dt, -1), kv_s, cache_rows, layer, page_table, lam, subln, lam_init)
    yp = _out_proj(xp, gt_p, o_p, z_p, w_out, ln_g, ln_b)
    ys = _out_proj(xs_flat, gt_s, o_s.reshape(1, db * dt, -1), z_s, w_out, ln_g, ln_b)
    rows_shape = (2, DIFF_KV_HEADS, 2 * DIFF_HEAD_DIM)
    return yp, ys.reshape(db, dt, d), kv_p.reshape(kv_p.shape[:2] + rows_shape), kv_s.reshape((db, dt) + rows_shape)


NSA_HPG = NSA_HEADS // NSA_KV_GROUPS
GL_PAD = LANES


def _nsa_proj_body(x_ref, sh_ref, sc_ref, c_ref, s1_ref, s2_ref, wq_ref, wkv_ref, ww_ref, wg_ref, wz_ref,
                   q_ref, kv_ref, w_ref, gl_ref, z_ref, vst_ref, vwt_ref):
    h = _modulated(x_ref, sh_ref, sc_ref)
    c, s1, s2 = c_ref[0], s1_ref[0], s2_ref[0]
    half = NSA_HEAD_DIM // 8
    q = _dot(h, wq_ref[...])
    for ch in range(q.shape[1] // LANES):
        qc = _rope128(q[:, ch * LANES:(ch + 1) * LANES], c, s1, s2, half)
        q_ref[0, :, ch * LANES:(ch + 1) * LANES] = (qc * (NSA_HEAD_DIM ** -0.5)).astype(BF16)
    kv = _dot(h, wkv_ref[...])
    kv_ref[0, :, :2 * LANES] = kv[:, :2 * LANES]
    kv_ref[0, :, 2 * LANES:3 * LANES] = _rope128(kv[:, 2 * LANES:3 * LANES], c, s1, s2, half)
    kv_ref[0, :, 3 * LANES:] = kv[:, 3 * LANES:]
    _store_transposed(vst_ref, kv[:, 3 * LANES:])
    w = _dot(h, ww_ref[...])
    w_ref[0, :, :LANES] = _rope128(w[:, :LANES], c, s1, s2, half)
    w_ref[0, :, LANES:] = w[:, LANES:]
    _store_transposed(vwt_ref, w[:, LANES:])
    gl_ref[0] = _dot(h, wg_ref[...])
    z_ref[0] = _dot(h, wz_ref[...])


def _nsa_proj(x, sh, sc, tables, w_in):
    hd, g, d = NSA_HEADS, NSA_KV_GROUPS, NSA_HEAD_DIM
    n1 = hd * d
    n2 = n1 + 4 * g * d
    n3 = n2 + 2 * g * d
    n4 = n3 + 3 * hd
    w = w_in.astype(BF16)
    wg = jnp.pad(w[:, n3:n4], ((0, 0), (0, GL_PAD - 3 * hd)))
    return _row_call(_nsa_proj_body, x, [sh, sc], list(tables), [w[:, :n1], w[:, n1:n2], w[:, n2:n3], wg, w[:, n4:]],
                     [(n1, BF16), (n2 - n1, F32), (n3 - n2, F32), (GL_PAD, F32), (w.shape[1] - n4, F32),
                      ("t", 1), ("t", 1)], ROW_TILE)


def _cmp_weights(pe, w1, b1, w2, b2):
    g = NSA_KV_GROUPS
    eye = jnp.eye(g, dtype=BF16)
    w1b = w1.astype(BF16)
    bd = (w1b[:, :, None, :, None, :] * eye[None, None, :, None, :, None]).reshape(
        2, CMP_BLOCK, g * NSA_HEAD_DIM, g * CMP_HIDDEN)
    wcat = jnp.concatenate([bd[:, :CMP_STRIDE], bd[:, CMP_STRIDE:]], axis=3)
    wcat = wcat.reshape(2, CMP_STRIDE // 2, 2 * g * NSA_HEAD_DIM, 2 * g * CMP_HIDDEN)
    w1f = w1b.reshape(2, CMP_BLOCK * NSA_HEAD_DIM, CMP_HIDDEN)
    pef = jnp.pad(pe.astype(BF16).reshape(2, 1, CMP_BLOCK * NSA_HEAD_DIM), ((0, 0), (0, 7), (0, 0)))
    w2b = w2.astype(BF16)
    w2bd = (w2b[:, None, :, None, :] * eye[None, :, None, :, None]).reshape(2, g * CMP_HIDDEN, g * NSA_HEAD_DIM)
    b2t = jnp.tile(b2, (1, g)).reshape(2, 1, g * NSA_HEAD_DIM)
    return [wcat, w1f, pef, b1.reshape(2, 1, CMP_HIDDEN), w2bd, b2t]


def _compress(get_rows, nblk, wcat_ref, w1f_ref, pe_ref, b1_ref, w2_ref, b2_ref):
    outs = []
    nh = NSA_KV_GROUPS * CMP_HIDDEN
    for comp in range(2):
        acc = jnp.zeros((nblk, 2 * nh), F32)
        for pp in range(CMP_STRIDE // 2):
            x = jnp.concatenate([get_rows(comp, 2 * pp), get_rows(comp, 2 * pp + 1)], axis=1).astype(BF16)
            acc = acc + _dot(x, wcat_ref[comp, pp])
        bias = _dot(pe_ref[comp], w1f_ref[comp])[0:1] + b1_ref[comp]
        bias = jnp.concatenate([bias] * NSA_KV_GROUPS, axis=1)
        hid = _silu(acc[:, :nh] + pltpu.roll(acc[:, nh:], nblk - 1, 0) + bias).astype(BF16)
        outs.append(_dot(hid, w2_ref[comp]) + b2_ref[comp])
    return outs


def _nsa_compress_body(kc_ref, vc_ref, wcat_ref, w1f_ref, pe_ref, b1_ref, w2_ref, b2_ref, k_ref, v_ref, *, nblk):
    def get_rows(comp, p):
        return (kc_ref, vc_ref)[comp][0, pl.ds(p, nblk, stride=CMP_STRIDE), :]
    k, v = _compress(get_rows, nblk, wcat_ref, w1f_ref, pe_ref, b1_ref, w2_ref, b2_ref)
    k_ref[0] = k
    v_ref[0] = v


def _nsa_compress(kv, cw):
    b, t, f = kv.shape
    nblk = t // CMP_STRIDE
    full = lambda a: pl.BlockSpec(a.shape, lambda b_: (0,) * a.ndim)
    return pl.pallas_call(
        functools.partial(_nsa_compress_body, nblk=nblk),
        out_shape=[jax.ShapeDtypeStruct((b, nblk, LANES), F32)] * 2,
        grid=(b,),
        in_specs=[pl.BlockSpec((1, t, LANES), lambda b_: (b_, 0, 0)),
                  pl.BlockSpec((1, t, LANES), lambda b_: (b_, 0, 1))] + [full(a) for a in cw],
        out_specs=[pl.BlockSpec((1, nblk, LANES), lambda b_: (b_, 0, 0))] * 2,
        compiler_params=_cparams(("arbitrary",)),
    )(kv, kv, *cw)


def _place_in_half(x, src_half, dst_half):
    return jnp.where(dst_half == src_half, x, pltpu.roll(x, LANES // 2, 1))


def _pair_queries(q, dst_half):
    q = q.astype(F32)
    lane = lax.broadcasted_iota(jnp.int32, q.shape, 1)
    keep = (lane // (LANES // 2)) == dst_half
    qa = jnp.where(keep, _place_in_half(q, 0, dst_half), 0.0)
    qb = jnp.where(keep, _place_in_half(q, 1, dst_half), 0.0)
    return jnp.concatenate([qa, qb], axis=0).astype(BF16)


def _pair_outputs(o2, src_half):
    tq = o2.shape[0] // 2
    lane = lax.broadcasted_iota(jnp.int32, (tq, LANES), 1)
    return jnp.where(lane < LANES // 2, _place_in_half(o2[:tq], src_half, 0), _place_in_half(o2[tq:], src_half, 1))


def _select_blocks(imp, cur, n_sel, k_eff):
    sp, n = imp.shape
    blk = lax.broadcasted_iota(jnp.int32, (sp, n), 0)
    blkf = blk.astype(F32)
    forced = (blk == 0) | (blk == cur) | (blk == cur - 1)
    allowed = (blk <= cur) & (blk < n_sel)
    score = jnp.where(blk > cur, NEG_INF, jnp.where(forced, FORCE_SCORE, imp))
    lowest = -3.0e38
    score = jnp.where(blk < n_sel, score, lowest)
    sel = jnp.zeros((sp, n), F32)
    for _ in range(k_eff):
        mx = jnp.max(score, axis=0, keepdims=True)
        first = jnp.min(jnp.where(score == mx, blkf, float(sp)), axis=0, keepdims=True)
        hit = blkf == first
        sel = jnp.where(hit, 1.0, sel)
        score = jnp.where(hit, lowest, score)
    return jnp.where(allowed, sel, 0.0)


def _overlap_t(n_sel_pad, n_cmp_pad, n_sel, n_cmp):
    ci = np.arange(n_cmp_pad)[None, :] * CMP_STRIDE
    sj = np.arange(n_sel_pad)[:, None] * SEL_BLOCK
    ov = (ci < sj + SEL_BLOCK) & (ci + CMP_BLOCK > sj)
    ov &= (np.arange(n_cmp_pad)[None, :] < n_cmp) & (np.arange(n_sel_pad)[:, None] < n_sel)
    return jnp.asarray(ov, BF16)


def _cmp_softmax(s, mask):
    s = jnp.where(mask, s, NEG_INF)
    p = jnp.where(mask, jnp.exp(s - jnp.max(s, axis=-1, keepdims=True)), 0.0)
    l = jnp.sum(p, axis=-1, keepdims=True)
    return p / jnp.where(l > 0.0, l, 1.0)


def _nsa_cmpsel_body(q_ref, k_ref, v_ref, ov_ref, o_ref, sel_ref, *, tq, n_sel, k_eff):
    g = pl.program_id(1)
    i = pl.program_id(2)
    nblk = k_ref.shape[1]
    nch = NSA_HPG // 2
    q2s = [_pair_queries(q_ref[0, :, ch * LANES:(ch + 1) * LANES], g) for ch in range(nch)]
    qall = jnp.concatenate(q2s, axis=0)
    s = _dot_nt(qall, k_ref[0].astype(BF16))
    m = qall.shape[0]
    qpos = i * tq + lax.broadcasted_iota(jnp.int32, (m, nblk), 0) % tq
    cmp_end = lax.broadcasted_iota(jnp.int32, (m, nblk), 1) * CMP_STRIDE + (CMP_BLOCK - 1)
    p = _cmp_softmax(s, cmp_end <= qpos)
    o = _dot(p.astype(BF16), v_ref[0].astype(BF16))
    for ch in range(nch):
        o_ref[0, :, ch * LANES:(ch + 1) * LANES] = _pair_outputs(o[2 * ch * tq:(2 * ch + 2) * tq], g)
    psum = p[:tq]
    for n in range(1, NSA_HPG):
        psum = psum + p[n * tq:(n + 1) * tq]
    hi = psum.astype(BF16)
    lo = (psum - hi.astype(F32)).astype(BF16)
    imp_t = _dot_nt(ov_ref[...], hi) + _dot_nt(ov_ref[...], lo)
    sp = imp_t.shape[0]
    cur = (i * tq + lax.broadcasted_iota(jnp.int32, (1, tq), 1)) // SEL_BLOCK
    sel_t = _select_blocks(imp_t, cur, n_sel, k_eff)
    if sp < LANES:
        sel_t = jnp.concatenate([sel_t, jnp.zeros((LANES - sp, tq), F32)], axis=0)
    sel_ref[0, 0] = sel_t.astype(BF16)


def _nsa_cmpsel(q, kcmp, vcmp):
    b, t, f = q.shape
    tq = min(TQ, t)
    nblk = kcmp.shape[1]
    n_sel = -(-t // SEL_BLOCK)
    assert n_sel <= LANES
    sp = -(-n_sel // 8) * 8
    ov = _overlap_t(sp, nblk, n_sel, nblk - 1)
    gw = f // NSA_KV_GROUPS
    return pl.pallas_call(
        functools.partial(_nsa_cmpsel_body, tq=tq, n_sel=n_sel, k_eff=min(N_SELECT, n_sel)),
        out_shape=[jax.ShapeDtypeStruct((b, t, f), F32), jax.ShapeDtypeStruct((b, NSA_KV_GROUPS, LANES, t), BF16)],
        grid=(b, NSA_KV_GROUPS, t // tq),
        in_specs=[pl.BlockSpec((1, tq, gw), lambda b_, g, i: (b_, i, g)),
                  pl.BlockSpec((1, nblk, LANES), lambda b_, g, i: (b_, 0, 0)),
                  pl.BlockSpec((1, nblk, LANES), lambda b_, g, i: (b_, 0, 0)),
                  pl.BlockSpec(ov.shape, lambda b_, g, i: (0, 0))],
        out_specs=[pl.BlockSpec((1, tq, gw), lambda b_, g, i: (b_, i, g)),
                   pl.BlockSpec((1, 1, LANES, tq), lambda b_, g, i: (b_, g, 0, i))],
        compiler_params=_cparams(("arbitrary", "arbitrary", "arbitrary")),
    )(q, kcmp, vcmp, ov)


def _block_expand(base_pos, n_blocks_pad, tk):
    blk = lax.broadcasted_iota(jnp.int32, (n_blocks_pad, tk), 0)
    pos = base_pos + lax.broadcasted_iota(jnp.int32, (n_blocks_pad, tk), 1)
    return (blk == pos // SEL_BLOCK).astype(BF16)


def _nsa_prompt_body(*refs, tq, mode):
    if mode == "sel":
        q_ref, k_ref, vt_ref, sel_ref, o_ref, m_ref, l_ref, acc_ref = refs
    else:
        q_ref, k_ref, vt_ref, o_ref, m_ref, l_ref, acc_ref = refs
    c = pl.program_id(1)
    i = pl.program_id(2)
    g = c // (NSA_HPG // 2)
    q2 = _pair_queries(q_ref[0], g)
    koff = lax.broadcasted_iota(jnp.int32, (tq, tq), 0)
    qpos = i * tq + lax.broadcasted_iota(jnp.int32, (tq, tq), 1)
    if mode == "sel":
        sel_t = sel_ref[0, 0]

    def load_k(j):
        return k_ref[0, pl.ds(pl.multiple_of(j * tq, tq), tq), :].astype(BF16)

    def load_v(j):
        return vt_ref[0, 0, j]

    def mask_fn(j, diag):
        start = j * tq
        if mode == "sel":
            blk = lax.broadcasted_iota(jnp.int32, (tq, LANES), 1)
            expand_t = (blk == (start + lax.broadcasted_iota(jnp.int32, (tq, LANES), 0)) // SEL_BLOCK).astype(BF16)
            hit = _dot(expand_t, sel_t)
            if not diag:
                bias = (hit - 1.0) * 1e30
                return jnp.concatenate([bias, bias], axis=1)
            mask = (hit > 0.5) & (start + koff <= qpos)
        else:
            mask = (start + koff <= qpos) & (qpos - (start + koff) < WINDOW)
        return jnp.concatenate([mask, mask], axis=1)

    j_lo = 0 if mode == "sel" else jnp.maximum(i - (WINDOW + tq - 1) // tq, 0)
    _flash_loop(q2, load_k, load_v, mask_fn, j_lo, i, m_ref, l_ref, acc_ref, transposed=True)
    o_t = acc_ref[...] / l_ref[...]
    o_ref[0] = _pair_outputs(jnp.concatenate([o_t[:, :tq].T, o_t[:, tq:].T], axis=0), g)


def _nsa_prompt_attn(q, kv, k_chunk, vt, sel=None):
    b, t, f = q.shape
    tq = min(TQ, t)
    assert vt.shape[-1] == tq
    cpg = NSA_HPG // 2
    in_specs = [pl.BlockSpec((1, tq, LANES), lambda b_, c, i: (b_, i, c)),
                pl.BlockSpec((1, t, LANES), lambda b_, c, i: (b_, 0, k_chunk)),
                _vt_spec(vt, lambda c: 0)]
    args = [q, kv, vt]
    if sel is not None:
        in_specs.append(pl.BlockSpec((1, 1, LANES, tq), lambda b_, c, i: (b_, c // cpg, 0, i)))
        args.append(sel)
    return pl.pallas_call(
        functools.partial(_nsa_prompt_body, tq=tq, mode="sel" if sel is not None else "win"),
        out_shape=jax.ShapeDtypeStruct((b, t, f), F32),
        grid=(b, f // LANES, t // tq),
        in_specs=in_specs,
        out_specs=pl.BlockSpec((1, tq, LANES), lambda b_, c, i: (b_, i, c)),
        scratch_shapes=_softmax_scratch_t(2 * tq, LANES),
        compiler_params=_cparams(("arbitrary", "arbitrary", "arbitrary")),
    )(*args)


def _out_nsa_body(x_ref, g_ref, oc_ref, os_ref, ow_ref, gl_ref, z_ref, e_ref, w_ref, lng_ref, lnb_ref, y_ref):
    sig = _sigmoid(gl_ref[0])
    o = (_split_dot(sig, e_ref[0]) * oc_ref[0] + _split_dot(sig, e_ref[1]) * os_ref[0]
         + _split_dot(sig, e_ref[2]) * ow_ref[0])
    _finish(o, z_ref, x_ref, g_ref, w_ref, lng_ref, lnb_ref, y_ref)


def _nsa_out_proj(x, gate, o_c, o_s, o_w, gl, z, w_out, ln_g, ln_b):
    d = x.shape[-1]
    e = np.zeros((3, GL_PAD, NSA_HEADS * NSA_HEAD_DIM), np.float32)
    for br in range(3):
        for h in range(NSA_HEADS):
            e[br, 3 * h + br, h * NSA_HEAD_DIM:(h + 1) * NSA_HEAD_DIM] = 1.0
    (y,) = _row_call(_out_nsa_body, x, [gate], [o_c, o_s, o_w, gl, z],
                     [jnp.asarray(e, BF16), w_out.astype(BF16), ln_g.reshape(1, d), ln_b.reshape(1, d)],
                     [(d, F32)], ROW_TILE)
    return y


def _row_token(m, n, rows_per_token, dt):
    return (lax.broadcasted_iota(jnp.int32, (m, n), 0) // rows_per_token) % dt


def _nsa_sample_cmp_body(pt_ref, q_ref, new_ref, ov_ref, wcat_ref, w1f_ref, pe_ref, b1_ref, w2_ref, b2_ref, *rest,
                         n_dense, pps, nsteps, dt, past, nblk):
    pages, (oc_ref, imp_ref, xk_ref, xv_ref) = rest[:pps], rest[pps:]
    c = pl.program_id(1)
    page = pages[0].shape[3]

    @pl.when(c == 0)
    def _():
        tail = xk_ref.shape[0] - past
        nw = new_ref[0]
        for x_ref, lo in ((xk_ref, 0), (xv_ref, LANES)):
            x_ref[pl.ds(past, tail), :] = jnp.zeros((tail, LANES), F32)
            x_ref[pl.ds(past, nw.shape[0]), :] = nw[:, lo:lo + LANES]

    for s in range(pps):
        pg = pages[s][0, 0]
        base = pl.multiple_of((c * pps + s) * page, page)
        xk_ref[pl.ds(base, page), :] = pg[:LANES].T
        xv_ref[pl.ds(base, page), :] = pg[LANES:].T

    @pl.when(c == nsteps - 1)
    def _():
        def get_rows(comp, p):
            return (xk_ref, xv_ref)[comp][pl.ds(p, nblk, stride=CMP_STRIDE), :]
        k, v = _compress(get_rows, nblk, wcat_ref, w1f_ref, pe_ref, b1_ref, w2_ref, b2_ref)
        q = q_ref[0]
        m = q.shape[0]
        s = _dot_nt(q, k.astype(BF16))
        qpos = past + _row_token(m, nblk, NSA_HPG, dt)
        cmp_end = lax.broadcasted_iota(jnp.int32, (m, nblk), 1) * CMP_STRIDE + (CMP_BLOCK - 1)
        p = _cmp_softmax(s, cmp_end <= qpos)
        oc_ref[0] = _dot(p.astype(BF16), v.astype(BF16))
        psum = jnp.sum(p.reshape(m // NSA_HPG, NSA_HPG, nblk), axis=1)
        imp_ref[0] = _split_dot(psum, ov_ref[...])


def _select_body(imp_ref, cur_ref, sel_ref, *, n_sel, k_eff):
    sel_ref[...] = _select_blocks(imp_ref[...], cur_ref[...], n_sel, k_eff)


def _nsa_sample_select(imp, past, dt, n_sel):
    b, r, spl = imp.shape
    sp = -(-n_sel // 8) * 8
    n = b * r
    npad = -(-n // LANES) * LANES
    imp_t = jnp.pad(imp.reshape(n, spl)[:, :sp].T, ((0, 0), (0, npad - n)))
    cur = jnp.tile((past + jnp.arange(dt, dtype=jnp.int32)) // SEL_BLOCK, n // dt)
    cur = jnp.pad(cur, (0, npad - n)).reshape(1, npad)
    sel_t = pl.pallas_call(
        functools.partial(_select_body, n_sel=n_sel, k_eff=min(N_SELECT, n_sel)),
        out_shape=jax.ShapeDtypeStruct((sp, npad), F32),
        compiler_params=pltpu.CompilerParams(vmem_limit_bytes=VMEM_LIMIT),
    )(imp_t, cur)
    return jnp.pad(sel_t[:, :n].T, ((0, 0), (0, spl - sp))).reshape(b, r, spl)


def _nsa_sample_sel_body(pt_ref, q_ref, sel_ref, new_ref, st_ref, neww_ref, *rest, n_dense, pps, nsteps, dt, past):
    pages, (os_ref, ow_ref, m_ref, l_ref, acc_ref, m2_ref, l2_ref, acc2_ref) = rest[:pps], rest[pps:]
    c = pl.program_id(1)
    q = q_ref[0]
    selb = sel_ref[0]
    m, spl = selb.shape
    page = pages[0].shape[3]

    def sel_tile(kv_t, base_pos, extra):
        tk = kv_t.shape[1]
        mask = _dot(selb, _block_expand(base_pos, spl, tk)) > 0.5
        if extra is not None:
            mask = mask & extra
        _softmax_tile(_dot(q, kv_t[:LANES].astype(BF16)), mask, kv_t[LANES:].astype(BF16), m_ref, l_ref, acc_ref, True)

    @pl.when(c == 0)
    def _():
        _softmax_init(m_ref, l_ref, acc_ref)
        _softmax_init(m2_ref, l2_ref, acc2_ref)
        tk = new_ref.shape[2]
        new_mask = _new_tile_mask(m, tk, NSA_HPG, dt, True)
        sel_tile(new_ref[0], past, new_mask)
        st = st_ref[0]
        wlen = st.shape[1]
        diff = (past + _row_token(m, wlen, NSA_HPG, dt)) - (past - wlen + lax.broadcasted_iota(jnp.int32, (m, wlen), 1))
        kpos_ok = (past - wlen + lax.broadcasted_iota(jnp.int32, (m, wlen), 1)) >= 0
        _softmax_tile(_dot(q, st[:LANES].astype(BF16)), (diff >= 0) & (diff < WINDOW) & kpos_ok,
                      st[LANES:].astype(BF16), m2_ref, l2_ref, acc2_ref, True)
        nw = neww_ref[0]
        diff_n = _row_token(m, tk, NSA_HPG, dt) - lax.broadcasted_iota(jnp.int32, (m, tk), 1)
        _softmax_tile(_dot(q, nw[:LANES].astype(BF16)), new_mask & (diff_n < WINDOW),
                      nw[LANES:].astype(BF16), m2_ref, l2_ref, acc2_ref, True)

    pgs = [pages[s][0, 0] for s in range(pps)]
    mask = _dot(selb, _block_expand(c * (pps * page), spl, pps * page)) > 0.5
    _softmax_pages(q, [pg[:LANES].astype(BF16) for pg in pgs], [pg[LANES:].astype(BF16) for pg in pgs], mask,
                   m_ref, l_ref, acc_ref)

    @pl.when(c == nsteps - 1)
    def _():
        os_ref[0] = acc_ref[...] / l_ref[...]
        ow_ref[0] = acc2_ref[...] / l2_ref[...]


def _nsa_sample_attn(q, new_kv, new_w, cache, win_state, layer, page_table, cw):
    b, dt, _ = q.shape
    g = NSA_KV_GROUPS
    npg, page = page_table.shape[1], cache.shape[3]
    past = npg * page
    tk_all = past + dt
    n_chunks = -(-tk_all // CMP_STRIDE)
    nblk = -(-n_chunks // LANES) * LANES
    n_sel = -(-tk_all // SEL_BLOCK)
    spl = -(-n_sel // LANES) * LANES
    qg = _group_block_diag(q, g)
    m = qg.shape[1]
    ov = _overlap_t(spl, nblk, n_sel, n_chunks - 1).T
    new_cmp = jnp.pad(new_kv[:, :, :2 * LANES], ((0, 0), (0, CMP_STRIDE - dt), (0, 0)))
    o_c, imp = _paged_call(
        functools.partial(_nsa_sample_cmp_body, dt=dt, past=past, nblk=nblk), page_table, cache, layer, (2 * LANES, 0),
        [qg, new_cmp], [(m, LANES, F32), (m // NSA_HPG, spl, F32)],
        [pltpu.VMEM((nblk * CMP_STRIDE, LANES), F32)] * 2, shared=[ov] + list(cw))
    sel = _nsa_sample_select(imp, past, dt, n_sel)
    sel_rows = jnp.repeat(sel, NSA_HPG, axis=1).astype(BF16)
    o_s, o_w = _paged_call(
        functools.partial(_nsa_sample_sel_body, dt=dt, past=past), page_table, cache, layer, (2 * LANES, 1),
        [qg, sel_rows, _as_page(new_kv[:, :, 2 * LANES:], page), win_state, _as_page(new_w, page)],
        [(m, LANES, F32), (m, LANES, F32)], _softmax_scratch(m, LANES) + _softmax_scratch(m, LANES))
    return tuple(_group_diag_extract(o, g, dt) for o in (o_c, o_s, o_w))


def _nsa_layer(xp, xs, mod_p, mod_s, pos_p, pos_s, w_in, pe, w1, b1, w2, b2, w_out, ln_g, ln_b,
               cache, win_state, layer, page_table):
    db, dt, d = xs.shape
    g, hd = NSA_KV_GROUPS, NSA_HEAD_DIM
    sh_p, sc_p, gt_p = _split_mod(mod_p)
    sh_s, sc_s, gt_s = _split_mod(mod_s, dt)
    xs_flat = xs.reshape(1, db * dt, d)
    tab_p = _rope_tables(pos_p, hd // 4, ROPE_THETA, hd)
    tab_s = _rope_tables(jnp.tile(pos_s, db), hd // 4, ROPE_THETA, hd)
    cw = _cmp_weights(pe, w1, b1, w2, b2)
    q_p, kv_p, w_p, gl_p, z_p, vst_p, vwt_p = _nsa_proj(xp, sh_p, sc_p, tab_p, w_in)
    q_s, kv_s, w_s, gl_s, z_s, _, _ = _nsa_proj(xs_flat, sh_s, sc_s, tab_s, w_in)
    kcmp, vcmp = _nsa_compress(kv_p, cw)
    oc_p, sel = _nsa_cmpsel(q_p, kcmp, vcmp)
    os_p = _nsa_prompt_attn(q_p, kv_p, 2, vst_p, sel)
    ow_p = _nsa_prompt_attn(q_p, w_p, 0, vwt_p)
    yp = _nsa_out_proj(xp, gt_p, oc_p, os_p, ow_p, gl_p, z_p, w_out, ln_g, ln_b)
    kv_s, w_s = kv_s.reshape(db, dt, -1), w_s.reshape(db, dt, -1)
    state = _page_view(win_state)[layer]
    oc_s, os_s, ow_s = _nsa_sample_attn(q_s.reshape(db, dt, -1), kv_s, w_s, cache, state, layer, page_table, cw)
    flat = lambda a: a.reshape(1, db * dt, -1)
    ys = _nsa_out_proj(xs_flat, gt_s, flat(oc_s), flat(os_s), flat(ow_s), gl_s, z_s, w_out, ln_g, ln_b)
    t = xp.shape[1]
    rows_p = kv_p.reshape(kv_p.shape[:2] + (4, g, hd))
    rows_s = kv_s.reshape(db, dt, 4, g, hd)
    win_p = w_p[:, t - min(WINDOW, t):].reshape(xp.shape[0], min(WINDOW, t), 2, g, hd)
    wrows = jnp.concatenate([win_state[layer], w_s.reshape(db, dt, 2, g, hd)], axis=1)
    win_s = wrows[:, wrows.shape[1] - min(WINDOW, wrows.shape[1]):]
    return yp, ys.reshape(db, dt, d), rows_p, rows_s, win_p, win_s


def kernel(x_prompt, x_sample, cache_sb_kv, cache_mla_latent, cache_diff_kv, cache_nsa_kv, state_nsa_window,
           page_table, c_prompt, c_sample, ada_w, ada_b, ln_g, ln_b, sb_w_in, sb_w_out,
           mla_w_in, mla_q_norm, mla_w_q_up, mla_kv_norm, mla_w_uk, mla_w_uv, mla_w_out,
           diff_w_in, diff_lam, diff_subln, diff_w_out,
           nsa_w_in, nsa_cmp_pe, nsa_cmp_w1, nsa_cmp_b1, nsa_cmp_w2, nsa_cmp_b2, nsa_w_out):
    nb = x_prompt.shape[0]
    mods = _adaln(jnp.concatenate([c_prompt, c_sample], axis=0), ada_w, ada_b)
    past = page_table.shape[1] * cache_sb_kv.shape[2]
    pos_p = jnp.arange(x_prompt.shape[1], dtype=jnp.int32)
    pos_s = past + jnp.arange(x_sample.shape[1], dtype=jnp.int32)
    xp, xs, sb_p, sb_s = _sb_layer(x_prompt, x_sample, mods[0, :nb], mods[0, nb:], sb_w_in[0], sb_w_out[0],
                                   ln_g[0], ln_b[0], _page_view(cache_sb_kv), 0, page_table)
    xp, xs, mla_p, mla_s = _mla_layer(xp, xs, mods[1, :nb], mods[1, nb:], pos_p, pos_s, mla_w_in[0], mla_q_norm[0],
                                      mla_w_q_up[0], mla_kv_norm[0], mla_w_uk[0], mla_w_uv[0], mla_w_out[0],
                                      ln_g[1], ln_b[1], _page_view(cache_mla_latent), 0, page_table)
    xp, xs, diff_p, diff_s = _diff_layer(xp, xs, mods[2, :nb], mods[2, nb:], pos_p, pos_s, 2, diff_w_in[0],
                                         diff_lam[0], diff_subln[0], diff_w_out[0], ln_g[2], ln_b[2],
                                         cache_diff_kv, 0, page_table)
    xp, xs, nsa_p, nsa_s, win_p, win_s = _nsa_layer(
        xp, xs, mods[3, :nb], mods[3, nb:], pos_p, pos_s, nsa_w_in[0], nsa_cmp_pe[0], nsa_cmp_w1[0], nsa_cmp_b1[0],
        nsa_cmp_w2[0], nsa_cmp_b2[0], nsa_w_out[0], ln_g[3], ln_b[3], _page_view(cache_nsa_kv), state_nsa_window,
        0, page_table)
    return (xp, xs, sb_p[None], sb_s[None], mla_p[None], mla_s[None], diff_p[None], diff_s[None],
            nsa_p[None], nsa_s[None], win_p[None], win_s[None])
```
